```python
import jax, jax.numpy as jnp
from jax import lax
import numpy as np

D_MODEL = 1024
BATCH = 32
SEQ = 256
DEPTH = 1
DEC_BATCH = 4
DEC_SEQ = 1024
PAST_LEN = 512

GRID_W = 64
POOL_WIDTH = 512
POOL_GROUPS = 4
POOL_GROUP_DIM = POOL_WIDTH // POOL_GROUPS
POOL_WINDOWS = (2, 4, 8, 16)
RWKV_WIDTH = 512
RWKV_HEAD_SIZE = 64
RWKV_HEADS = RWKV_WIDTH // RWKV_HEAD_SIZE
DECAY_LORA = 64
AAA_LORA = 64
GATE_LORA = 128
N_EXPERTS = 32
TOP_K = 4
D_FF = 1024
SWIGLU_ALPHA = 1.702
SWIGLU_LIMIT = 7.0
RMS_EPS = 1e-6
GN_EPS = 1e-5 * RWKV_HEAD_SIZE
N_MOD = 6

OFF_R = POOL_WIDTH
OFF_MERGE = OFF_R + 3 * RWKV_WIDTH + 2 * DECAY_LORA + 2 * AAA_LORA + GATE_LORA
IN_COLS = OFF_MERGE + 2 * D_MODEL
SHIFT_COLS = OFF_MERGE - OFF_R

kernel_name = "pool_birwkv7_moe_prefix_dit"


def rmsnorm(x, w):
    xf = x.astype(jnp.float32)
    y = xf * lax.rsqrt(jnp.mean(xf * xf, axis=-1, keepdims=True) + RMS_EPS)
    return (y * w).astype(x.dtype)


def modulation(cvec, w_mod, b_mod):
    m = jax.nn.silu(cvec.astype(jnp.float32)) @ w_mod + b_mod
    return jnp.split(m[:, None, :], N_MOD, axis=-1)


def shift_context(p):
    B, T, C = p.shape
    g = p.reshape(B, T, C // 2, 2)
    prev = jnp.pad(g[:, :-1, :, 0], ((0, 0), (1, 0), (0, 0)))
    nxt = jnp.pad(g[:, 1:, :, 1], ((0, 0), (0, 1), (0, 0)))
    return jnp.stack([prev, nxt], axis=-1).reshape(B, T, C)


def shift_grid(p):
    B, T, C = p.shape
    rows = T // GRID_W
    g = p.reshape(B, rows, GRID_W, C // 4, 4)
    left = jnp.pad(g[:, :, :-1, :, 0], ((0, 0), (0, 0), (1, 0), (0, 0)))
    right = jnp.pad(g[:, :, 1:, :, 1], ((0, 0), (0, 0), (0, 1), (0, 0)))
    up = jnp.pad(g[:, :-1, :, :, 2], ((0, 0), (1, 0), (0, 0), (0, 0)))
    down = jnp.pad(g[:, 1:, :, :, 3], ((0, 0), (0, 1), (0, 0), (0, 0)))
    return jnp.stack([left, right, up, down], axis=-1).reshape(B, T, C)


def pool_branch(u, pool_w, pool_scale):
    B, T, _ = u.shape
    uf = u.astype(jnp.float32)
    cs = jnp.pad(jnp.cumsum(uf, axis=1), ((0, 0), (1, 0), (0, 0)))
    t = jnp.arange(T)
    outs = []
    for gi, win in enumerate(POOL_WINDOWS):
        sl = slice(gi * POOL_GROUP_DIM, (gi + 1) * POOL_GROUP_DIM)
        lo = jnp.clip(t - win // 2, 0, T)
        hi = jnp.clip(t + win // 2, 0, T)
        csg = cs[:, :, sl]
        s = jnp.take(csg, hi, axis=1) - jnp.take(csg, lo, axis=1)
        cnt = (hi - lo).astype(jnp.float32)[None, :, None]
        outs.append(s / cnt - uf[:, :, sl])
    pooled = jnp.stack(outs, axis=2)
    mixed = jnp.einsum('btgc,gcd->btgd', pooled, pool_w).reshape(B, T, POOL_WIDTH)
    return mixed * pool_scale


def wkv_scan(r, w, k, v, kk, a, s0, reverse):
    def step(S, inp):
        r_t, w_t, k_t, v_t, kk_t, a_t = inp
        sa = jnp.einsum('bhvk,bhk->bhv', S, -kk_t)
        S = (S * w_t[:, :, None, :]
             + sa[..., None] * (kk_t * a_t)[:, :, None, :]
             + v_t[..., None] * k_t[:, :, None, :])
        y = jnp.einsum('bhvk,bhk->bhv', S, r_t)
        return S, y
    xs = tuple(jnp.moveaxis(z, 1, 0) for z in (r, w, k, v, kk, a))
    S, ys = lax.scan(step, s0.astype(jnp.float32), xs, reverse=reverse)
    return jnp.moveaxis(ys, 0, 1), S


def rwkv_branch(p, shift_fn, s0_f, s0_b, lp):
    B, T, _ = p.shape
    pf = p.astype(jnp.float32)
    pm = pf + lp['shift_mu'] * (shift_fn(pf) - pf)

    def heads(z):
        return z.reshape(B, T, RWKV_HEADS, RWKV_HEAD_SIZE)

    r = pm[..., 0:RWKV_WIDTH]
    k = pm[..., RWKV_WIDTH:2 * RWKV_WIDTH]
    v = pm[..., 2 * RWKV_WIDTH:3 * RWKV_WIDTH]
    o = 3 * RWKV_WIDTH
    w_lo = (pm[..., o:o + DECAY_LORA], pm[..., o + DECAY_LORA:o + 2 * DECAY_LORA])
    o += 2 * DECAY_LORA
    a_lo = (pm[..., o:o + AAA_LORA], pm[..., o + AAA_LORA:o + 2 * AAA_LORA])
    o += 2 * AAA_LORA
    g = jax.nn.sigmoid(pm[..., o:o + GATE_LORA]) @ lp['gate_w2']

    kk = heads(k * lp['k_k'])
    kk = kk / jnp.maximum(jnp.sqrt(jnp.sum(kk * kk, axis=-1, keepdims=True)), 1e-12)
    rh, vh = heads(r), heads(v)
    ys, finals = [], []
    for d, (reverse, s0) in enumerate(((False, s0_f), (True, s0_b))):
        w = -jax.nn.softplus(-(lp['decay_w0'][d] + jnp.tanh(w_lo[d]) @ lp['decay_w2'][d])) - 0.5
        decay = jnp.exp(-jnp.exp(w))
        a = jax.nn.sigmoid(lp['iclr_a0'][d] + a_lo[d] @ lp['iclr_a2'][d])
        kd = heads(k * (1.0 + (a - 1.0) * lp['k_a']))
        yd, Sd = wkv_scan(rh, heads(decay), kd, vh, kk, heads(a), s0, reverse)
        yd = yd + jnp.sum(rh * kd * lp['r_k'], axis=-1, keepdims=True) * vh
        ys.append(yd)
        finals.append(Sd)
    y = ys[0] + ys[1]
    mu = jnp.mean(y, axis=-1, keepdims=True)
    var = jnp.mean((y - mu) ** 2, axis=-1, keepdims=True)
    y = ((y - mu) * lax.rsqrt(var + GN_EPS)).reshape(B, T, RWKV_WIDTH)
    y = (y * lp['ln_x_w'] + lp['ln_x_b']) * g
    return y, finals[0], finals[1]


def moe(h, lp):
    B, T, D = h.shape
    xt = h.reshape(B * T, D)
    logits = (xt @ lp['router_w'] + lp['router_b']).astype(jnp.float32)
    top_v, top_i = lax.top_k(logits, TOP_K)
    top_w = jax.nn.softmax(top_v, axis=-1)
    combine = jnp.sum(jax.nn.one_hot(top_i, N_EXPERTS, dtype=jnp.float32) * top_w[..., None], axis=1)
    out = jnp.zeros((B * T, D), jnp.float32)
    for e in range(N_EXPERTS):
        gu = xt @ lp['expert_w_gu'][e] + lp['expert_b_gu'][e]
        glu = jnp.minimum(gu[:, :D_FF], SWIGLU_LIMIT)
        lin = jnp.clip(gu[:, D_FF:], -SWIGLU_LIMIT, SWIGLU_LIMIT)
        act = glu * jax.nn.sigmoid(SWIGLU_ALPHA * glu) * (lin + 1.0)
        out = out + combine[:, e:e + 1] * (act @ lp['expert_w_down'][e] + lp['expert_b_down'][e])
    return out.astype(h.dtype).reshape(B, T, D)


def trunk_layer(x, mod, shift_fn, s0_f, s0_b, lp):
    shift1, scale1, gate1, shift2, scale2, gate2 = mod
    hn = rmsnorm(x, lp['norm_mix_w']) * (1.0 + scale1) + shift1
    proj = hn @ lp['w_in']
    merge = jax.nn.sigmoid(proj[..., OFF_MERGE:] + lp['b_merge'])
    y_pool = pool_branch(proj[..., :OFF_R], lp['pool_w'], lp['pool_scale']) @ lp['w_pool_out']
    y_rwkv, sf, sb = rwkv_branch(proj[..., OFF_R:OFF_MERGE], shift_fn, s0_f, s0_b, lp)
    y_rwkv = y_rwkv @ lp['w_rwkv_out']
    mixed = (merge[..., :D_MODEL] * y_pool + merge[..., D_MODEL:] * y_rwkv) @ lp['w_o']
    x = (x + gate1 * mixed).astype(x.dtype)
    hn2 = rmsnorm(x, lp['norm_ffn_w']) * (1.0 + scale2) + shift2
    x = (x + gate2 * moe(hn2, lp)).astype(x.dtype)
    return x, sf, sb


def setup_inputs(seed: int = 0) -> dict:
    key = jax.random.key(seed)
    ks = iter(jax.random.split(key, 48))

    def nrm(shape, s):
        return jax.random.normal(next(ks), shape, jnp.float32) * s

    D, L, H, N, E = D_MODEL, DEPTH, RWKV_HEADS, RWKV_HEAD_SIZE, N_EXPERTS
    return {
        'x_prompt': nrm((BATCH, SEQ, D), 1.0),
        'x_sample': nrm((DEC_BATCH, DEC_SEQ, D), 1.0),
        'state_fwd': nrm((DEC_BATCH, L, H, N, N), 0.3),
        'state_bwd': nrm((DEC_BATCH, L, H, N, N), 0.3),
        'c': nrm((DEC_BATCH, D), 1.0),
        'c_ctx': nrm((D,), 1.0),
        'w_mod': nrm((L, D, N_MOD * D), 0.5 * D ** -0.5),
        'b_mod': nrm((L, N_MOD * D), 0.02),
        'norm_mix_w': 1.0 + nrm((L, D), 0.1),
        'w_in': nrm((L, D, IN_COLS), D ** -0.5),
        'b_merge': nrm((L, 2 * D), 0.1),
        'pool_w': nrm((L, POOL_GROUPS, POOL_GROUP_DIM, POOL_GROUP_DIM), POOL_GROUP_DIM ** -0.5),
        'pool_scale': 1.0 + nrm((L, POOL_WIDTH), 0.1),
        'w_pool_out': nrm((L, POOL_WIDTH, D), POOL_WIDTH ** -0.5),
        'shift_mu': jax.random.uniform(next(ks), (L, SHIFT_COLS), jnp.float32),
        'decay_w0': nrm((L, 2, RWKV_WIDTH), 0.5),
        'decay_w2': nrm((L, 2, DECAY_LORA, RWKV_WIDTH), 0.5 * DECAY_LORA ** -0.5),
        'iclr_a0': nrm((L, 2, RWKV_WIDTH), 0.5),
        'iclr_a2': nrm((L, 2, AAA_LORA, RWKV_WIDTH), 0.5 * AAA_LORA ** -0.5),
        'gate_w2': nrm((L, GATE_LORA, RWKV_WIDTH), GATE_LORA ** -0.5),
        'k_k': 0.85 + nrm((L, RWKV_WIDTH), 0.05),
        'k_a': 1.0 + nrm((L, RWKV_WIDTH), 0.05),
        'r_k': nrm((L, H, N), 0.1),
        'ln_x_w': 1.0 + nrm((L, RWKV_WIDTH), 0.1),
        'ln_x_b': nrm((L, RWKV_WIDTH), 0.02),
        'w_rwkv_out': nrm((L, RWKV_WIDTH, D), RWKV_WIDTH ** -0.5),
        'w_o': nrm((L, D, D), D ** -0.5),
        'norm_ffn_w': 1.0 + nrm((L, D), 0.1),
        'router_w': nrm((L, D, E), D ** -0.5),
        'router_b': nrm((L, E), 0.01),
        'expert_w_gu': nrm((L, E, D, 2 * D_FF), D ** -0.5),
        'expert_b_gu': nrm((L, E, 2 * D_FF), 0.01),
        'expert_w_down': nrm((L, E, D_FF, D), D_FF ** -0.5),
        'expert_b_down': nrm((L, E, D), 0.01),
        'norm_final_w': 1.0 + nrm((D,), 0.1),
    }


def reference(x_prompt, x_sample, state_fwd, state_bwd, c, c_ctx, w_mod, b_mod, norm_mix_w, w_in, b_merge,
              pool_w, pool_scale, w_pool_out, shift_mu, decay_w0, decay_w2, iclr_a0, iclr_a2, gate_w2,
              k_k, k_a, r_k, ln_x_w, ln_x_b, w_rwkv_out, w_o, norm_ffn_w, router_w, router_b,
              expert_w_gu, expert_b_gu, expert_w_down, expert_b_down, norm_final_w):
    xp, xs = x_prompt, x_sample
    zeros = jnp.zeros((xp.shape[0], RWKV_HEADS, RWKV_HEAD_SIZE, RWKV_HEAD_SIZE), jnp.float32)
    new_f, new_b = [], []
    for l in range(DEPTH):
        lp = {
            'norm_mix_w': norm_mix_w[l], 'w_in': w_in[l], 'b_merge': b_merge[l],
            'pool_w': pool_w[l], 'pool_scale': pool_scale[l], 'w_pool_out': w_pool_out[l],
            'shift_mu': shift_mu[l], 'decay_w0': decay_w0[l], 'decay_w2': decay_w2[l],
            'iclr_a0': iclr_a0[l], 'iclr_a2': iclr_a2[l], 'gate_w2': gate_w2[l],
            'k_k': k_k[l], 'k_a': k_a[l], 'r_k': r_k[l], 'ln_x_w': ln_x_w[l], 'ln_x_b': ln_x_b[l],
            'w_rwkv_out': w_rwkv_out[l], 'w_o': w_o[l], 'norm_ffn_w': norm_ffn_w[l],
            'router_w': router_w[l], 'router_b': router_b[l],
            'expert_w_gu': expert_w_gu[l], 'expert_b_gu': expert_b_gu[l],
            'expert_w_down': expert_w_down[l], 'expert_b_down': expert_b_down[l],
        }
        mod_ctx = modulation(c_ctx[None, :], w_mod[l], b_mod[l])
        mod_lat = modulation(c, w_mod[l], b_mod[l])
        xp, sf, sb = trunk_layer(xp, mod_ctx, shift_context, zeros, zeros, lp)
        new_f.append(sf)
        new_b.append(sb)
        xs, _, _ = trunk_layer(xs, mod_lat, shift_grid, state_fwd[:, l], state_bwd[:, l], lp)
    y_prompt = rmsnorm(xp, norm_final_w)
    y_sample = rmsnorm(xs, norm_final_w)
    new_state_fwd = jnp.stack(new_f, axis=1)
    new_state_bwd = jnp.stack(new_b, axis=1)
    return (y_prompt, y_sample, new_state_fwd, new_state_bwd)
```

```python
import functools
import math

import jax
import jax.numpy as jnp
from jax import lax
from jax.experimental import pallas as pl
from jax.experimental.pallas import tpu as pltpu

F32 = jnp.float32
BF16 = jnp.bfloat16

D = 1024
N_CTX_SEQ, T_CTX = 32, 256
N_LAT_SEQ, T_LAT = 4, 1024
GRID_W = 64
N_CTX_TOK = N_CTX_SEQ * T_CTX
N_LAT_TOK = N_LAT_SEQ * T_LAT
N_TOK = N_CTX_TOK + N_LAT_TOK

POOL_W = 512
POOL_G = 4
POOL_GD = POOL_W // POOL_G
POOL_WINDOWS = (2, 4, 8, 16)
RW = 512
HEAD = 64
N_HEAD = RW // HEAD
N_PAIR = N_HEAD // 2
LORA = 64
GATE_LORA = 128
N_EXP = 32
TOP_K = 4
D_FF = 1024
SWIGLU_ALPHA = 1.702
SWIGLU_LIMIT = 7.0
RMS_EPS = 1e-6
GN_EPS = 1e-5 * HEAD
N_MOD = 6

OFF_R = POOL_W
SHIFT_COLS = 3 * RW + 4 * LORA + GATE_LORA
OFF_MERGE = OFF_R + SHIFT_COLS
IN_COLS = OFF_MERGE + 2 * D

LANES = 128
CHUNK = 64
ROWS = 256
HALO = 64
EXT = ROWS + 2 * HALO
TM_PROJ = 512
TM_POST = 256
TM_MOE = 512
MOD_ROWS = 8
VMEM_LIMIT = 56 * 1024 * 1024


def _sigmoid(x):
    return 1.0 / (1.0 + jnp.exp(-x))


def _split2(a):
    hi = a.astype(BF16)
    lo = (a - hi.astype(F32)).astype(BF16)
    return hi, lo


_NN = (((1,), (0,)), ((), ()))
_NT = (((1,), (1,)), ((), ()))


def _dot(a, b, dims=_NN):
    return lax.dot_general(a, b, dims, preferred_element_type=F32)


def _dot3(a, b, dims=_NN):
    ah, al = _split2(a)
    bh, bl = _split2(b)
    return _dot(ah, bh, dims) + (_dot(ah, bl, dims) + _dot(al, bh, dims))


def _dot_exact_lhs(a_bf16, b, passes):
    acc = None
    rem = b
    for _ in range(passes):
        part = rem.astype(BF16)
        term = _dot(a_bf16, part)
        acc = term if acc is None else acc + term
        rem = rem - part.astype(F32)
    return acc


def _dot_exact_rhs(a, b_bf16, passes):
    acc = None
    rem = a
    for _ in range(passes):
        part = rem.astype(BF16)
        term = _dot(part, b_bf16)
        acc = term if acc is None else acc + term
        rem = rem - part.astype(F32)
    return acc


def _modnorm(x, w, scale, shift):
    ms = jnp.mean(x * x, axis=-1, keepdims=True)
    return x * lax.rsqrt(ms + RMS_EPS) * w * (1.0 + scale) + shift


def _mod_row(i, tm):
    n_ctx = N_CTX_TOK // tm
    per = T_LAT // tm
    return jnp.where(i < n_ctx, 0, 1 + (i - n_ctx) // per)


def _mod_kernel(c_ref, w_ref, b_ref, o_ref):
    c = c_ref[...]
    s = c * _sigmoid(c)
    o_ref[...] = _dot3(s, w_ref[...]) + b_ref[...]


def _modulation(cvec, w_mod, b_mod):
    return pl.pallas_call(
        _mod_kernel,
        out_shape=jax.ShapeDtypeStruct((MOD_ROWS, N_MOD * D), F32),
        grid=(N_MOD,),
        in_specs=[
            pl.BlockSpec((MOD_ROWS, D), lambda j: (0, 0)),
            pl.BlockSpec((D, D), lambda j: (0, j)),
            pl.BlockSpec((1, D), lambda j: (0, j)),
        ],
        out_specs=pl.BlockSpec((MOD_ROWS, D), lambda j: (0, j)),
        compiler_params=pltpu.CompilerParams(dimension_semantics=("arbitrary",), vmem_limit_bytes=VMEM_LIMIT),
        name="mod",
    )(cvec, w_mod, b_mod)


def _proj_kernel(x_ref, mod_ref, nw_ref, w_ref, o_ref):
    mod = mod_ref[0]
    hn = _modnorm(x_ref[...], nw_ref[...], mod[:, D:2 * D], mod[:, 0:D])
    o_ref[...] = _dot(hn.astype(BF16), w_ref[...])


def _projection(x_all, mod3, norm_w, w_a):
    n_cols = w_a.shape[1]
    return pl.pallas_call(
        _proj_kernel,
        out_shape=jax.ShapeDtypeStruct((N_TOK, n_cols), F32),
        grid=(N_TOK // TM_PROJ,),
        in_specs=[
            pl.BlockSpec((TM_PROJ, D), lambda i: (i, 0)),
            pl.BlockSpec((1, 1, N_MOD * D), lambda i: (_mod_row(i, TM_PROJ), 0, 0)),
            pl.BlockSpec((1, D), lambda i: (0, 0)),
            pl.BlockSpec((D, n_cols), lambda i: (0, 0)),
        ],
        out_specs=pl.BlockSpec((TM_PROJ, n_cols), lambda i: (i, 0)),
        compiler_params=pltpu.CompilerParams(dimension_semantics=("arbitrary",), vmem_limit_bytes=VMEM_LIMIT),
        name="proj",
    )(x_all, mod3, norm_w, w_a)


def _front_kernel(grid_mode, n_chunks, seq_len,
                  cur_ref, prev_ref, next_ref, poolw_ref, pscale_ref, mu_ref,
                  dw0_ref, dw2_ref, a0_ref, a2_ref, kk_ref, ka_ref, rk_ref, seg_ref,
                  ypre_ref, gsig_ref, r_ref, v_ref, kkn_ref, bonus_ref,
                  lwf_ref, lwb_ref, kdf_ref, kdb_ref, bf_ref, bb_ref):
    c = pl.program_id(1)
    cur = cur_ref[...]
    prev = jnp.where(c > 0, prev_ref[...], 0.0)
    nxt = jnp.where(c < n_chunks - 1, next_ref[...], 0.0)
    ext = jnp.concatenate([prev, cur, nxt], axis=0)

    def down(x, s):
        return pltpu.roll(x, s, 0)

    def up(x, s):
        return pltpu.roll(x, EXT - s, 0)

    def mid(x):
        return x[HALO:HALO + ROWS]

    row = lax.broadcasted_iota(jnp.int32, (ROWS, 1), 0)
    t_seq = c * ROWS + row

    outs = []
    for gi, win in enumerate(POOL_WINDOWS):
        h = win // 2
        u = ext[:, gi * POOL_GD:(gi + 1) * POOL_GD]
        trail, lead, s = u, u, 1
        while s < h:
            trail = trail + down(trail, s)
            lead = lead + up(lead, s)
            s *= 2
        total = mid(down(trail, 1) + lead)
        cnt = (jnp.minimum(t_seq + h, seq_len) - jnp.maximum(t_seq - h, 0)).astype(F32)
        pooled = total / cnt - mid(u)
        outs.append(_dot(pooled.astype(BF16), poolw_ref[gi]))
    mixed = jnp.concatenate(outs, axis=1) * pscale_ref[...]
    ypre_ref[...] = mixed.astype(BF16)

    pe = ext[:, OFF_R:OFF_MERGE]
    p = mid(pe)
    lane = lax.broadcasted_iota(jnp.int32, (1, SHIFT_COLS), 1)
    if grid_mode:
        col = t_seq % GRID_W
        left = jnp.where(col > 0, mid(down(pe, 1)), 0.0)
        right = jnp.where(col < GRID_W - 1, mid(up(pe, 1)), 0.0)
        upn = pe[0:ROWS]
        dnn = pe[2 * HALO:2 * HALO + ROWS]
        q = lane % 4
        sh = jnp.where(q == 0, left, jnp.where(q == 1, right, jnp.where(q == 2, upn, dnn)))
    else:
        sh = jnp.where(lane % 2 == 0, mid(down(pe, 1)), mid(up(pe, 1)))
    pm = p + mu_ref[...] * (sh - p)

    r = pm[:, 0:RW]
    k = pm[:, RW:2 * RW]
    v = pm[:, 2 * RW:3 * RW]
    o = 3 * RW
    w_lo = jnp.tanh(pm[:, o:o + 2 * LORA])
    a_lo = pm[:, o + 2 * LORA:o + 4 * LORA]
    gsig_ref[...] = _sigmoid(pm[:, o + 4 * LORA:o + 4 * LORA + GATE_LORA]).astype(BF16)

    seg = seg_ref[...]
    kkr = k * kk_ref[...]
    ss = _dot_exact_rhs(kkr * kkr, seg, 2)
    kkn = kkr / jnp.maximum(jnp.sqrt(ss), 1e-12)
    r_ref[...] = r
    v_ref[...] = v
    kkn_ref[...] = kkn

    bonus = None
    for d, (lw_ref, kd_ref, b_ref) in enumerate(((lwf_ref, kdf_ref, bf_ref), (lwb_ref, kdb_ref, bb_ref))):
        z = dw0_ref[d:d + 1, :] + _dot3(w_lo, dw2_ref[d])
        lw_ref[...] = -math.exp(-0.5) * _sigmoid(z)
        a = _sigmoid(a0_ref[d:d + 1, :] + _dot3(a_lo, a2_ref[d]))
        kd = k * (1.0 + (a - 1.0) * ka_ref[...])
        kd_ref[...] = kd
        b_ref[...] = kkn * a
        bd = _dot_exact_rhs(r * kd * rk_ref[...], seg, 2) * v
        bonus = bd if bonus is None else bonus + bd
    bonus_ref[...] = bonus


def _front(proj, grid_mode, n_seq, seq_len, tok0, weights):
    n_chunks = seq_len // ROWS
    n_tok = n_seq * seq_len
    blk0 = tok0 // ROWS
    hpb = ROWS // HALO
    last_halo = N_TOK // HALO - 1

    def cur_map(b, c):
        return (blk0 + b * n_chunks + c, 0)

    def prev_map(b, c):
        return (jnp.maximum((blk0 + b * n_chunks + c) * hpb - 1, 0), 0)

    def next_map(b, c):
        return (jnp.minimum((blk0 + b * n_chunks + c + 1) * hpb, last_halo), 0)

    def out_map(b, c):
        return (b * n_chunks + c, 0)

    def full(a):
        nd = a.ndim
        return pl.BlockSpec(a.shape, lambda b, c, _nd=nd: (0,) * _nd)

    f32_out = jax.ShapeDtypeStruct((n_tok, RW), F32)
    out_shape = [jax.ShapeDtypeStruct((n_tok, POOL_W), BF16), jax.ShapeDtypeStruct((n_tok, GATE_LORA), BF16)]
    out_shape += [f32_out] * 10
    out_specs = [pl.BlockSpec((ROWS, POOL_W), out_map), pl.BlockSpec((ROWS, GATE_LORA), out_map)]
    out_specs += [pl.BlockSpec((ROWS, RW), out_map)] * 10
    return pl.pallas_call(
        functools.partial(_front_kernel, grid_mode, n_chunks, seq_len),
        out_shape=out_shape,
        grid=(n_seq, n_chunks),
        in_specs=[
            pl.BlockSpec((ROWS, OFF_MERGE), cur_map),
            pl.BlockSpec((HALO, OFF_MERGE), prev_map),
            pl.BlockSpec((HALO, OFF_MERGE), next_map),
        ] + [full(a) for a in weights],
        out_specs=out_specs,
        compiler_params=pltpu.CompilerParams(dimension_semantics=("arbitrary", "arbitrary"),
                                             vmem_limit_bytes=VMEM_LIMIT),
        name="front_grid" if grid_mode else "front_ctx",
    )(proj, proj, proj, *weights)


def _scan_kernel(n_chunks, has_init, *refs):
    (r_ref, v_ref, kk_ref, bonus_ref, lwf_ref, lwb_ref, kdf_ref, kdb_ref, bf_ref, bb_ref) = refs[:10]
    if has_init:
        s0f_ref, s0b_ref = refs[10:12]
        y_ref, sf_ref, sb_ref = refs[12:]
    else:
        y_ref, sf_ref, sb_ref = refs[10:]
    C = CHUNK
    P2 = 2 * C
    ri = lax.broadcasted_iota(jnp.int32, (P2, P2), 0)
    ci = lax.broadcasted_iota(jnp.int32, (P2, P2), 1)
    same = (ri // C) == (ci // C)
    rp, cp = ri % C, ci % C
    eye = (ri == ci).astype(F32)
    ti = lax.broadcasted_iota(jnp.int32, (C, C), 0)
    tj = lax.broadcasted_iota(jnp.int32, (C, C), 1)
    head_a = lax.broadcasted_iota(jnp.int32, (C, LANES), 1) < HEAD

    def stack(x):
        return jnp.concatenate([jnp.where(head_a, x, 0.0), jnp.where(head_a, 0.0, x)], axis=0)

    def chunk(S, row0, lw_ref, kd_ref, b_ref, reverse):
        if reverse:
            strict, incl, tri, last = same & (cp > rp), same & (cp >= rp), (tj >= ti), 0
        else:
            strict, incl, tri, last = same & (cp < rp), same & (cp <= rp), (tj <= ti), C - 1
        sl = pl.ds(row0, C)
        lw = lw_ref[sl, :]
        cum_i = _dot_exact_lhs(tri.astype(BF16), lw, 3)
        e_i = jnp.exp(cum_i)
        e_n = jnp.exp(-cum_i)
        kkc = kk_ref[sl, :]
        QR = jnp.concatenate([stack(kkc * jnp.exp(cum_i - lw)), stack(r_ref[sl, :] * e_i)], axis=0)
        BK = jnp.concatenate([stack(b_ref[sl, :] * e_n), stack(kd_ref[sl, :] * e_n)], axis=0)
        Vs = stack(v_ref[sl, :])
        G = _dot3(QR, BK, _NT)
        L = jnp.where(strict, G[0:P2, 0:P2], 0.0)
        Lk = jnp.where(strict, G[0:P2, P2:2 * P2], 0.0)
        RB = jnp.where(incl, G[P2:2 * P2, 0:P2], 0.0)
        RK = jnp.where(incl, G[P2:2 * P2, P2:2 * P2], 0.0)
        X = eye - L
        Pw = -L
        for _ in range(int(math.log2(C)) - 1):
            Pw = _dot3(Pw, Pw)
            X = X + _dot3(X, Pw)
        QRA = _dot3(QR, S, _NT)
        W = QRA[0:P2] + _dot3(Lk, Vs)
        U = -_dot3(X, W)
        UV = jnp.concatenate([U, Vs], axis=0)
        Ys = QRA[P2:2 * P2] + _dot3(jnp.concatenate([RB, RK], axis=1), UV)
        dS = _dot3(UV.T, BK)
        S_new = (S + dS) * e_i[last:last + 1, :]
        return S_new, Ys[0:C] + Ys[C:P2]

    y_ref[...] = bonus_ref[...]
    if has_init:
        S0f, S0b = s0f_ref[0, 0], s0b_ref[0, 0]
    else:
        S0f = S0b = jnp.zeros((P2, P2), F32)

    def body(c, carry):
        Sf, Sb = carry
        rf = pl.multiple_of(c * C, C)
        rb = pl.multiple_of((n_chunks - 1 - c) * C, C)
        Sf, yf = chunk(Sf, rf, lwf_ref, kdf_ref, bf_ref, False)
        y_ref[pl.ds(rf, C), :] += yf
        Sb, yb = chunk(Sb, rb, lwb_ref, kdb_ref, bb_ref, True)
        y_ref[pl.ds(rb, C), :] += yb
        return Sf, Sb

    Sf, Sb = lax.fori_loop(0, n_chunks, body, (S0f, S0b))
    sf_ref[0, 0] = Sf
    sb_ref[0, 0] = Sb


def _scan(arrs, n_seq, seq_len, init):
    n_tok = n_seq * seq_len
    n_chunks = seq_len // CHUNK
    tok_spec = pl.BlockSpec((seq_len, LANES), lambda b, p: (b, p))
    st_spec = pl.BlockSpec((1, 1, LANES, LANES), lambda b, p: (b, p, 0, 0))
    st_shape = jax.ShapeDtypeStruct((n_seq, N_PAIR, LANES, LANES), F32)
    ins = list(arrs)
    in_specs = [tok_spec] * 10
    if init is not None:
        ins += list(init)
        in_specs += [st_spec, st_spec]
    return pl.pallas_call(
        functools.partial(_scan_kernel, n_chunks, init is not None),
        out_shape=[jax.ShapeDtypeStruct((n_tok, RW), F32), st_shape, st_shape],
        grid=(n_seq, N_PAIR),
        in_specs=in_specs,
        out_specs=[tok_spec, st_spec, st_spec],
        compiler_params=pltpu.CompilerParams(dimension_semantics=("arbitrary", "arbitrary"),
                                             vmem_limit_bytes=VMEM_LIMIT),
        name="scan_init" if init is not None else "scan_zero",
    )(*ins)


def _post_kernel(x_ref, mod_ref, ypre_ref, gsig_ref, y_ref,
                 nmw_ref, wm_ref, bm_ref, wpo_ref, seg_ref, lnw_ref, lnb_ref, gw2_ref, wro_ref, wo_ref,
                 nfw_ref, rw_ref, rb_ref,
                 x1_ref, hn2_ref, comb_ref):
    x = x_ref[...]
    mod = mod_ref[0]
    shift1, scale1, gate1 = mod[:, 0:D], mod[:, D:2 * D], mod[:, 2 * D:3 * D]
    shift2, scale2 = mod[:, 3 * D:4 * D], mod[:, 4 * D:5 * D]
    hn = _modnorm(x, nmw_ref[...], scale1, shift1)
    merge = _sigmoid(_dot(hn.astype(BF16), wm_ref[...]) + bm_ref[...])
    y_pool = _dot(ypre_ref[...], wpo_ref[...])

    y = y_ref[...]
    seg = seg_ref[...]
    mu = _dot_exact_rhs(y, seg, 2) * (1.0 / HEAD)
    yc = y - mu
    var = _dot_exact_rhs(yc * yc, seg, 2) * (1.0 / HEAD)
    yn = yc * lax.rsqrt(var + GN_EPS) * lnw_ref[...] + lnb_ref[...]
    g = _dot(gsig_ref[...], gw2_ref[...])
    y_rwkv = _dot((yn * g).astype(BF16), wro_ref[...])

    mixed = merge[:, 0:D] * y_pool + merge[:, D:2 * D] * y_rwkv
    x1 = x + gate1 * _dot(mixed.astype(BF16), wo_ref[...])
    x1_ref[...] = x1
    hn2 = _modnorm(x1, nfw_ref[...], scale2, shift2)
    hn2_ref[...] = hn2.astype(BF16)

    logits = _dot3(hn2, rw_ref[...]) + rb_ref[...]
    lane = lax.broadcasted_iota(jnp.int32, logits.shape, 1)
    work = logits
    sel = None
    top = None
    for j in range(TOP_K):
        m = jnp.max(work, axis=-1, keepdims=True)
        if j == 0:
            top = m
        idx = jnp.min(jnp.where(work == m, lane, LANES), axis=-1, keepdims=True)
        pick = lane == idx
        sel = pick if sel is None else (sel | pick)
        work = jnp.where(pick, -jnp.inf, work)
    e = jnp.where(sel, jnp.exp(logits - top), 0.0)
    comb_ref[...] = e / jnp.sum(e, axis=-1, keepdims=True)


def _post(x_all, mod3, ypre, gsig, y, weights):
    def tile(n, dt):
        return pl.BlockSpec((TM_POST, n), lambda i: (i, 0))

    def full(a):
        nd = a.ndim
        return pl.BlockSpec(a.shape, lambda i, _nd=nd: (0,) * _nd)

    return pl.pallas_call(
        _post_kernel,
        out_shape=[jax.ShapeDtypeStruct((N_TOK, D), F32), jax.ShapeDtypeStruct((N_TOK, D), BF16),
                   jax.ShapeDtypeStruct((N_TOK, LANES), F32)],
        grid=(N_TOK // TM_POST,),
        in_specs=[
            tile(D, F32),
            pl.BlockSpec((1, 1, N_MOD * D), lambda i: (_mod_row(i, TM_POST), 0, 0)),
            tile(POOL_W, BF16), tile(GATE_LORA, BF16), tile(RW, F32),
        ] + [full(a) for a in weights],
        out_specs=[tile(D, F32), tile(D, BF16), tile(LANES, F32)],
        compiler_params=pltpu.CompilerParams(dimension_semantics=("arbitrary",), vmem_limit_bytes=VMEM_LIMIT),
        name="post",
    )(x_all, mod3, ypre, gsig, y, *weights)


def _moe_kernel(x1_ref, hn2_ref, comb_ref, mod_ref, wgu_ref, bgu_ref, wd_ref, bd_ref, nfw_ref, o_ref, acc_ref):
    e = pl.program_id(1)

    @pl.when(e == 0)
    def _():
        acc_ref[...] = jnp.zeros_like(acc_ref)

    gu = _dot(hn2_ref[...], wgu_ref[0]) + bgu_ref[0]
    glu = jnp.minimum(gu[:, 0:D_FF], SWIGLU_LIMIT)
    lin = jnp.clip(gu[:, D_FF:2 * D_FF], -SWIGLU_LIMIT, SWIGLU_LIMIT)
    act = glu * _sigmoid(SWIGLU_ALPHA * glu) * (lin + 1.0)
    out = _dot(act.astype(BF16), wd_ref[0]) + bd_ref[0]
    comb = comb_ref[...]
    lane = lax.broadcasted_iota(jnp.int32, comb.shape, 1)
    ce = jnp.sum(jnp.where(lane == e, comb, 0.0), axis=-1, keepdims=True)
    acc_ref[...] += ce * out

    @pl.when(e == N_EXP - 1)
    def _():
        gate2 = mod_ref[0][:, 5 * D:6 * D]
        x2 = x1_ref[...] + gate2 * acc_ref[...]
        ms = jnp.mean(x2 * x2, axis=-1, keepdims=True)
        o_ref[...] = x2 * lax.rsqrt(ms + RMS_EPS) * nfw_ref[...]


def _moe(x1, hn2, comb, mod3, wgu, bgu, wd, bd, nfw):
    return pl.pallas_call(
        _moe_kernel,
        out_shape=jax.ShapeDtypeStruct((N_TOK, D), F32),
        grid=(N_TOK // TM_MOE, N_EXP),
        in_specs=[
            pl.BlockSpec((TM_MOE, D), lambda i, e: (i, 0)),
            pl.BlockSpec((TM_MOE, D), lambda i, e: (i, 0)),
            pl.BlockSpec((TM_MOE, LANES), lambda i, e: (i, 0)),
            pl.BlockSpec((1, 1, N_MOD * D), lambda i, e: (_mod_row(i, TM_MOE), 0, 0)),
            pl.BlockSpec((1, D, 2 * D_FF), lambda i, e: (e, 0, 0)),
            pl.BlockSpec((1, 1, 2 * D_FF), lambda i, e: (e, 0, 0)),
            pl.BlockSpec((1, D_FF, D), lambda i, e: (e, 0, 0)),
            pl.BlockSpec((1, 1, D), lambda i, e: (e, 0, 0)),
            pl.BlockSpec((1, D), lambda i, e: (0, 0)),
        ],
        out_specs=pl.BlockSpec((TM_MOE, D), lambda i, e: (i, 0)),
        scratch_shapes=[pltpu.VMEM((TM_MOE, D), F32)],
        compiler_params=pltpu.CompilerParams(dimension_semantics=("arbitrary", "arbitrary"),
                                             vmem_limit_bytes=VMEM_LIMIT),
        name="moe",
    )(x1, hn2, comb, mod3, wgu, bgu, wd, bd, nfw)


def _pair_blockdiag(s):
    b = s.shape[0]
    s = s.reshape(b, N_PAIR, 2, HEAD, HEAD)
    z = jnp.zeros((b, N_PAIR, HEAD, HEAD), s.dtype)
    top = jnp.concatenate([s[:, :, 0], z], axis=-1)
    bot = jnp.concatenate([z, s[:, :, 1]], axis=-1)
    return jnp.concatenate([top, bot], axis=-2)


def _pair_unblock(s):
    b = s.shape[0]
    a = s[:, :, 0:HEAD, 0:HEAD]
    c = s[:, :, HEAD:, HEAD:]
    return jnp.stack([a, c], axis=2).reshape(b, 1, N_HEAD, HEAD, HEAD)


def kernel(x_prompt, x_sample, state_fwd, state_bwd, c, c_ctx, w_mod, b_mod, norm_mix_w, w_in, b_merge, pool_w, pool_scale, w_pool_out, shift_mu, decay_w0, decay_w2, iclr_a0, iclr_a2, gate_w2, k_k, k_a, r_k, ln_x_w, ln_x_b, w_rwkv_out, w_o, norm_ffn_w, router_w, router_b, expert_w_gu, expert_b_gu, expert_w_down, expert_b_down, norm_final_w):
    l = 0
    x_all = jnp.concatenate([x_prompt.reshape(N_CTX_TOK, D), x_sample.reshape(N_LAT_TOK, D)], axis=0)
    cvec = jnp.concatenate([c_ctx[None, :], c, jnp.zeros((MOD_ROWS - 1 - N_LAT_SEQ, D), F32)], axis=0)
    mod = _modulation(cvec, w_mod[l], b_mod[l][None, :])
    mod3 = mod.reshape(MOD_ROWS, 1, N_MOD * D)

    w_in_l = w_in[l]
    proj = _projection(x_all, mod3, norm_mix_w[l][None, :], w_in_l[:, :OFF_MERGE].astype(BF16))

    zl = jnp.zeros((LORA, RW), F32)
    dw2 = jnp.stack([jnp.concatenate([decay_w2[l, 0], zl], 0), jnp.concatenate([zl, decay_w2[l, 1]], 0)])
    a2 = jnp.stack([jnp.concatenate([iclr_a2[l, 0], zl], 0), jnp.concatenate([zl, iclr_a2[l, 1]], 0)])
    hid = jnp.arange(RW) // HEAD
    seg = (hid[:, None] == hid[None, :]).astype(BF16)
    front_w = (pool_w[l].astype(BF16), pool_scale[l][None, :], shift_mu[l][None, :],
               decay_w0[l], dw2, iclr_a0[l], a2, k_k[l][None, :], k_a[l][None, :],
               r_k[l].reshape(1, RW), seg)
    f_ctx = _front(proj, False, N_CTX_SEQ, T_CTX, 0, front_w)
    f_lat = _front(proj, True, N_LAT_SEQ, T_LAT, N_CTX_TOK, front_w)

    y_ctx, sf, sb = _scan(f_ctx[2:], N_CTX_SEQ, T_CTX, None)
    init = (_pair_blockdiag(state_fwd[:, l]), _pair_blockdiag(state_bwd[:, l]))
    y_lat, _, _ = _scan(f_lat[2:], N_LAT_SEQ, T_LAT, init)

    ypre = jnp.concatenate([f_ctx[0], f_lat[0]], axis=0)
    gsig = jnp.concatenate([f_ctx[1], f_lat[1]], axis=0)
    y = jnp.concatenate([y_ctx, y_lat], axis=0)

    rw_pad = jnp.concatenate([router_w[l], jnp.zeros((D, LANES - N_EXP), F32)], axis=1)
    rb_pad = jnp.concatenate([router_b[l], jnp.full((LANES - N_EXP,), -1e30, F32)])[None, :]
    post_w = (norm_mix_w[l][None, :], w_in_l[:, OFF_MERGE:].astype(BF16), b_merge[l][None, :],
              w_pool_out[l].astype(BF16), seg, ln_x_w[l][None, :], ln_x_b[l][None, :],
              gate_w2[l].astype(BF16), w_rwkv_out[l].astype(BF16), w_o[l].astype(BF16),
              norm_ffn_w[l][None, :], rw_pad, rb_pad)
    x1, hn2, comb = _post(x_all, mod3, ypre, gsig, y, post_w)

    out = _moe(x1, hn2, comb, mod3,
               expert_w_gu[l].astype(BF16), expert_b_gu[l][:, None, :],
               expert_w_down[l].astype(BF16), expert_b_down[l][:, None, :], norm_final_w[None, :])
    y_prompt = out[:N_CTX_TOK].reshape(N_CTX_SEQ, T_CTX, D)
    y_sample = out[N_CTX_TOK:].reshape(N_LAT_SEQ, T_LAT, D)
    return (y_prompt, y_sample, _pair_unblock(sf), _pair_unblock(sb))
```

```python
import functools
import math

import jax
import jax.numpy as jnp
from jax import lax
from jax.experimental import pallas as pl
from jax.experimental.pallas import tpu as pltpu

F32 = jnp.float32
BF16 = jnp.bfloat16

D = 1024
N_CTX_SEQ, T_CTX = 32, 256
N_LAT_SEQ, T_LAT = 4, 1024
GRID_W = 64
N_CTX_TOK = N_CTX_SEQ * T_CTX
N_LAT_TOK = N_LAT_SEQ * T_LAT
N_TOK = N_CTX_TOK + N_LAT_TOK

POOL_W = 512
POOL_G = 4
POOL_GD = POOL_W // POOL_G
POOL_WINDOWS = (2, 4, 8, 16)
RW = 512
HEAD = 64
N_HEAD = RW // HEAD
N_PAIR = N_HEAD // 2
LORA = 64
GATE_LORA = 128
N_EXP = 32
TOP_K = 4
D_FF = 1024
SWIGLU_ALPHA = 1.702
SWIGLU_LIMIT = 7.0
RMS_EPS = 1e-6
GN_EPS = 1e-5 * HEAD
N_MOD = 6

OFF_R = POOL_W
SHIFT_COLS = 3 * RW + 4 * LORA + GATE_LORA
OFF_MERGE = OFF_R + SHIFT_COLS
IN_COLS = OFF_MERGE + 2 * D

LANES = 128
CHUNK = 64
ROWS = 256
HALO = 64
EXT = ROWS + 2 * HALO
TM_PROJ = 512
TM_POST = 256
TM_DISP = 256
TM_SLOT = 256
N_SLOT_TILES = N_TOK * TOP_K // TM_SLOT + N_EXP
N_SLOTS = N_SLOT_TILES * TM_SLOT
RANK_BITS = 14
META_NT = N_EXP
MOD_ROWS = 8
VMEM_LIMIT = 56 * 1024 * 1024


def _sigmoid(x):
    return 1.0 / (1.0 + jnp.exp(-x))


def _split2(a):
    hi = a.astype(BF16)
    lo = (a - hi.astype(F32)).astype(BF16)
    return hi, lo


_NN = (((1,), (0,)), ((), ()))
_NT = (((1,), (1,)), ((), ()))


def _dot(a, b, dims=_NN):
    return lax.dot_general(a, b, dims, preferred_element_type=F32)


def _dot3(a, b, dims=_NN):
    ah, al = _split2(a)
    bh, bl = _split2(b)
    return _dot(ah, bh, dims) + (_dot(ah, bl, dims) + _dot(al, bh, dims))


def _dot_exact_lhs(a_bf16, b, passes):
    acc = None
    rem = b
    for _ in range(passes):
        part = rem.astype(BF16)
        term = _dot(a_bf16, part)
        acc = term if acc is None else acc + term
        rem = rem - part.astype(F32)
    return acc


def _dot_exact_rhs(a, b_bf16, passes):
    acc = None
    rem = a
    for _ in range(passes):
        part = rem.astype(BF16)
        term = _dot(part, b_bf16)
        acc = term if acc is None else acc + term
        rem = rem - part.astype(F32)
    return acc


def _modnorm(x, w, scale, shift):
    ms = jnp.mean(x * x, axis=-1, keepdims=True)
    return x * lax.rsqrt(ms + RMS_EPS) * w * (1.0 + scale) + shift


def _mod_row(i, tm):
    n_ctx = N_CTX_TOK // tm
    per = T_LAT // tm
    return jnp.where(i < n_ctx, 0, 1 + (i - n_ctx) // per)


def _mod_kernel(c_ref, w_ref, b_ref, o_ref):
    c = c_ref[...]
    s = c * _sigmoid(c)
    o_ref[...] = _dot3(s, w_ref[...]) + b_ref[...]


def _modulation(cvec, w_mod, b_mod):
    return pl.pallas_call(
        _mod_kernel,
        out_shape=jax.ShapeDtypeStruct((MOD_ROWS, N_MOD * D), F32),
        grid=(N_MOD,),
        in_specs=[
            pl.BlockSpec((MOD_ROWS, D), lambda j: (0, 0)),
            pl.BlockSpec((D, D), lambda j: (0, j)),
            pl.BlockSpec((1, D), lambda j: (0, j)),
        ],
        out_specs=pl.BlockSpec((MOD_ROWS, D), lambda j: (0, j)),
        compiler_params=pltpu.CompilerParams(dimension_semantics=("arbitrary",), vmem_limit_bytes=VMEM_LIMIT),
        name="mod",
    )(cvec, w_mod, b_mod)


def _proj_kernel(x_ref, mod_ref, nw_ref, w_ref, o_ref):
    mod = mod_ref[0]
    hn = _modnorm(x_ref[...], nw_ref[...], mod[:, D:2 * D], mod[:, 0:D])
    o_ref[...] = _dot(hn.astype(BF16), w_ref[...])


def _projection(x_all, mod3, norm_w, w_a):
    n_cols = w_a.shape[1]
    return pl.pallas_call(
        _proj_kernel,
        out_shape=jax.ShapeDtypeStruct((N_TOK, n_cols), F32),
        grid=(N_TOK // TM_PROJ,),
        in_specs=[
            pl.BlockSpec((TM_PROJ, D), lambda i: (i, 0)),
            pl.BlockSpec((1, 1, N_MOD * D), lambda i: (_mod_row(i, TM_PROJ), 0, 0)),
            pl.BlockSpec((1, D), lambda i: (0, 0)),
            pl.BlockSpec((D, n_cols), lambda i: (0, 0)),
        ],
        out_specs=pl.BlockSpec((TM_PROJ, n_cols), lambda i: (i, 0)),
        compiler_params=pltpu.CompilerParams(dimension_semantics=("arbitrary",), vmem_limit_bytes=VMEM_LIMIT),
        name="proj",
    )(x_all, mod3, norm_w, w_a)


def _front_kernel(grid_mode, n_chunks, seq_len,
                  cur_ref, prev_ref, next_ref, poolw_ref, pscale_ref, mu_ref,
                  dw0_ref, dw2_ref, a0_ref, a2_ref, kk_ref, ka_ref, rk_ref, seg_ref,
                  ypre_ref, gsig_ref, r_ref, v_ref, kkn_ref, bonus_ref,
                  lwf_ref, lwb_ref, kdf_ref, kdb_ref, bf_ref, bb_ref):
    c = pl.program_id(1)
    cur = cur_ref[...]
    prev = jnp.where(c > 0, prev_ref[...], 0.0)
    nxt = jnp.where(c < n_chunks - 1, next_ref[...], 0.0)
    ext = jnp.concatenate([prev, cur, nxt], axis=0)

    def down(x, s):
        return pltpu.roll(x, s, 0)

    def up(x, s):
        return pltpu.roll(x, EXT - s, 0)

    def mid(x):
        return x[HALO:HALO + ROWS]

    row = lax.broadcasted_iota(jnp.int32, (ROWS, 1), 0)
    t_seq = c * ROWS + row

    outs = []
    for gi, win in enumerate(POOL_WINDOWS):
        h = win // 2
        u = ext[:, gi * POOL_GD:(gi + 1) * POOL_GD]
        trail, lead, s = u, u, 1
        while s < h:
            trail = trail + down(trail, s)
            lead = lead + up(lead, s)
            s *= 2
        total = mid(down(trail, 1) + lead)
        cnt = (jnp.minimum(t_seq + h, seq_len) - jnp.maximum(t_seq - h, 0)).astype(F32)
        pooled = total / cnt - mid(u)
        outs.append(_dot(pooled.astype(BF16), poolw_ref[gi]))
    mixed = jnp.concatenate(outs, axis=1) * pscale_ref[...]
    ypre_ref[...] = mixed.astype(BF16)

    pe = ext[:, OFF_R:OFF_MERGE]
    p = mid(pe)
    lane = lax.broadcasted_iota(jnp.int32, (1, SHIFT_COLS), 1)
    if grid_mode:
        col = t_seq % GRID_W
        left = jnp.where(col > 0, mid(down(pe, 1)), 0.0)
        right = jnp.where(col < GRID_W - 1, mid(up(pe, 1)), 0.0)
        upn = pe[0:ROWS]
        dnn = pe[2 * HALO:2 * HALO + ROWS]
        q = lane % 4
        sh = jnp.where(q == 0, left, jnp.where(q == 1, right, jnp.where(q == 2, upn, dnn)))
    else:
        sh = jnp.where(lane % 2 == 0, mid(down(pe, 1)), mid(up(pe, 1)))
    pm = p + mu_ref[...] * (sh - p)

    r = pm[:, 0:RW]
    k = pm[:, RW:2 * RW]
    v = pm[:, 2 * RW:3 * RW]
    o = 3 * RW
    w_lo = jnp.tanh(pm[:, o:o + 2 * LORA])
    a_lo = pm[:, o + 2 * LORA:o + 4 * LORA]
    gsig_ref[...] = _sigmoid(pm[:, o + 4 * LORA:o + 4 * LORA + GATE_LORA]).astype(BF16)

    seg = seg_ref[...]
    kkr = k * kk_ref[...]
    ss = _dot_exact_rhs(kkr * kkr, seg, 2)
    kkn = kkr / jnp.maximum(jnp.sqrt(ss), 1e-12)
    r_ref[...] = r
    v_ref[...] = v
    kkn_ref[...] = kkn

    bonus = None
    for d, (lw_ref, kd_ref, b_ref) in enumerate(((lwf_ref, kdf_ref, bf_ref), (lwb_ref, kdb_ref, bb_ref))):
        z = dw0_ref[d:d + 1, :] + _dot3(w_lo, dw2_ref[d])
        lw_ref[...] = -math.exp(-0.5) * _sigmoid(z)
        a = _sigmoid(a0_ref[d:d + 1, :] + _dot3(a_lo, a2_ref[d]))
        kd = k * (1.0 + (a - 1.0) * ka_ref[...])
        kd_ref[...] = kd
        b_ref[...] = kkn * a
        bd = _dot_exact_rhs(r * kd * rk_ref[...], seg, 2) * v
        bonus = bd if bonus is None else bonus + bd
    bonus_ref[...] = bonus


def _front(proj, grid_mode, n_seq, seq_len, tok0, weights):
    n_chunks = seq_len // ROWS
    n_tok = n_seq * seq_len
    blk0 = tok0 // ROWS
    hpb = ROWS // HALO
    last_halo = N_TOK // HALO - 1

    def cur_map(b, c):
        return (blk0 + b * n_chunks + c, 0)

    def prev_map(b, c):
        return (jnp.maximum((blk0 + b * n_chunks + c) * hpb - 1, 0), 0)

    def next_map(b, c):
        return (jnp.minimum((blk0 + b * n_chunks + c + 1) * hpb, last_halo), 0)

    def out_map(b, c):
        return (b * n_chunks + c, 0)

    def full(a):
        nd = a.ndim
        return pl.BlockSpec(a.shape, lambda b, c, _nd=nd: (0,) * _nd)

    f32_out = jax.ShapeDtypeStruct((n_tok, RW), F32)
    out_shape = [jax.ShapeDtypeStruct((n_tok, POOL_W), BF16), jax.ShapeDtypeStruct((n_tok, GATE_LORA), BF16)]
    out_shape += [f32_out] * 10
    out_specs = [pl.BlockSpec((ROWS, POOL_W), out_map), pl.BlockSpec((ROWS, GATE_LORA), out_map)]
    out_specs += [pl.BlockSpec((ROWS, RW), out_map)] * 10
    return pl.pallas_call(
        functools.partial(_front_kernel, grid_mode, n_chunks, seq_len),
        out_shape=out_shape,
        grid=(n_seq, n_chunks),
        in_specs=[
            pl.BlockSpec((ROWS, OFF_MERGE), cur_map),
            pl.BlockSpec((HALO, OFF_MERGE), prev_map),
            pl.BlockSpec((HALO, OFF_MERGE), next_map),
        ] + [full(a) for a in weights],
        out_specs=out_specs,
        compiler_params=pltpu.CompilerParams(dimension_semantics=("arbitrary", "arbitrary"),
                                             vmem_limit_bytes=VMEM_LIMIT),
        name="front_grid" if grid_mode else "front_ctx",
    )(proj, proj, proj, *weights)


def _scan_kernel(n_chunks, has_init, *refs):
    (r_ref, v_ref, kk_ref, bonus_ref, lwf_ref, lwb_ref, kdf_ref, kdb_ref, bf_ref, bb_ref) = refs[:10]
    if has_init:
        s0f_ref, s0b_ref = refs[10:12]
        y_ref, sf_ref, sb_ref = refs[12:]
    else:
        y_ref, sf_ref, sb_ref = refs[10:]
    C = CHUNK
    P2 = 2 * C
    ri = lax.broadcasted_iota(jnp.int32, (P2, P2), 0)
    ci = lax.broadcasted_iota(jnp.int32, (P2, P2), 1)
    same = (ri // C) == (ci // C)
    rp, cp = ri % C, ci % C
    eye = (ri == ci).astype(F32)
    ti = lax.broadcasted_iota(jnp.int32, (C, C), 0)
    tj = lax.broadcasted_iota(jnp.int32, (C, C), 1)
    head_a = lax.broadcasted_iota(jnp.int32, (C, LANES), 1) < HEAD

    def stack(x):
        return jnp.concatenate([jnp.where(head_a, x, 0.0), jnp.where(head_a, 0.0, x)], axis=0)

    def chunk(S, row0, lw_ref, kd_ref, b_ref, reverse):
        if reverse:
            strict, incl, tri, last = same & (cp > rp), same & (cp >= rp), (tj >= ti), 0
        else:
            strict, incl, tri, last = same & (cp < rp), same & (cp <= rp), (tj <= ti), C - 1
        sl = pl.ds(row0, C)
        lw = lw_ref[sl, :]
        cum_i = _dot_exact_lhs(tri.astype(BF16), lw, 3)
        e_i = jnp.exp(cum_i)
        e_n = jnp.exp(-cum_i)
        kkc = kk_ref[sl, :]
        QR = jnp.concatenate([stack(kkc * jnp.exp(cum_i - lw)), stack(r_ref[sl, :] * e_i)], axis=0)
        BK = jnp.concatenate([stack(b_ref[sl, :] * e_n), stack(kd_ref[sl, :] * e_n)], axis=0)
        Vs = stack(v_ref[sl, :])
        G = _dot3(QR, BK, _NT)
        L = jnp.where(strict, G[0:P2, 0:P2], 0.0)
        Lk = jnp.where(strict, G[0:P2, P2:2 * P2], 0.0)
        RB = jnp.where(incl, G[P2:2 * P2, 0:P2], 0.0)
        RK = jnp.where(incl, G[P2:2 * P2, P2:2 * P2], 0.0)
        X = eye - L
        Pw = -L
        for _ in range(int(math.log2(C)) - 1):
            Pw = _dot3(Pw, Pw)
            X = X + _dot3(X, Pw)
        QRA = _dot3(QR, S, _NT)
        W = QRA[0:P2] + _dot3(Lk, Vs)
        U = -_dot3(X, W)
        UV = jnp.concatenate([U, Vs], axis=0)
        Ys = QRA[P2:2 * P2] + _dot3(jnp.concatenate([RB, RK], axis=1), UV)
        dS = _dot3(UV.T, BK)
        S_new = (S + dS) * e_i[last:last + 1, :]
        return S_new, Ys[0:C] + Ys[C:P2]

    y_ref[...] = bonus_ref[...]
    if has_init:
        S0f, S0b = s0f_ref[0, 0], s0b_ref[0, 0]
    else:
        S0f = S0b = jnp.zeros((P2, P2), F32)

    def body(c, carry):
        Sf, Sb = carry
        rf = pl.multiple_of(c * C, C)
        rb = pl.multiple_of((n_chunks - 1 - c) * C, C)
        Sf, yf = chunk(Sf, rf, lwf_ref, kdf_ref, bf_ref, False)
        y_ref[pl.ds(rf, C), :] += yf
        Sb, yb = chunk(Sb, rb, lwb_ref, kdb_ref, bb_ref, True)
        y_ref[pl.ds(rb, C), :] += yb
        return Sf, Sb

    Sf, Sb = lax.fori_loop(0, n_chunks, body, (S0f, S0b))
    sf_ref[0, 0] = Sf
    sb_ref[0, 0] = Sb


def _scan(arrs, n_seq, seq_len, init):
    n_tok = n_seq * seq_len
    n_chunks = seq_len // CHUNK
    tok_spec = pl.BlockSpec((seq_len, LANES), lambda b, p: (b, p))
    st_spec = pl.BlockSpec((1, 1, LANES, LANES), lambda b, p: (b, p, 0, 0))
    st_shape = jax.ShapeDtypeStruct((n_seq, N_PAIR, LANES, LANES), F32)
    ins = list(arrs)
    in_specs = [tok_spec] * 10
    if init is not None:
        ins += list(init)
        in_specs += [st_spec, st_spec]
    return pl.pallas_call(
        functools.partial(_scan_kernel, n_chunks, init is not None),
        out_shape=[jax.ShapeDtypeStruct((n_tok, RW), F32), st_shape, st_shape],
        grid=(n_seq, N_PAIR),
        in_specs=in_specs,
        out_specs=[tok_spec, st_spec, st_spec],
        compiler_params=pltpu.CompilerParams(dimension_semantics=("arbitrary", "arbitrary"),
                                             vmem_limit_bytes=VMEM_LIMIT),
        name="scan_init" if init is not None else "scan_zero",
    )(*ins)


def _post_kernel(x_ref, mod_ref, ypre_ref, gsig_ref, y_ref,
                 nmw_ref, wm_ref, bm_ref, wpo_ref, seg_ref, lnw_ref, lnb_ref, gw2_ref, wro_ref, wo_ref,
                 nfw_ref, rw_ref, rb_ref,
                 x1_ref, hn2_ref, code_ref, w4_ref, meta_ref, texp_ref, carry_ref):
    i = pl.program_id(0)

    @pl.when(i == 0)
    def _():
        carry_ref[...] = jnp.zeros_like(carry_ref)

    x = x_ref[...]
    mod = mod_ref[0]
    shift1, scale1, gate1 = mod[:, 0:D], mod[:, D:2 * D], mod[:, 2 * D:3 * D]
    shift2, scale2 = mod[:, 3 * D:4 * D], mod[:, 4 * D:5 * D]
    hn = _modnorm(x, nmw_ref[...], scale1, shift1)
    merge = _sigmoid(_dot(hn.astype(BF16), wm_ref[...]) + bm_ref[...])
    y_pool = _dot(ypre_ref[...], wpo_ref[...])

    y = y_ref[...]
    seg = seg_ref[...]
    mu = _dot_exact_rhs(y, seg, 2) * (1.0 / HEAD)
    yc = y - mu
    var = _dot_exact_rhs(yc * yc, seg, 2) * (1.0 / HEAD)
    yn = yc * lax.rsqrt(var + GN_EPS) * lnw_ref[...] + lnb_ref[...]
    g = _dot(gsig_ref[...], gw2_ref[...])
    y_rwkv = _dot((yn * g).astype(BF16), wro_ref[...])

    mixed = merge[:, 0:D] * y_pool + merge[:, D:2 * D] * y_rwkv
    x1 = x + gate1 * _dot(mixed.astype(BF16), wo_ref[...])
    x1_ref[...] = x1
    hn2 = _modnorm(x1, nfw_ref[...], scale2, shift2)
    hn2_ref[...] = hn2

    logits = _dot3(hn2, rw_ref[...]) + rb_ref[...]
    lane = lax.broadcasted_iota(jnp.int32, logits.shape, 1)
    work = logits
    sel = None
    top = None
    for j in range(TOP_K):
        m = jnp.max(work, axis=-1, keepdims=True)
        if j == 0:
            top = m
        idx = jnp.min(jnp.where(work == m, lane, LANES), axis=-1, keepdims=True)
        pick = lane == idx
        sel = pick if sel is None else (sel | pick)
        work = jnp.where(pick, -jnp.inf, work)
    e = jnp.where(sel, jnp.exp(logits - top), 0.0)
    comb = e / jnp.sum(e, axis=-1, keepdims=True)

    tm = logits.shape[0]
    sel_b = jnp.where(sel, 1.0, 0.0).astype(BF16)
    before = (lax.broadcasted_iota(jnp.int32, (tm, tm), 1) < lax.broadcasted_iota(jnp.int32, (tm, tm), 0))
    rank = carry_ref[...] + _dot(before.astype(BF16), sel_b)
    carry = carry_ref[...] + jnp.sum(sel_b.astype(F32), axis=0, keepdims=True)
    carry_ref[...] = carry
    lower_e = (lax.broadcasted_iota(jnp.int32, (LANES, LANES), 0) < lax.broadcasted_iota(jnp.int32, (LANES, LANES), 1))
    rowpos = _dot(sel_b, lower_e.astype(BF16))
    code = jnp.zeros(logits.shape, jnp.int32)
    w4 = jnp.zeros(logits.shape, F32)
    for k in range(TOP_K):
        mk = sel & (rowpos == float(k))
        ek = jnp.sum(jnp.where(mk, lane, 0), axis=-1, keepdims=True)
        rk = jnp.sum(jnp.where(mk, rank, 0.0), axis=-1, keepdims=True).astype(jnp.int32)
        wk = jnp.sum(jnp.where(mk, comb, 0.0), axis=-1, keepdims=True)
        code = jnp.where(lane == k, ek * (1 << RANK_BITS) + rk, code)
        w4 = jnp.where(lane == k, wk, w4)
    code_ref[...] = code
    w4_ref[...] = w4

    tiles = jnp.floor((carry + (TM_SLOT - 1)) * (1.0 / TM_SLOT))
    incl = (lax.broadcasted_iota(jnp.int32, (LANES, LANES), 0) <= lax.broadcasted_iota(jnp.int32, (LANES, LANES), 1))
    tiles8 = jnp.broadcast_to(tiles, (8, LANES)).astype(BF16)
    cum = _dot(tiles8, incl.astype(BF16))[0:1]
    lane1 = lax.broadcasted_iota(jnp.int32, (1, LANES), 1)
    offs = ((cum - tiles) * TM_SLOT).astype(jnp.int32)
    n_used = jnp.sum(jnp.where(lane1 == N_EXP - 1, cum, 0.0), axis=-1, keepdims=True).astype(jnp.int32)
    meta_ref[...] = jnp.where(lane1 == META_NT, n_used, jnp.where(lane1 < N_EXP, offs, 0))
    tile_id = lax.broadcasted_iota(jnp.int32, (N_SLOT_TILES, LANES), 0).astype(F32)
    done = jnp.where((lax.broadcasted_iota(jnp.int32, (N_SLOT_TILES, LANES), 1) < N_EXP) & (cum <= tile_id), 1, 0)
    texp = jnp.minimum(jnp.sum(done, axis=-1, keepdims=True), N_EXP - 1)
    texp_ref[...] = jnp.broadcast_to(texp, (N_SLOT_TILES, LANES))


def _post(x_all, mod3, ypre, gsig, y, weights):
    def tile(n, dt):
        return pl.BlockSpec((TM_POST, n), lambda i: (i, 0))

    def full(a):
        nd = a.ndim
        return pl.BlockSpec(a.shape, lambda i, _nd=nd: (0,) * _nd)

    def const(shape):
        return pl.BlockSpec(shape, lambda i: (0, 0))

    return pl.pallas_call(
        _post_kernel,
        out_shape=[jax.ShapeDtypeStruct((N_TOK, D), F32), jax.ShapeDtypeStruct((N_TOK, D), F32),
                   jax.ShapeDtypeStruct((N_TOK, LANES), jnp.int32), jax.ShapeDtypeStruct((N_TOK, LANES), F32),
                   jax.ShapeDtypeStruct((1, LANES), jnp.int32),
                   jax.ShapeDtypeStruct((N_SLOT_TILES, LANES), jnp.int32)],
        grid=(N_TOK // TM_POST,),
        in_specs=[
            tile(D, F32),
            pl.BlockSpec((1, 1, N_MOD * D), lambda i: (_mod_row(i, TM_POST), 0, 0)),
            tile(POOL_W, BF16), tile(GATE_LORA, BF16), tile(RW, F32),
        ] + [full(a) for a in weights],
        out_specs=[tile(D, F32), tile(D, F32), tile(LANES, jnp.int32), tile(LANES, F32),
                   const((1, LANES)), const((N_SLOT_TILES, LANES))],
        scratch_shapes=[pltpu.VMEM((1, LANES), F32)],
        compiler_params=pltpu.CompilerParams(dimension_semantics=("arbitrary",), vmem_limit_bytes=VMEM_LIMIT),
        name="post",
    )(x_all, mod3, ypre, gsig, y, *weights)


def _slot_of(code, meta_ref):
    return meta_ref[code >> RANK_BITS] + (code & ((1 << RANK_BITS) - 1))


def _load_codes(code_ref, code_smem, sem):
    cp = pltpu.make_async_copy(code_ref.at[pl.program_id(0)], code_smem, sem)
    cp.start()
    cp.wait()


def _dispatch_kernel(code_ref, meta_ref, hn2_ref, xs_init_ref, xs_ref, code_smem, sem_c, sem):
    del xs_init_ref
    _load_codes(code_ref, code_smem, sem_c)

    def body(t, carry):
        for k in range(TOP_K):
            slot = _slot_of(code_smem[t * TOP_K + k], meta_ref)
            pltpu.make_async_copy(hn2_ref.at[pl.ds(t, 1)], xs_ref.at[pl.ds(slot, 1)], sem).start()
        return carry

    lax.fori_loop(0, TM_DISP, body, 0, unroll=8)
    for _ in range(TOP_K):
        pltpu.make_async_copy(hn2_ref, xs_ref.at[pl.ds(0, TM_DISP)], sem).wait()


def _dispatch(code2d, meta, hn2):
    n_steps = N_TOK // TM_DISP
    return pl.pallas_call(
        _dispatch_kernel,
        out_shape=jax.ShapeDtypeStruct((N_SLOTS, D), F32),
        grid=(n_steps,),
        in_specs=[
            pl.BlockSpec(code2d.shape, lambda i: (0, 0)),
            pl.BlockSpec(memory_space=pltpu.SMEM),
            pl.BlockSpec((TM_DISP, D), lambda i: (i, 0)),
            pl.BlockSpec(memory_space=pl.ANY),
        ],
        out_specs=pl.BlockSpec(memory_space=pl.ANY),
        scratch_shapes=[pltpu.SMEM((TM_DISP * TOP_K,), jnp.int32), pltpu.SemaphoreType.DMA, pltpu.SemaphoreType.DMA],
        input_output_aliases={3: 0},
        compiler_params=pltpu.CompilerParams(dimension_semantics=("arbitrary",), vmem_limit_bytes=VMEM_LIMIT),
        name="dispatch",
    )(code2d, meta, hn2, jnp.zeros((N_SLOTS, D), F32))


def _ffn_kernel(texp_ref, meta_ref, xs_ref, wgu_ref, bgu_ref, wd_ref, bd_ref, ys_ref, wgu_bf, wd_bf):
    i = pl.program_id(0)
    valid = i < meta_ref[META_NT]
    fresh = (i == 0) | (texp_ref[i] != texp_ref[jnp.maximum(i - 1, 0)])

    @pl.when(valid & fresh)
    def _():
        wgu_bf[...] = wgu_ref[0].astype(BF16)
        wd_bf[...] = wd_ref[0].astype(BF16)

    @pl.when(valid)
    def _():
        gu = _dot(xs_ref[...].astype(BF16), wgu_bf[...]) + bgu_ref[0]
        glu = jnp.minimum(gu[:, 0:D_FF], SWIGLU_LIMIT)
        lin = jnp.clip(gu[:, D_FF:2 * D_FF], -SWIGLU_LIMIT, SWIGLU_LIMIT)
        act = glu * _sigmoid(SWIGLU_ALPHA * glu) * (lin + 1.0)
        ys_ref[...] = _dot(act.astype(BF16), wd_bf[...]) + bd_ref[0]

    @pl.when(jnp.logical_not(valid))
    def _():
        ys_ref[...] = jnp.zeros_like(ys_ref)


def _ffn(texp, meta, xs, wgu, bgu, wd, bd):
    return pl.pallas_call(
        _ffn_kernel,
        out_shape=jax.ShapeDtypeStruct((N_SLOTS, D), F32),
        grid_spec=pltpu.PrefetchScalarGridSpec(
            num_scalar_prefetch=2,
            grid=(N_SLOT_TILES,),
            in_specs=[
                pl.BlockSpec((TM_SLOT, D), lambda i, te, me: (i, 0)),
                pl.BlockSpec((1, D, 2 * D_FF), lambda i, te, me: (te[i], 0, 0)),
                pl.BlockSpec((1, 1, 2 * D_FF), lambda i, te, me: (te[i], 0, 0)),
                pl.BlockSpec((1, D_FF, D), lambda i, te, me: (te[i], 0, 0)),
                pl.BlockSpec((1, 1, D), lambda i, te, me: (te[i], 0, 0)),
            ],
            out_specs=pl.BlockSpec((TM_SLOT, D), lambda i, te, me: (i, 0)),
            scratch_shapes=[pltpu.VMEM((D, 2 * D_FF), BF16), pltpu.VMEM((D_FF, D), BF16)],
        ),
        compiler_params=pltpu.CompilerParams(dimension_semantics=("arbitrary",), vmem_limit_bytes=VMEM_LIMIT),
        name="ffn",
    )(texp, meta, xs, wgu, bgu, wd, bd)


def _combine_kernel(code_ref, meta_ref, x1_ref, w4_ref, mod_ref, nfw_ref, ys_ref, o_ref, code_smem, buf, sem_c, sem):
    _load_codes(code_ref, code_smem, sem_c)

    def body(t, carry):
        for k in range(TOP_K):
            slot = _slot_of(code_smem[t * TOP_K + k], meta_ref)
            pltpu.make_async_copy(ys_ref.at[pl.ds(slot, 1)], buf.at[k, pl.ds(t, 1)], sem).start()
        return carry

    lax.fori_loop(0, TM_DISP, body, 0, unroll=8)
    for k in range(TOP_K):
        pltpu.make_async_copy(ys_ref.at[pl.ds(0, TM_DISP)], buf.at[k], sem).wait()
    w4 = w4_ref[...]
    moe = w4[:, 0:1] * buf[0]
    for k in range(1, TOP_K):
        moe = moe + w4[:, k:k + 1] * buf[k]
    gate2 = mod_ref[0][:, 5 * D:6 * D]
    x2 = x1_ref[...] + gate2 * moe
    ms = jnp.mean(x2 * x2, axis=-1, keepdims=True)
    o_ref[...] = x2 * lax.rsqrt(ms + RMS_EPS) * nfw_ref[...]


def _combine(code2d, meta, x1, w4, mod3, nfw, ys):
    n_steps = N_TOK // TM_DISP
    return pl.pallas_call(
        _combine_kernel,
        out_shape=jax.ShapeDtypeStruct((N_TOK, D), F32),
        grid=(n_steps,),
        in_specs=[
            pl.BlockSpec(code2d.shape, lambda i: (0, 0)),
            pl.BlockSpec(memory_space=pltpu.SMEM),
            pl.BlockSpec((TM_DISP, D), lambda i: (i, 0)),
            pl.BlockSpec((TM_DISP, LANES), lambda i: (i, 0)),
            pl.BlockSpec((1, 1, N_MOD * D), lambda i: (_mod_row(i, TM_DISP), 0, 0)),
            pl.BlockSpec((1, D), lambda i: (0, 0)),
            pl.BlockSpec(memory_space=pl.ANY),
        ],
        out_specs=pl.BlockSpec((TM_DISP, D), lambda i: (i, 0)),
        scratch_shapes=[pltpu.SMEM((TM_DISP * TOP_K,), jnp.int32), pltpu.VMEM((TOP_K, TM_DISP, D), F32),
                        pltpu.SemaphoreType.DMA, pltpu.SemaphoreType.DMA],
        compiler_params=pltpu.CompilerParams(dimension_semantics=("arbitrary",), vmem_limit_bytes=VMEM_LIMIT),
        name="combine",
    )(code2d, meta, x1, w4, mod3, nfw, ys)


def _pair_blockdiag(s):
    b = s.shape[0]
    s = s.reshape(b, N_PAIR, 2, HEAD, HEAD)
    z = jnp.zeros((b, N_PAIR, HEAD, HEAD), s.dtype)
    top = jnp.concatenate([s[:, :, 0], z], axis=-1)
    bot = jnp.concatenate([z, s[:, :, 1]], axis=-1)
    return jnp.concatenate([top, bot], axis=-2)


def _pair_unblock(s):
    b = s.shape[0]
    a = s[:, :, 0:HEAD, 0:HEAD]
    c = s[:, :, HEAD:, HEAD:]
    return jnp.stack([a, c], axis=2).reshape(b, 1, N_HEAD, HEAD, HEAD)


def kernel(x_prompt, x_sample, state_fwd, state_bwd, c, c_ctx, w_mod, b_mod, norm_mix_w, w_in, b_merge, pool_w, pool_scale, w_pool_out, shift_mu, decay_w0, decay_w2, iclr_a0, iclr_a2, gate_w2, k_k, k_a, r_k, ln_x_w, ln_x_b, w_rwkv_out, w_o, norm_ffn_w, router_w, router_b, expert_w_gu, expert_b_gu, expert_w_down, expert_b_down, norm_final_w):
    l = 0
    x_all = jnp.concatenate([x_prompt.reshape(N_CTX_TOK, D), x_sample.reshape(N_LAT_TOK, D)], axis=0)
    cvec = jnp.concatenate([c_ctx[None, :], c, jnp.zeros((MOD_ROWS - 1 - N_LAT_SEQ, D), F32)], axis=0)
    mod = _modulation(cvec, w_mod[l], b_mod[l][None, :])
    mod3 = mod.reshape(MOD_ROWS, 1, N_MOD * D)

    w_in_l = w_in[l]
    proj = _projection(x_all, mod3, norm_mix_w[l][None, :], w_in_l[:, :OFF_MERGE].astype(BF16))

    zl = jnp.zeros((LORA, RW), F32)
    dw2 = jnp.stack([jnp.concatenate([decay_w2[l, 0], zl], 0), jnp.concatenate([zl, decay_w2[l, 1]], 0)])
    a2 = jnp.stack([jnp.concatenate([iclr_a2[l, 0], zl], 0), jnp.concatenate([zl, iclr_a2[l, 1]], 0)])
    hid = jnp.arange(RW) // HEAD
    seg = (hid[:, None] == hid[None, :]).astype(BF16)
    front_w = (pool_w[l].astype(BF16), pool_scale[l][None, :], shift_mu[l][None, :],
               decay_w0[l], dw2, iclr_a0[l], a2, k_k[l][None, :], k_a[l][None, :],
               r_k[l].reshape(1, RW), seg)
    f_ctx = _front(proj, False, N_CTX_SEQ, T_CTX, 0, front_w)
    f_lat = _front(proj, True, N_LAT_SEQ, T_LAT, N_CTX_TOK, front_w)

    y_ctx, sf, sb = _scan(f_ctx[2:], N_CTX_SEQ, T_CTX, None)
    init = (_pair_blockdiag(state_fwd[:, l]), _pair_blockdiag(state_bwd[:, l]))
    y_lat, _, _ = _scan(f_lat[2:], N_LAT_SEQ, T_LAT, init)

    ypre = jnp.concatenate([f_ctx[0], f_lat[0]], axis=0)
    gsig = jnp.concatenate([f_ctx[1], f_lat[1]], axis=0)
    y = jnp.concatenate([y_ctx, y_lat], axis=0)

    rw_pad = jnp.concatenate([router_w[l], jnp.zeros((D, LANES - N_EXP), F32)], axis=1)
    rb_pad = jnp.concatenate([router_b[l], jnp.full((LANES - N_EXP,), -1e30, F32)])[None, :]
    post_w = (norm_mix_w[l][None, :], w_in_l[:, OFF_MERGE:].astype(BF16), b_merge[l][None, :],
              w_pool_out[l].astype(BF16), seg, ln_x_w[l][None, :], ln_x_b[l][None, :],
              gate_w2[l].astype(BF16), w_rwkv_out[l].astype(BF16), w_o[l].astype(BF16),
              norm_ffn_w[l][None, :], rw_pad, rb_pad)
    x1, hn2, code, w4, meta, texp = _post(x_all, mod3, ypre, gsig, y, post_w)

    code2d = code[:, :TOP_K].reshape(N_TOK // TM_DISP, TM_DISP * TOP_K)
    meta1 = meta.reshape(LANES)
    xs = _dispatch(code2d, meta1, hn2)
    ys = _ffn(texp[:, 0], meta1, xs, expert_w_gu[l], expert_b_gu[l][:, None, :],
              expert_w_down[l], expert_b_down[l][:, None, :])
    out = _combine(code2d, meta1, x1, w4, mod3, norm_final_w[None, :], ys)
    y_prompt = out[:N_CTX_TOK].reshape(N_CTX_SEQ, T_CTX, D)
    y_sample = out[N_CTX_TOK:].reshape(N_LAT_SEQ, T_LAT, D)
    return (y_prompt, y_sample, _pair_unblock(sf), _pair_unblock(sb))
```

```python
import functools
import math

import jax
import jax.numpy as jnp
from jax import lax
from jax.experimental import pallas as pl
from jax.experimental.pallas import tpu as pltpu

F32 = jnp.float32
BF16 = jnp.bfloat16

D = 1024
N_CTX_SEQ, T_CTX = 32, 256
N_LAT_SEQ, T_LAT = 4, 1024
GRID_W = 64
N_CTX_TOK = N_CTX_SEQ * T_CTX
N_LAT_TOK = N_LAT_SEQ * T_LAT
N_TOK = N_CTX_TOK + N_LAT_TOK

POOL_W = 512
POOL_G = 4
POOL_GD = POOL_W // POOL_G
POOL_WINDOWS = (2, 4, 8, 16)
RW = 512
HEAD = 64
N_HEAD = RW // HEAD
N_PAIR = N_HEAD // 2
LORA = 64
GATE_LORA = 128
N_EXP = 32
TOP_K = 4
D_FF = 1024
SWIGLU_ALPHA = 1.702
SWIGLU_LIMIT = 7.0
RMS_EPS = 1e-6
GN_EPS = 1e-5 * HEAD
N_MOD = 6

OFF_R = POOL_W
SHIFT_COLS = 3 * RW + 4 * LORA + GATE_LORA
OFF_MERGE = OFF_R + SHIFT_COLS
IN_COLS = OFF_MERGE + 2 * D

LANES = 128
CHUNK = 64
PAIRS_PER_GROUP = 4
ROWS = 256
HALO = 64
EXT = ROWS + 2 * HALO
TM_PROJ = 512
TM_POST = 256
TM_DISP = 256
TM_SLOT = 256
N_SLOT_TILES = N_TOK * TOP_K // TM_SLOT + N_EXP
N_SLOTS = N_SLOT_TILES * TM_SLOT
RANK_BITS = 14
META_NT = N_EXP
MOD_ROWS = 8
VMEM_LIMIT = 56 * 1024 * 1024


def _sigmoid(x):
    return 1.0 / (1.0 + jnp.exp(-x))


def _split2(a):
    hi = a.astype(BF16)
    lo = (a - hi.astype(F32)).astype(BF16)
    return hi, lo


_NN = (((1,), (0,)), ((), ()))
_NT = (((1,), (1,)), ((), ()))


def _dot(a, b, dims=_NN):
    return lax.dot_general(a, b, dims, preferred_element_type=F32)


def _dot3(a, b, dims=_NN):
    ah, al = _split2(a)
    bh, bl = _split2(b)
    return _dot(ah, bh, dims) + (_dot(ah, bl, dims) + _dot(al, bh, dims))


def _dot_exact_lhs(a_bf16, b, passes):
    acc = None
    rem = b
    for _ in range(passes):
        part = rem.astype(BF16)
        term = _dot(a_bf16, part)
        acc = term if acc is None else acc + term
        rem = rem - part.astype(F32)
    return acc


def _dot_exact_rhs(a, b_bf16, passes):
    acc = None
    rem = a
    for _ in range(passes):
        part = rem.astype(BF16)
        term = _dot(part, b_bf16)
        acc = term if acc is None else acc + term
        rem = rem - part.astype(F32)
    return acc


def _modnorm(x, w, scale, shift):
    ms = jnp.mean(x * x, axis=-1, keepdims=True)
    return x * lax.rsqrt(ms + RMS_EPS) * w * (1.0 + scale) + shift


def _mod_row(i, tm):
    n_ctx = N_CTX_TOK // tm
    per = T_LAT // tm
    return jnp.where(i < n_ctx, 0, 1 + (i - n_ctx) // per)


def _mod_kernel(c_ref, w_ref, b_ref, o_ref):
    c = c_ref[...]
    s = c * _sigmoid(c)
    o_ref[...] = _dot3(s, w_ref[...]) + b_ref[...]


def _modulation(cvec, w_mod, b_mod):
    return pl.pallas_call(
        _mod_kernel,
        out_shape=jax.ShapeDtypeStruct((MOD_ROWS, N_MOD * D), F32),
        grid=(N_MOD,),
        in_specs=[
            pl.BlockSpec((MOD_ROWS, D), lambda j: (0, 0)),
            pl.BlockSpec((D, D), lambda j: (0, j)),
            pl.BlockSpec((1, D), lambda j: (0, j)),
        ],
        out_specs=pl.BlockSpec((MOD_ROWS, D), lambda j: (0, j)),
        compiler_params=pltpu.CompilerParams(dimension_semantics=("arbitrary",), vmem_limit_bytes=VMEM_LIMIT),
        name="mod",
    )(cvec, w_mod, b_mod)


def _proj_kernel(x_ref, mod_ref, nw_ref, w_ref, o_ref):
    mod = mod_ref[0]
    hn = _modnorm(x_ref[...], nw_ref[...], mod[:, D:2 * D], mod[:, 0:D])
    o_ref[...] = _dot(hn.astype(BF16), w_ref[...])


def _projection(x_all, mod3, norm_w, w_a):
    n_cols = w_a.shape[1]
    return pl.pallas_call(
        _proj_kernel,
        out_shape=jax.ShapeDtypeStruct((N_TOK, n_cols), F32),
        grid=(N_TOK // TM_PROJ,),
        in_specs=[
            pl.BlockSpec((TM_PROJ, D), lambda i: (i, 0)),
            pl.BlockSpec((1, 1, N_MOD * D), lambda i: (_mod_row(i, TM_PROJ), 0, 0)),
            pl.BlockSpec((1, D), lambda i: (0, 0)),
            pl.BlockSpec((D, n_cols), lambda i: (0, 0)),
        ],
        out_specs=pl.BlockSpec((TM_PROJ, n_cols), lambda i: (i, 0)),
        compiler_params=pltpu.CompilerParams(dimension_semantics=("arbitrary",), vmem_limit_bytes=VMEM_LIMIT),
        name="proj",
    )(x_all, mod3, norm_w, w_a)


def _front_kernel(grid_mode, n_chunks, seq_len,
                  cur_ref, prev_ref, next_ref, poolw_ref, pscale_ref, mu_ref,
                  dw0_ref, dw2_ref, a0_ref, a2_ref, kk_ref, ka_ref, rk_ref, seg_ref,
                  ypre_ref, gsig_ref, r_ref, v_ref, kkn_ref, bonus_ref,
                  lwf_ref, lwb_ref, kdf_ref, kdb_ref, bf_ref, bb_ref):
    c = pl.program_id(1)
    cur = cur_ref[...]
    prev = jnp.where(c > 0, prev_ref[...], 0.0)
    nxt = jnp.where(c < n_chunks - 1, next_ref[...], 0.0)
    ext = jnp.concatenate([prev, cur, nxt], axis=0)

    def down(x, s):
        return pltpu.roll(x, s, 0)

    def up(x, s):
        return pltpu.roll(x, EXT - s, 0)

    def mid(x):
        return x[HALO:HALO + ROWS]

    row = lax.broadcasted_iota(jnp.int32, (ROWS, 1), 0)
    t_seq = c * ROWS + row

    outs = []
    for gi, win in enumerate(POOL_WINDOWS):
        h = win // 2
        u = ext[:, gi * POOL_GD:(gi + 1) * POOL_GD]
        trail, lead, s = u, u, 1
        while s < h:
            trail = trail + down(trail, s)
            lead = lead + up(lead, s)
            s *= 2
        total = mid(down(trail, 1) + lead)
        cnt = (jnp.minimum(t_seq + h, seq_len) - jnp.maximum(t_seq - h, 0)).astype(F32)
        pooled = total / cnt - mid(u)
        outs.append(_dot(pooled.astype(BF16), poolw_ref[gi]))
    mixed = jnp.concatenate(outs, axis=1) * pscale_ref[...]
    ypre_ref[...] = mixed.astype(BF16)

    pe = ext[:, OFF_R:OFF_MERGE]
    p = mid(pe)
    lane = lax.broadcasted_iota(jnp.int32, (1, SHIFT_COLS), 1)
    if grid_mode:
        col = t_seq % GRID_W
        left = jnp.where(col > 0, mid(down(pe, 1)), 0.0)
        right = jnp.where(col < GRID_W - 1, mid(up(pe, 1)), 0.0)
        upn = pe[0:ROWS]
        dnn = pe[2 * HALO:2 * HALO + ROWS]
        q = lane % 4
        sh = jnp.where(q == 0, left, jnp.where(q == 1, right, jnp.where(q == 2, upn, dnn)))
    else:
        sh = jnp.where(lane % 2 == 0, mid(down(pe, 1)), mid(up(pe, 1)))
    pm = p + mu_ref[...] * (sh - p)

    r = pm[:, 0:RW]
    k = pm[:, RW:2 * RW]
    v = pm[:, 2 * RW:3 * RW]
    o = 3 * RW
    w_lo = jnp.tanh(pm[:, o:o + 2 * LORA])
    a_lo = pm[:, o + 2 * LORA:o + 4 * LORA]
    gsig_ref[...] = _sigmoid(pm[:, o + 4 * LORA:o + 4 * LORA + GATE_LORA]).astype(BF16)

    seg = seg_ref[...]
    kkr = k * kk_ref[...]
    ss = _dot_exact_rhs(kkr * kkr, seg, 2)
    kkn = kkr / jnp.maximum(jnp.sqrt(ss), 1e-12)
    r_ref[...] = r
    v_ref[...] = v
    kkn_ref[...] = kkn

    bonus = None
    for d, (lw_ref, kd_ref, b_ref) in enumerate(((lwf_ref, kdf_ref, bf_ref), (lwb_ref, kdb_ref, bb_ref))):
        z = dw0_ref[d:d + 1, :] + _dot3(w_lo, dw2_ref[d])
        lw_ref[...] = -math.exp(-0.5) * _sigmoid(z)
        a = _sigmoid(a0_ref[d:d + 1, :] + _dot3(a_lo, a2_ref[d]))
        kd = k * (1.0 + (a - 1.0) * ka_ref[...])
        kd_ref[...] = kd
        b_ref[...] = kkn * a
        bd = _dot_exact_rhs(r * kd * rk_ref[...], seg, 2) * v
        bonus = bd if bonus is None else bonus + bd
    bonus_ref[...] = bonus


def _front(proj, grid_mode, n_seq, seq_len, tok0, weights):
    n_chunks = seq_len // ROWS
    n_tok = n_seq * seq_len
    blk0 = tok0 // ROWS
    hpb = ROWS // HALO
    last_halo = N_TOK // HALO - 1

    def cur_map(b, c):
        return (blk0 + b * n_chunks + c, 0)

    def prev_map(b, c):
        return (jnp.maximum((blk0 + b * n_chunks + c) * hpb - 1, 0), 0)

    def next_map(b, c):
        return (jnp.minimum((blk0 + b * n_chunks + c + 1) * hpb, last_halo), 0)

    def out_map(b, c):
        return (b * n_chunks + c, 0)

    def full(a):
        nd = a.ndim
        return pl.BlockSpec(a.shape, lambda b, c, _nd=nd: (0,) * _nd)

    f32_out = jax.ShapeDtypeStruct((n_tok, RW), F32)
    out_shape = [jax.ShapeDtypeStruct((n_tok, POOL_W), BF16), jax.ShapeDtypeStruct((n_tok, GATE_LORA), BF16)]
    out_shape += [f32_out] * 10
    out_specs = [pl.BlockSpec((ROWS, POOL_W), out_map), pl.BlockSpec((ROWS, GATE_LORA), out_map)]
    out_specs += [pl.BlockSpec((ROWS, RW), out_map)] * 10
    return pl.pallas_call(
        functools.partial(_front_kernel, grid_mode, n_chunks, seq_len),
        out_shape=out_shape,
        grid=(n_seq, n_chunks),
        in_specs=[
            pl.BlockSpec((ROWS, OFF_MERGE), cur_map),
            pl.BlockSpec((HALO, OFF_MERGE), prev_map),
            pl.BlockSpec((HALO, OFF_MERGE), next_map),
        ] + [full(a) for a in weights],
        out_specs=out_specs,
        compiler_params=pltpu.CompilerParams(dimension_semantics=("arbitrary", "arbitrary"),
                                             vmem_limit_bytes=VMEM_LIMIT),
        name="front_grid" if grid_mode else "front_ctx",
    )(proj, proj, proj, *weights)


def _scan_kernel(n_chunks, has_init, *refs):
    (r_ref, v_ref, kk_ref, bonus_ref, lwf_ref, lwb_ref, kdf_ref, kdb_ref, bf_ref, bb_ref) = refs[:10]
    if has_init:
        s0f_ref, s0b_ref = refs[10:12]
        y_ref, sf_ref, sb_ref = refs[12:]
    else:
        y_ref, sf_ref, sb_ref = refs[10:]
    C = CHUNK
    P2 = 2 * C
    ri = lax.broadcasted_iota(jnp.int32, (P2, P2), 0)
    ci = lax.broadcasted_iota(jnp.int32, (P2, P2), 1)
    same = (ri // C) == (ci // C)
    rp, cp = ri % C, ci % C
    eye = (ri == ci).astype(F32)
    ti = lax.broadcasted_iota(jnp.int32, (C, C), 0)
    tj = lax.broadcasted_iota(jnp.int32, (C, C), 1)
    head_a = lax.broadcasted_iota(jnp.int32, (C, LANES), 1) < HEAD

    def stack(x):
        return jnp.concatenate([jnp.where(head_a, x, 0.0), jnp.where(head_a, 0.0, x)], axis=0)

    def d3(a, b, dims=_NN):
        return _dot(a[0], b[0], dims) + (_dot(a[0], b[1], dims) + _dot(a[1], b[0], dims))

    def chunks(chains):
        n = range(len(chains))
        masks, sls = [], []
        for (s_ref, p, row0, lw_ref, kd_ref, b_ref, reverse) in chains:
            if reverse:
                masks.append((same & (cp > rp), same & (cp >= rp), (tj >= ti), 0))
            else:
                masks.append((same & (cp < rp), same & (cp <= rp), (tj <= ti), C - 1))
            sls.append((pl.ds(row0, C), slice(p * LANES, (p + 1) * LANES)))
        lw = [chains[i][3][sls[i]] for i in n]
        cum = [_dot_exact_lhs(masks[i][2].astype(BF16), lw[i], 3) for i in n]
        e_i = [jnp.exp(cum[i]) for i in n]
        e_n = [jnp.exp(-cum[i]) for i in n]
        QR = [_split2(jnp.concatenate([stack(kk_ref[sls[i]] * jnp.exp(cum[i] - lw[i])),
                                       stack(r_ref[sls[i]] * e_i[i])], axis=0)) for i in n]
        BK = [_split2(jnp.concatenate([stack(chains[i][5][sls[i]] * e_n[i]),
                                       stack(chains[i][4][sls[i]] * e_n[i])], axis=0)) for i in n]
        v_st = [stack(v_ref[sls[i]]) for i in n]
        Vs = [_split2(v_st[i]) for i in n]
        S = [chains[i][0][0, chains[i][1]] for i in n]
        G = [d3(QR[i], BK[i], _NT) for i in n]
        L = [jnp.where(masks[i][0], G[i][0:P2, 0:P2], 0.0) for i in n]
        Lk = [jnp.where(masks[i][0], G[i][0:P2, P2:2 * P2], 0.0) for i in n]
        RBK = [jnp.concatenate([jnp.where(masks[i][1], G[i][P2:2 * P2, 0:P2], 0.0),
                                jnp.where(masks[i][1], G[i][P2:2 * P2, P2:2 * P2], 0.0)], axis=1) for i in n]
        X = [eye - L[i] for i in n]
        Pw = [_split2(-L[i]) for i in n]
        for _ in range(int(math.log2(C)) - 1):
            Pw = [_split2(d3(Pw[i], Pw[i])) for i in n]
            X = [X[i] + d3(_split2(X[i]), Pw[i]) for i in n]
        QRA = [d3(QR[i], _split2(S[i]), _NT) for i in n]
        W = [QRA[i][0:P2] + d3(_split2(Lk[i]), Vs[i]) for i in n]
        U = [-d3(_split2(X[i]), _split2(W[i])) for i in n]
        UV = [jnp.concatenate([U[i], v_st[i]], axis=0) for i in n]
        Ys = [QRA[i][P2:2 * P2] + d3(_split2(RBK[i]), _split2(UV[i])) for i in n]
        dS = [d3(_split2(UV[i].T), BK[i]) for i in n]
        for i in n:
            last = masks[i][3]
            chains[i][0][0, chains[i][1]] = (S[i] + dS[i]) * e_i[i][last:last + 1, :]
            y_ref[sls[i]] += Ys[i][0:C] + Ys[i][C:P2]

    y_ref[...] = bonus_ref[...]
    if has_init:
        sf_ref[...] = s0f_ref[...]
        sb_ref[...] = s0b_ref[...]
    else:
        sf_ref[...] = jnp.zeros_like(sf_ref)
        sb_ref[...] = jnp.zeros_like(sb_ref)

    def body(c, carry):
        rf = pl.multiple_of(c * C, C)
        rb = pl.multiple_of((n_chunks - 1 - c) * C, C)
        for p0 in range(0, N_PAIR, PAIRS_PER_GROUP):
            chains = []
            for p in range(p0, p0 + PAIRS_PER_GROUP):
                chains.append((sf_ref, p, rf, lwf_ref, kdf_ref, bf_ref, False))
                chains.append((sb_ref, p, rb, lwb_ref, kdb_ref, bb_ref, True))
            chunks(chains)
        return carry

    lax.fori_loop(0, n_chunks, body, 0)


def _scan(arrs, n_seq, seq_len, init):
    n_tok = n_seq * seq_len
    n_chunks = seq_len // CHUNK
    tok_spec = pl.BlockSpec((seq_len, RW), lambda b: (b, 0))
    st_spec = pl.BlockSpec((1, N_PAIR, LANES, LANES), lambda b: (b, 0, 0, 0))
    st_shape = jax.ShapeDtypeStruct((n_seq, N_PAIR, LANES, LANES), F32)
    ins = list(arrs)
    in_specs = [tok_spec] * 10
    if init is not None:
        ins += list(init)
        in_specs += [st_spec, st_spec]
    return pl.pallas_call(
        functools.partial(_scan_kernel, n_chunks, init is not None),
        out_shape=[jax.ShapeDtypeStruct((n_tok, RW), F32), st_shape, st_shape],
        grid=(n_seq,),
        in_specs=in_specs,
        out_specs=[tok_spec, st_spec, st_spec],
        compiler_params=pltpu.CompilerParams(dimension_semantics=("arbitrary",), vmem_limit_bytes=VMEM_LIMIT),
        name="scan_init" if init is not None else "scan_zero",
    )(*ins)


def _post_kernel(x_ref, mod_ref, ypre_ref, gsig_ref, y_ref,
                 nmw_ref, wm_ref, bm_ref, wpo_ref, seg_ref, lnw_ref, lnb_ref, gw2_ref, wro_ref, wo_ref,
                 nfw_ref, rw_ref, rb_ref,
                 x1_ref, hn2_ref, code_ref, w4_ref, meta_ref, texp_ref, carry_ref):
    i = pl.program_id(0)

    @pl.when(i == 0)
    def _():
        carry_ref[...] = jnp.zeros_like(carry_ref)

    x = x_ref[...]
    mod = mod_ref[0]
    shift1, scale1, gate1 = mod[:, 0:D], mod[:, D:2 * D], mod[:, 2 * D:3 * D]
    shift2, scale2 = mod[:, 3 * D:4 * D], mod[:, 4 * D:5 * D]
    hn = _modnorm(x, nmw_ref[...], scale1, shift1)
    merge = _sigmoid(_dot(hn.astype(BF16), wm_ref[...]) + bm_ref[...])
    y_pool = _dot(ypre_ref[...], wpo_ref[...])

    y = y_ref[...]
    seg = seg_ref[...]
    mu = _dot_exact_rhs(y, seg, 2) * (1.0 / HEAD)
    yc = y - mu
    var = _dot_exact_rhs(yc * yc, seg, 2) * (1.0 / HEAD)
    yn = yc * lax.rsqrt(var + GN_EPS) * lnw_ref[...] + lnb_ref[...]
    g = _dot(gsig_ref[...], gw2_ref[...])
    y_rwkv = _dot((yn * g).astype(BF16), wro_ref[...])

    mixed = merge[:, 0:D] * y_pool + merge[:, D:2 * D] * y_rwkv
    x1 = x + gate1 * _dot(mixed.astype(BF16), wo_ref[...])
    x1_ref[...] = x1
    hn2 = _modnorm(x1, nfw_ref[...], scale2, shift2)
    hn2_ref[...] = hn2

    logits = _dot3(hn2, rw_ref[...]) + rb_ref[...]
    lane = lax.broadcasted_iota(jnp.int32, logits.shape, 1)
    work = logits
    sel = None
    top = None
    for j in range(TOP_K):
        m = jnp.max(work, axis=-1, keepdims=True)
        if j == 0:
            top = m
        idx = jnp.min(jnp.where(work == m, lane, LANES), axis=-1, keepdims=True)
        pick = lane == idx
        sel = pick if sel is None else (sel | pick)
        work = jnp.where(pick, -jnp.inf, work)
    e = jnp.where(sel, jnp.exp(logits - top), 0.0)
    comb = e / jnp.sum(e, axis=-1, keepdims=True)

    tm = logits.shape[0]
    sel_b = jnp.where(sel, 1.0, 0.0).astype(BF16)
    before = (lax.broadcasted_iota(jnp.int32, (tm, tm), 1) < lax.broadcasted_iota(jnp.int32, (tm, tm), 0))
    rank = carry_ref[...] + _dot(before.astype(BF16), sel_b)
    carry = carry_ref[...] + jnp.sum(sel_b.astype(F32), axis=0, keepdims=True)
    carry_ref[...] = carry
    lower_e = (lax.broadcasted_iota(jnp.int32, (LANES, LANES), 0) < lax.broadcasted_iota(jnp.int32, (LANES, LANES), 1))
    rowpos = _dot(sel_b, lower_e.astype(BF16))
    code = jnp.zeros(logits.shape, jnp.int32)
    w4 = jnp.zeros(logits.shape, F32)
    for k in range(TOP_K):
        mk = sel & (rowpos == float(k))
        ek = jnp.sum(jnp.where(mk, lane, 0), axis=-1, keepdims=True)
        rk = jnp.sum(jnp.where(mk, rank, 0.0), axis=-1, keepdims=True).astype(jnp.int32)
        wk = jnp.sum(jnp.where(mk, comb, 0.0), axis=-1, keepdims=True)
        code = jnp.where(lane == k, ek * (1 << RANK_BITS) + rk, code)
        w4 = jnp.where(lane == k, wk, w4)
    code_ref[...] = code
    w4_ref[...] = w4

    tiles = jnp.floor((carry + (TM_SLOT - 1)) * (1.0 / TM_SLOT))
    incl = (lax.broadcasted_iota(jnp.int32, (LANES, LANES), 0) <= lax.broadcasted_iota(jnp.int32, (LANES, LANES), 1))
    tiles8 = jnp.broadcast_to(tiles, (8, LANES)).astype(BF16)
    cum = _dot(tiles8, incl.astype(BF16))[0:1]
    lane1 = lax.broadcasted_iota(jnp.int32, (1, LANES), 1)
    offs = ((cum - tiles) * TM_SLOT).astype(jnp.int32)
    n_used = jnp.sum(jnp.where(lane1 == N_EXP - 1, cum, 0.0), axis=-1, keepdims=True).astype(jnp.int32)
    meta_ref[...] = jnp.where(lane1 == META_NT, n_used, jnp.where(lane1 < N_EXP, offs, 0))
    tile_id = lax.broadcasted_iota(jnp.int32, (N_SLOT_TILES, LANES), 0).astype(F32)
    done = jnp.where((lax.broadcasted_iota(jnp.int32, (N_SLOT_TILES, LANES), 1) < N_EXP) & (cum <= tile_id), 1, 0)
    texp = jnp.minimum(jnp.sum(done, axis=-1, keepdims=True), N_EXP - 1)
    texp_ref[...] = jnp.broadcast_to(texp, (N_SLOT_TILES, LANES))


def _post(x_all, mod3, ypre, gsig, y, weights):
    def tile(n, dt):
        return pl.BlockSpec((TM_POST, n), lambda i: (i, 0))

    def full(a):
        nd = a.ndim
        return pl.BlockSpec(a.shape, lambda i, _nd=nd: (0,) * _nd)

    def const(shape):
        return pl.BlockSpec(shape, lambda i: (0, 0))

    return pl.pallas_call(
        _post_kernel,
        out_shape=[jax.ShapeDtypeStruct((N_TOK, D), F32), jax.ShapeDtypeStruct((N_TOK, D), F32),
                   jax.ShapeDtypeStruct((N_TOK, LANES), jnp.int32), jax.ShapeDtypeStruct((N_TOK, LANES), F32),
                   jax.ShapeDtypeStruct((1, LANES), jnp.int32),
                   jax.ShapeDtypeStruct((N_SLOT_TILES, LANES), jnp.int32)],
        grid=(N_TOK // TM_POST,),
        in_specs=[
            tile(D, F32),
            pl.BlockSpec((1, 1, N_MOD * D), lambda i: (_mod_row(i, TM_POST), 0, 0)),
            tile(POOL_W, BF16), tile(GATE_LORA, BF16), tile(RW, F32),
        ] + [full(a) for a in weights],
        out_specs=[tile(D, F32), tile(D, F32), tile(LANES, jnp.int32), tile(LANES, F32),
                   const((1, LANES)), const((N_SLOT_TILES, LANES))],
        scratch_shapes=[pltpu.VMEM((1, LANES), F32)],
        compiler_params=pltpu.CompilerParams(dimension_semantics=("arbitrary",), vmem_limit_bytes=VMEM_LIMIT),
        name="post",
    )(x_all, mod3, ypre, gsig, y, *weights)


def _slot_of(code, meta_ref):
    return meta_ref[code >> RANK_BITS] + (code & ((1 << RANK_BITS) - 1))


def _load_codes(code_ref, code_smem, sem):
    cp = pltpu.make_async_copy(code_ref.at[pl.program_id(0)], code_smem, sem)
    cp.start()
    cp.wait()


def _dispatch_kernel(code_ref, meta_ref, hn2_ref, xs_init_ref, xs_ref, code_smem, sem_c, sem):
    del xs_init_ref
    _load_codes(code_ref, code_smem, sem_c)

    def body(t, carry):
        for k in range(TOP_K):
            slot = _slot_of(code_smem[t * TOP_K + k], meta_ref)
            pltpu.make_async_copy(hn2_ref.at[pl.ds(t, 1)], xs_ref.at[pl.ds(slot, 1)], sem).start()
        return carry

    lax.fori_loop(0, TM_DISP, body, 0, unroll=8)
    for _ in range(TOP_K):
        pltpu.make_async_copy(hn2_ref, xs_ref.at[pl.ds(0, TM_DISP)], sem).wait()


def _dispatch(code2d, meta, hn2):
    n_steps = N_TOK // TM_DISP
    return pl.pallas_call(
        _dispatch_kernel,
        out_shape=jax.ShapeDtypeStruct((N_SLOTS, D), F32),
        grid=(n_steps,),
        in_specs=[
            pl.BlockSpec(code2d.shape, lambda i: (0, 0)),
            pl.BlockSpec(memory_space=pltpu.SMEM),
            pl.BlockSpec((TM_DISP, D), lambda i: (i, 0)),
            pl.BlockSpec(memory_space=pl.ANY),
        ],
        out_specs=pl.BlockSpec(memory_space=pl.ANY),
        scratch_shapes=[pltpu.SMEM((TM_DISP * TOP_K,), jnp.int32), pltpu.SemaphoreType.DMA, pltpu.SemaphoreType.DMA],
        input_output_aliases={3: 0},
        compiler_params=pltpu.CompilerParams(dimension_semantics=("arbitrary",), vmem_limit_bytes=VMEM_LIMIT),
        name="dispatch",
    )(code2d, meta, hn2, jnp.zeros((N_SLOTS, D), F32))


def _ffn_kernel(texp_ref, meta_ref, xs_ref, wgu_ref, bgu_ref, wd_ref, bd_ref, ys_ref, wgu_bf, wd_bf):
    i = pl.program_id(0)
    valid = i < meta_ref[META_NT]
    fresh = (i == 0) | (texp_ref[i] != texp_ref[jnp.maximum(i - 1, 0)])

    @pl.when(valid & fresh)
    def _():
        wgu_bf[...] = wgu_ref[0].astype(BF16)
        wd_bf[...] = wd_ref[0].astype(BF16)

    @pl.when(valid)
    def _():
        gu = _dot(xs_ref[...].astype(BF16), wgu_bf[...]) + bgu_ref[0]
        glu = jnp.minimum(gu[:, 0:D_FF], SWIGLU_LIMIT)
        lin = jnp.clip(gu[:, D_FF:2 * D_FF], -SWIGLU_LIMIT, SWIGLU_LIMIT)
        act = glu * _sigmoid(SWIGLU_ALPHA * glu) * (lin + 1.0)
        ys_ref[...] = _dot(act.astype(BF16), wd_bf[...]) + bd_ref[0]

    @pl.when(jnp.logical_not(valid))
    def _():
        ys_ref[...] = jnp.zeros_like(ys_ref)


def _ffn(texp, meta, xs, wgu, bgu, wd, bd):
    return pl.pallas_call(
        _ffn_kernel,
        out_shape=jax.ShapeDtypeStruct((N_SLOTS, D), F32),
        grid_spec=pltpu.PrefetchScalarGridSpec(
            num_scalar_prefetch=2,
            grid=(N_SLOT_TILES,),
            in_specs=[
                pl.BlockSpec((TM_SLOT, D), lambda i, te, me: (i, 0)),
                pl.BlockSpec((1, D, 2 * D_FF), lambda i, te, me: (te[i], 0, 0)),
                pl.BlockSpec((1, 1, 2 * D_FF), lambda i, te, me: (te[i], 0, 0)),
                pl.BlockSpec((1, D_FF, D), lambda i, te, me: (te[i], 0, 0)),
                pl.BlockSpec((1, 1, D), lambda i, te, me: (te[i], 0, 0)),
            ],
            out_specs=pl.BlockSpec((TM_SLOT, D), lambda i, te, me: (i, 0)),
            scratch_shapes=[pltpu.VMEM((D, 2 * D_FF), BF16), pltpu.VMEM((D_FF, D), BF16)],
        ),
        compiler_params=pltpu.CompilerParams(dimension_semantics=("arbitrary",), vmem_limit_bytes=VMEM_LIMIT),
        name="ffn",
    )(texp, meta, xs, wgu, bgu, wd, bd)


def _combine_kernel(code_ref, meta_ref, x1_ref, w4_ref, mod_ref, nfw_ref, ys_ref, o_ref, code_smem, buf, sem_c, sem):
    _load_codes(code_ref, code_smem, sem_c)

    def body(t, carry):
        for k in range(TOP_K):
            slot = _slot_of(code_smem[t * TOP_K + k], meta_ref)
            pltpu.make_async_copy(ys_ref.at[pl.ds(slot, 1)], buf.at[k, pl.ds(t, 1)], sem).start()
        return carry

    lax.fori_loop(0, TM_DISP, body, 0, unroll=8)
    for k in range(TOP_K):
        pltpu.make_async_copy(ys_ref.at[pl.ds(0, TM_DISP)], buf.at[k], sem).wait()
    w4 = w4_ref[...]
    moe = w4[:, 0:1] * buf[0]
    for k in range(1, TOP_K):
        moe = moe + w4[:, k:k + 1] * buf[k]
    gate2 = mod_ref[0][:, 5 * D:6 * D]
    x2 = x1_ref[...] + gate2 * moe
    ms = jnp.mean(x2 * x2, axis=-1, keepdims=True)
    o_ref[...] = x2 * lax.rsqrt(ms + RMS_EPS) * nfw_ref[...]


def _combine(code2d, meta, x1, w4, mod3, nfw, ys):
    n_steps = N_TOK // TM_DISP
    return pl.pallas_call(
        _combine_kernel,
        out_shape=jax.ShapeDtypeStruct((N_TOK, D), F32),
        grid=(n_steps,),
        in_specs=[
            pl.BlockSpec(code2d.shape, lambda i: (0, 0)),
            pl.BlockSpec(memory_space=pltpu.SMEM),
            pl.BlockSpec((TM_DISP, D), lambda i: (i, 0)),
            pl.BlockSpec((TM_DISP, LANES), lambda i: (i, 0)),
            pl.BlockSpec((1, 1, N_MOD * D), lambda i: (_mod_row(i, TM_DISP), 0, 0)),
            pl.BlockSpec((1, D), lambda i: (0, 0)),
            pl.BlockSpec(memory_space=pl.ANY),
        ],
        out_specs=pl.BlockSpec((TM_DISP, D), lambda i: (i, 0)),
        scratch_shapes=[pltpu.SMEM((TM_DISP * TOP_K,), jnp.int32), pltpu.VMEM((TOP_K, TM_DISP, D), F32),
                        pltpu.SemaphoreType.DMA, pltpu.SemaphoreType.DMA],
        compiler_params=pltpu.CompilerParams(dimension_semantics=("arbitrary",), vmem_limit_bytes=VMEM_LIMIT),
        name="combine",
    )(code2d, meta, x1, w4, mod3, nfw, ys)


def _pair_blockdiag(s):
    b = s.shape[0]
    s = s.reshape(b, N_PAIR, 2, HEAD, HEAD)
    z = jnp.zeros((b, N_PAIR, HEAD, HEAD), s.dtype)
    top = jnp.concatenate([s[:, :, 0], z], axis=-1)
    bot = jnp.concatenate([z, s[:, :, 1]], axis=-1)
    return jnp.concatenate([top, bot], axis=-2)


def _pair_unblock(s):
    b = s.shape[0]
    a = s[:, :, 0:HEAD, 0:HEAD]
    c = s[:, :, HEAD:, HEAD:]
    return jnp.stack([a, c], axis=2).reshape(b, 1, N_HEAD, HEAD, HEAD)


def kernel(x_prompt, x_sample, state_fwd, state_bwd, c, c_ctx, w_mod, b_mod, norm_mix_w, w_in, b_merge, pool_w, pool_scale, w_pool_out, shift_mu, decay_w0, decay_w2, iclr_a0, iclr_a2, gate_w2, k_k, k_a, r_k, ln_x_w, ln_x_b, w_rwkv_out, w_o, norm_ffn_w, router_w, router_b, expert_w_gu, expert_b_gu, expert_w_down, expert_b_down, norm_final_w):
    l = 0
    x_all = jnp.concatenate([x_prompt.reshape(N_CTX_TOK, D), x_sample.reshape(N_LAT_TOK, D)], axis=0)
    cvec = jnp.concatenate([c_ctx[None, :], c, jnp.zeros((MOD_ROWS - 1 - N_LAT_SEQ, D), F32)], axis=0)
    mod = _modulation(cvec, w_mod[l], b_mod[l][None, :])
    mod3 = mod.reshape(MOD_ROWS, 1, N_MOD * D)

    w_in_l = w_in[l]
    proj = _projection(x_all, mod3, norm_mix_w[l][None, :], w_in_l[:, :OFF_MERGE].astype(BF16))

    zl = jnp.zeros((LORA, RW), F32)
    dw2 = jnp.stack([jnp.concatenate([decay_w2[l, 0], zl], 0), jnp.concatenate([zl, decay_w2[l, 1]], 0)])
    a2 = jnp.stack([jnp.concatenate([iclr_a2[l, 0], zl], 0), jnp.concatenate([zl, iclr_a2[l, 1]], 0)])
    hid = jnp.arange(RW) // HEAD
    seg = (hid[:, None] == hid[None, :]).astype(BF16)
    front_w = (pool_w[l].astype(BF16), pool_scale[l][None, :], shift_mu[l][None, :],
               decay_w0[l], dw2, iclr_a0[l], a2, k_k[l][None, :], k_a[l][None, :],
               r_k[l].reshape(1, RW), seg)
    f_ctx = _front(proj, False, N_CTX_SEQ, T_CTX, 0, front_w)
    f_lat = _front(proj, True, N_LAT_SEQ, T_LAT, N_CTX_TOK, front_w)

    y_ctx, sf, sb = _scan(f_ctx[2:], N_CTX_SEQ, T_CTX, None)
    init = (_pair_blockdiag(state_fwd[:, l]), _pair_blockdiag(state_bwd[:, l]))
    y_lat, _, _ = _scan(f_lat[2:], N_LAT_SEQ, T_LAT, init)

    ypre = jnp.concatenate([f_ctx[0], f_lat[0]], axis=0)
    gsig = jnp.concatenate([f_ctx[1], f_lat[1]], axis=0)
    y = jnp.concatenate([y_ctx, y_lat], axis=0)

    rw_pad = jnp.concatenate([router_w[l], jnp.zeros((D, LANES - N_EXP), F32)], axis=1)
    rb_pad = jnp.concatenate([router_b[l], jnp.full((LANES - N_EXP,), -1e30, F32)])[None, :]
    post_w = (norm_mix_w[l][None, :], w_in_l[:, OFF_MERGE:].astype(BF16), b_merge[l][None, :],
              w_pool_out[l].astype(BF16), seg, ln_x_w[l][None, :], ln_x_b[l][None, :],
              gate_w2[l].astype(BF16), w_rwkv_out[l].astype(BF16), w_o[l].astype(BF16),
              norm_ffn_w[l][None, :], rw_pad, rb_pad)
    x1, hn2, code, w4, meta, texp = _post(x_all, mod3, ypre, gsig, y, post_w)

    code2d = code[:, :TOP_K].reshape(N_TOK // TM_DISP, TM_DISP * TOP_K)
    meta1 = meta.reshape(LANES)
    xs = _dispatch(code2d, meta1, hn2)
    ys = _ffn(texp[:, 0], meta1, xs, expert_w_gu[l], expert_b_gu[l][:, None, :],
              expert_w_down[l], expert_b_down[l][:, None, :])
    out = _combine(code2d, meta1, x1, w4, mod3, norm_final_w[None, :], ys)
    y_prompt = out[:N_CTX_TOK].reshape(N_CTX_SEQ, T_CTX, D)
    y_sample = out[N_CTX_TOK:].reshape(N_LAT_SEQ, T_LAT, D)
    return (y_prompt, y_sample, _pair_unblock(sf), _pair_unblock(sb))
```

```python
import functools
import math

import jax
import jax.numpy as jnp
from jax import lax
from jax.experimental import pallas as pl
from jax.experimental.pallas import tpu as pltpu

F32 = jnp.float32
BF16 = jnp.bfloat16

D = 1024
N_CTX_SEQ, T_CTX = 32, 256
N_LAT_SEQ, T_LAT = 4, 1024
GRID_W = 64
N_CTX_TOK = N_CTX_SEQ * T_CTX
N_LAT_TOK = N_LAT_SEQ * T_LAT
N_TOK = N_CTX_TOK + N_LAT_TOK

POOL_W = 512
POOL_G = 4
POOL_GD = POOL_W // POOL_G
POOL_WINDOWS = (2, 4, 8, 16)
RW = 512
HEAD = 64
N_HEAD = RW // HEAD
N_PAIR = N_HEAD // 2
LORA = 64
GATE_LORA = 128
N_EXP = 32
TOP_K = 4
D_FF = 1024
SWIGLU_ALPHA = 1.702
SWIGLU_LIMIT = 7.0
RMS_EPS = 1e-6
GN_EPS = 1e-5 * HEAD
N_MOD = 6

OFF_R = POOL_W
SHIFT_COLS = 3 * RW + 4 * LORA + GATE_LORA
OFF_MERGE = OFF_R + SHIFT_COLS
IN_COLS = OFF_MERGE + 2 * D

LANES = 128
CHUNK = 64
PAIRS_PER_GROUP = 4
ROWS = 256
HALO = 64
EXT = ROWS + 2 * HALO
TM_PROJ = 512
TM_POST = 256
TM_DISP = 256
TM_SLOT = 256
N_SLOT_TILES = N_TOK * TOP_K // TM_SLOT + N_EXP
N_SLOTS = N_SLOT_TILES * TM_SLOT
RANK_BITS = 14
META_NT = N_EXP
MOD_ROWS = 8
VMEM_LIMIT = 56 * 1024 * 1024


def _sigmoid(x):
    return 1.0 / (1.0 + jnp.exp(-x))


def _split2(a):
    hi = a.astype(BF16)
    lo = (a - hi.astype(F32)).astype(BF16)
    return hi, lo


_NN = (((1,), (0,)), ((), ()))
_NT = (((1,), (1,)), ((), ()))


def _dot(a, b, dims=_NN):
    return lax.dot_general(a, b, dims, preferred_element_type=F32)


def _dot3(a, b, dims=_NN):
    ah, al = _split2(a)
    bh, bl = _split2(b)
    return _dot(ah, bh, dims) + (_dot(ah, bl, dims) + _dot(al, bh, dims))


def _dot_exact_lhs(a_bf16, b, passes):
    acc = None
    rem = b
    for _ in range(passes):
        part = rem.astype(BF16)
        term = _dot(a_bf16, part)
        acc = term if acc is None else acc + term
        rem = rem - part.astype(F32)
    return acc


def _dot_exact_rhs(a, b_bf16, passes):
    acc = None
    rem = a
    for _ in range(passes):
        part = rem.astype(BF16)
        term = _dot(part, b_bf16)
        acc = term if acc is None else acc + term
        rem = rem - part.astype(F32)
    return acc


def _modnorm(x, w, scale, shift):
    ms = jnp.mean(x * x, axis=-1, keepdims=True)
    return x * lax.rsqrt(ms + RMS_EPS) * w * (1.0 + scale) + shift


def _mod_row(i, tm):
    n_ctx = N_CTX_TOK // tm
    per = T_LAT // tm
    return jnp.where(i < n_ctx, 0, 1 + (i - n_ctx) // per)


def _mod_kernel(c_ref, w_ref, b_ref, o_ref):
    c = c_ref[...]
    s = c * _sigmoid(c)
    o_ref[...] = _dot3(s, w_ref[...]) + b_ref[...]


def _modulation(cvec, w_mod, b_mod):
    return pl.pallas_call(
        _mod_kernel,
        out_shape=jax.ShapeDtypeStruct((MOD_ROWS, N_MOD * D), F32),
        grid=(N_MOD,),
        in_specs=[
            pl.BlockSpec((MOD_ROWS, D), lambda j: (0, 0)),
            pl.BlockSpec((D, D), lambda j: (0, j)),
            pl.BlockSpec((1, D), lambda j: (0, j)),
        ],
        out_specs=pl.BlockSpec((MOD_ROWS, D), lambda j: (0, j)),
        compiler_params=pltpu.CompilerParams(dimension_semantics=("arbitrary",), vmem_limit_bytes=VMEM_LIMIT),
        name="mod",
    )(cvec, w_mod, b_mod)


def _path_specs(tm, n_cols=D):
    n_ctx = N_CTX_TOK // tm
    return [pl.BlockSpec((tm, n_cols), lambda i: (jnp.minimum(i, n_ctx - 1), 0)),
            pl.BlockSpec((tm, n_cols), lambda i: (jnp.maximum(i - n_ctx, 0), 0))]


def _path_tile(c_ref, l_ref, tm):
    return jnp.where(pl.program_id(0) < N_CTX_TOK // tm, c_ref[...], l_ref[...])


def _proj_kernel(xc_ref, xl_ref, mod_ref, nw_ref, w_ref, o_ref):
    mod = mod_ref[0]
    hn = _modnorm(_path_tile(xc_ref, xl_ref, TM_PROJ), nw_ref[...], mod[:, D:2 * D], mod[:, 0:D])
    o_ref[...] = _dot(hn.astype(BF16), w_ref[...])


def _projection(x_ctx, x_lat, mod3, norm_w, w_a):
    n_cols = w_a.shape[1]
    return pl.pallas_call(
        _proj_kernel,
        out_shape=jax.ShapeDtypeStruct((N_TOK, n_cols), F32),
        grid=(N_TOK // TM_PROJ,),
        in_specs=_path_specs(TM_PROJ) + [
            pl.BlockSpec((1, 1, N_MOD * D), lambda i: (_mod_row(i, TM_PROJ), 0, 0)),
            pl.BlockSpec((1, D), lambda i: (0, 0)),
            pl.BlockSpec((D, n_cols), lambda i: (0, 0)),
        ],
        out_specs=pl.BlockSpec((TM_PROJ, n_cols), lambda i: (i, 0)),
        compiler_params=pltpu.CompilerParams(dimension_semantics=("arbitrary",), vmem_limit_bytes=VMEM_LIMIT),
        name="proj",
    )(x_ctx, x_lat, mod3, norm_w, w_a)


N_FRONT_IN = 14
N_FRONT_OUT = 12


def _front_kernel(grid_mode, n_chunks, seq_len, *refs):
    (cur_ref, prev_ref, next_ref, poolw_ref, pscale_ref, mu_ref,
     dw0_ref, dw2_ref, a0_ref, a2_ref, kk_ref, ka_ref, rk_ref, seg_ref) = refs[:N_FRONT_IN]
    (ypre_ref, gsig_ref, r_ref, v_ref, kkn_ref, bonus_ref,
     lwf_ref, lwb_ref, kdf_ref, kdb_ref, bf_ref, bb_ref) = refs[-N_FRONT_OUT:]
    c = pl.program_id(1)
    cur = cur_ref[...]
    prev = jnp.where(c > 0, prev_ref[...], 0.0)
    nxt = jnp.where(c < n_chunks - 1, next_ref[...], 0.0)
    ext = jnp.concatenate([prev, cur, nxt], axis=0)

    def down(x, s):
        return pltpu.roll(x, s, 0)

    def up(x, s):
        return pltpu.roll(x, EXT - s, 0)

    def mid(x):
        return x[HALO:HALO + ROWS]

    row = lax.broadcasted_iota(jnp.int32, (ROWS, 1), 0)
    t_seq = c * ROWS + row

    outs = []
    for gi, win in enumerate(POOL_WINDOWS):
        h = win // 2
        u = ext[:, gi * POOL_GD:(gi + 1) * POOL_GD]
        trail, lead, s = u, u, 1
        while s < h:
            trail = trail + down(trail, s)
            lead = lead + up(lead, s)
            s *= 2
        total = mid(down(trail, 1) + lead)
        cnt = (jnp.minimum(t_seq + h, seq_len) - jnp.maximum(t_seq - h, 0)).astype(F32)
        pooled = total / cnt - mid(u)
        outs.append(_dot(pooled.astype(BF16), poolw_ref[gi]))
    mixed = jnp.concatenate(outs, axis=1) * pscale_ref[...]
    ypre_ref[...] = mixed.astype(BF16)

    pe = ext[:, OFF_R:OFF_MERGE]
    p = mid(pe)
    lane = lax.broadcasted_iota(jnp.int32, (1, SHIFT_COLS), 1)
    if grid_mode:
        col = t_seq % GRID_W
        left = jnp.where(col > 0, mid(down(pe, 1)), 0.0)
        right = jnp.where(col < GRID_W - 1, mid(up(pe, 1)), 0.0)
        upn = pe[0:ROWS]
        dnn = pe[2 * HALO:2 * HALO + ROWS]
        q = lane % 4
        sh = jnp.where(q == 0, left, jnp.where(q == 1, right, jnp.where(q == 2, upn, dnn)))
    else:
        sh = jnp.where(lane % 2 == 0, mid(down(pe, 1)), mid(up(pe, 1)))
    pm = p + mu_ref[...] * (sh - p)

    r = pm[:, 0:RW]
    k = pm[:, RW:2 * RW]
    v = pm[:, 2 * RW:3 * RW]
    o = 3 * RW
    w_lo = jnp.tanh(pm[:, o:o + 2 * LORA])
    a_lo = pm[:, o + 2 * LORA:o + 4 * LORA]
    gsig_ref[...] = _sigmoid(pm[:, o + 4 * LORA:o + 4 * LORA + GATE_LORA]).astype(BF16)

    seg = seg_ref[...]
    kkr = k * kk_ref[...]
    ss = _dot_exact_rhs(kkr * kkr, seg, 2)
    kkn = kkr / jnp.maximum(jnp.sqrt(ss), 1e-12)
    r_ref[...] = r
    v_ref[...] = v
    kkn_ref[...] = kkn

    bonus = None
    for d, (lw_ref, kd_ref, b_ref) in enumerate(((lwf_ref, kdf_ref, bf_ref), (lwb_ref, kdb_ref, bb_ref))):
        z = dw0_ref[d:d + 1, :] + _dot3(w_lo, dw2_ref[d])
        lw_ref[...] = -math.exp(-0.5) * _sigmoid(z)
        a = _sigmoid(a0_ref[d:d + 1, :] + _dot3(a_lo, a2_ref[d]))
        kd = k * (1.0 + (a - 1.0) * ka_ref[...])
        kd_ref[...] = kd
        b_ref[...] = kkn * a
        bd = _dot_exact_rhs(r * kd * rk_ref[...], seg, 2) * v
        bonus = bd if bonus is None else bonus + bd
    bonus_ref[...] = bonus


def _front(proj, grid_mode, n_seq, seq_len, tok0, weights):
    n_chunks = seq_len // ROWS
    n_tok = n_seq * seq_len
    blk0 = tok0 // ROWS
    hpb = ROWS // HALO
    last_halo = N_TOK // HALO - 1

    def cur_map(b, c):
        return (blk0 + b * n_chunks + c, 0)

    def prev_map(b, c):
        return (jnp.maximum((blk0 + b * n_chunks + c) * hpb - 1, 0), 0)

    def next_map(b, c):
        return (jnp.minimum((blk0 + b * n_chunks + c + 1) * hpb, last_halo), 0)

    def full(a):
        nd = a.ndim
        return pl.BlockSpec(a.shape, lambda b, c, _nd=nd: (0,) * _nd)

    def out_map(b, c):
        return (b * n_chunks + c, 0)

    f32_out = jax.ShapeDtypeStruct((n_tok, RW), F32)
    out_shape = [jax.ShapeDtypeStruct((n_tok, POOL_W), BF16), jax.ShapeDtypeStruct((n_tok, GATE_LORA), BF16)]
    out_shape += [f32_out] * (N_FRONT_OUT - 2)
    out_specs = [pl.BlockSpec((ROWS, POOL_W), out_map), pl.BlockSpec((ROWS, GATE_LORA), out_map)]
    out_specs += [pl.BlockSpec((ROWS, RW), out_map)] * (N_FRONT_OUT - 2)
    assert len(weights) == N_FRONT_IN - 3
    return pl.pallas_call(
        functools.partial(_front_kernel, grid_mode, n_chunks, seq_len),
        out_shape=out_shape,
        grid=(n_seq, n_chunks),
        in_specs=[
            pl.BlockSpec((ROWS, OFF_MERGE), cur_map),
            pl.BlockSpec((HALO, OFF_MERGE), prev_map),
            pl.BlockSpec((HALO, OFF_MERGE), next_map),
        ] + [full(a) for a in weights],
        out_specs=out_specs,
        compiler_params=pltpu.CompilerParams(dimension_semantics=("arbitrary", "arbitrary"),
                                             vmem_limit_bytes=VMEM_LIMIT),
        name="front_grid" if grid_mode else "front_ctx",
    )(proj, proj, proj, *weights)


def _scan_kernel(n_chunks, has_init, *refs):
    (r_ref, v_ref, kk_ref, bonus_ref, lwf_ref, lwb_ref, kdf_ref, kdb_ref, bf_ref, bb_ref) = refs[:10]
    if has_init:
        s0f_ref, s0b_ref = refs[10:12]
    y_ref, sf_ref, sb_ref = refs[-3:]
    C = CHUNK
    P2 = 2 * C
    ri = lax.broadcasted_iota(jnp.int32, (P2, P2), 0)
    ci = lax.broadcasted_iota(jnp.int32, (P2, P2), 1)
    same = (ri // C) == (ci // C)
    rp, cp = ri % C, ci % C
    eye = (ri == ci).astype(F32)
    ti = lax.broadcasted_iota(jnp.int32, (C, C), 0)
    tj = lax.broadcasted_iota(jnp.int32, (C, C), 1)
    head_a = lax.broadcasted_iota(jnp.int32, (C, LANES), 1) < HEAD

    def stack(x):
        return jnp.concatenate([jnp.where(head_a, x, 0.0), jnp.where(head_a, 0.0, x)], axis=0)

    def d3(a, b, dims=_NN):
        return _dot(a[0], b[0], dims) + (_dot(a[0], b[1], dims) + _dot(a[1], b[0], dims))

    def chunks(chains):
        n = range(len(chains))
        masks, sls = [], []
        for (s_ref, p, row0, lw_ref, kd_ref, b_ref, reverse) in chains:
            if reverse:
                masks.append((same & (cp > rp), same & (cp >= rp), (tj >= ti), 0))
            else:
                masks.append((same & (cp < rp), same & (cp <= rp), (tj <= ti), C - 1))
            sls.append((pl.ds(row0, C), slice(p * LANES, (p + 1) * LANES)))
        lw = [chains[i][3][sls[i]] for i in n]
        cum = [_dot_exact_lhs(masks[i][2].astype(BF16), lw[i], 3) for i in n]
        e_i = [jnp.exp(cum[i]) for i in n]
        e_n = [jnp.exp(-cum[i]) for i in n]
        QR = [_split2(jnp.concatenate([stack(kk_ref[sls[i]] * jnp.exp(cum[i] - lw[i])),
                                       stack(r_ref[sls[i]] * e_i[i])], axis=0)) for i in n]
        BK = [_split2(jnp.concatenate([stack(chains[i][5][sls[i]] * e_n[i]),
                                       stack(chains[i][4][sls[i]] * e_n[i])], axis=0)) for i in n]
        v_st = [stack(v_ref[sls[i]]) for i in n]
        Vs = [_split2(v_st[i]) for i in n]
        S = [chains[i][0][0, chains[i][1]] for i in n]
        G = [d3(QR[i], BK[i], _NT) for i in n]
        L = [jnp.where(masks[i][0], G[i][0:P2, 0:P2], 0.0) for i in n]
        Lk = [jnp.where(masks[i][0], G[i][0:P2, P2:2 * P2], 0.0) for i in n]
        RBK = [jnp.concatenate([jnp.where(masks[i][1], G[i][P2:2 * P2, 0:P2], 0.0),
                                jnp.where(masks[i][1], G[i][P2:2 * P2, P2:2 * P2], 0.0)], axis=1) for i in n]
        X = [eye - L[i] for i in n]
        Pw = [_split2(-L[i]) for i in n]
        for _ in range(int(math.log2(C)) - 1):
            Pw = [_split2(d3(Pw[i], Pw[i])) for i in n]
            X = [X[i] + d3(_split2(X[i]), Pw[i]) for i in n]
        QRA = [d3(QR[i], _split2(S[i]), _NT) for i in n]
        W = [QRA[i][0:P2] + d3(_split2(Lk[i]), Vs[i]) for i in n]
        U = [-d3(_split2(X[i]), _split2(W[i])) for i in n]
        UV = [jnp.concatenate([U[i], v_st[i]], axis=0) for i in n]
        Ys = [QRA[i][P2:2 * P2] + d3(_split2(RBK[i]), _split2(UV[i])) for i in n]
        dS = [d3(_split2(UV[i].T), BK[i]) for i in n]
        for i in n:
            last = masks[i][3]
            chains[i][0][0, chains[i][1]] = (S[i] + dS[i]) * e_i[i][last:last + 1, :]
            y_ref[sls[i]] += Ys[i][0:C] + Ys[i][C:P2]

    y_ref[...] = bonus_ref[...]
    if has_init:
        sf_ref[...] = s0f_ref[...]
        sb_ref[...] = s0b_ref[...]
    else:
        sf_ref[...] = jnp.zeros_like(sf_ref)
        sb_ref[...] = jnp.zeros_like(sb_ref)

    def body(c, carry):
        rf = pl.multiple_of(c * C, C)
        rb = pl.multiple_of((n_chunks - 1 - c) * C, C)
        for p0 in range(0, N_PAIR, PAIRS_PER_GROUP):
            chains = []
            for p in range(p0, p0 + PAIRS_PER_GROUP):
                chains.append((sf_ref, p, rf, lwf_ref, kdf_ref, bf_ref, False))
                chains.append((sb_ref, p, rb, lwb_ref, kdb_ref, bb_ref, True))
            chunks(chains)
        return carry

    lax.fori_loop(0, n_chunks, body, 0)


def _scan(arrs, n_seq, seq_len, init):
    n_tok = n_seq * seq_len
    n_chunks = seq_len // CHUNK
    tok_spec = pl.BlockSpec((seq_len, RW), lambda b: (b, 0))
    st_spec = pl.BlockSpec((1, N_PAIR, LANES, LANES), lambda b: (b, 0, 0, 0))
    st_shape = jax.ShapeDtypeStruct((n_seq, N_PAIR, LANES, LANES), F32)
    ins = list(arrs)
    in_specs = [tok_spec] * 10
    if init is not None:
        ins += list(init)
        in_specs += [st_spec, st_spec]
    return pl.pallas_call(
        functools.partial(_scan_kernel, n_chunks, init is not None),
        out_shape=[jax.ShapeDtypeStruct((n_tok, RW), F32), st_shape, st_shape],
        grid=(n_seq,),
        in_specs=in_specs,
        out_specs=[tok_spec, st_spec, st_spec],
        compiler_params=pltpu.CompilerParams(dimension_semantics=("arbitrary",), vmem_limit_bytes=VMEM_LIMIT),
        name="scan_init" if init is not None else "scan_zero",
    )(*ins)


def _post_kernel(xc_ref, xl_ref, mod_ref, yprec_ref, yprel_ref, gsigc_ref, gsigl_ref, yc_ref, yl_ref,
                 nmw_ref, wm_ref, bm_ref, wpo_ref, seg_ref, lnw_ref, lnb_ref, gw2_ref, wro_ref, wo_ref,
                 nfw_ref, rw_ref, rb_ref,
                 x1_ref, hn2_ref, code_ref, w4_ref, meta_ref, texp_ref, carry_ref):
    i = pl.program_id(0)

    @pl.when(i == 0)
    def _():
        carry_ref[...] = jnp.zeros_like(carry_ref)

    x = _path_tile(xc_ref, xl_ref, TM_POST)
    mod = mod_ref[0]
    shift1, scale1, gate1 = mod[:, 0:D], mod[:, D:2 * D], mod[:, 2 * D:3 * D]
    shift2, scale2 = mod[:, 3 * D:4 * D], mod[:, 4 * D:5 * D]
    hn = _modnorm(x, nmw_ref[...], scale1, shift1)
    merge = _sigmoid(_dot(hn.astype(BF16), wm_ref[...]) + bm_ref[...])
    y_pool = _dot(_path_tile(yprec_ref, yprel_ref, TM_POST), wpo_ref[...])

    y = _path_tile(yc_ref, yl_ref, TM_POST)
    seg = seg_ref[...]
    mu = _dot_exact_rhs(y, seg, 2) * (1.0 / HEAD)
    yc = y - mu
    var = _dot_exact_rhs(yc * yc, seg, 2) * (1.0 / HEAD)
    yn = yc * lax.rsqrt(var + GN_EPS) * lnw_ref[...] + lnb_ref[...]
    g = _dot(_path_tile(gsigc_ref, gsigl_ref, TM_POST), gw2_ref[...])
    y_rwkv = _dot((yn * g).astype(BF16), wro_ref[...])

    mixed = merge[:, 0:D] * y_pool + merge[:, D:2 * D] * y_rwkv
    x1 = x + gate1 * _dot(mixed.astype(BF16), wo_ref[...])
    x1_ref[...] = x1
    hn2 = _modnorm(x1, nfw_ref[...], scale2, shift2)
    hn2_ref[...] = hn2

    logits = _dot3(hn2, rw_ref[...]) + rb_ref[...]
    lane = lax.broadcasted_iota(jnp.int32, logits.shape, 1)
    work = logits
    sel = None
    top = None
    for j in range(TOP_K):
        m = jnp.max(work, axis=-1, keepdims=True)
        if j == 0:
            top = m
        idx = jnp.min(jnp.where(work == m, lane, LANES), axis=-1, keepdims=True)
        pick = lane == idx
        sel = pick if sel is None else (sel | pick)
        work = jnp.where(pick, -jnp.inf, work)
    e = jnp.where(sel, jnp.exp(logits - top), 0.0)
    comb = e / jnp.sum(e, axis=-1, keepdims=True)

    tm = logits.shape[0]
    sel_b = jnp.where(sel, 1.0, 0.0).astype(BF16)
    before = (lax.broadcasted_iota(jnp.int32, (tm, tm), 1) < lax.broadcasted_iota(jnp.int32, (tm, tm), 0))
    rank = carry_ref[...] + _dot(before.astype(BF16), sel_b)
    carry = carry_ref[...] + jnp.sum(sel_b.astype(F32), axis=0, keepdims=True)
    carry_ref[...] = carry
    lower_e = (lax.broadcasted_iota(jnp.int32, (LANES, LANES), 0) < lax.broadcasted_iota(jnp.int32, (LANES, LANES), 1))
    rowpos = _dot(sel_b, lower_e.astype(BF16))
    code = jnp.zeros(logits.shape, jnp.int32)
    w4 = jnp.zeros(logits.shape, F32)
    for k in range(TOP_K):
        mk = sel & (rowpos == float(k))
        ek = jnp.sum(jnp.where(mk, lane, 0), axis=-1, keepdims=True)
        rk = jnp.sum(jnp.where(mk, rank, 0.0), axis=-1, keepdims=True).astype(jnp.int32)
        wk = jnp.sum(jnp.where(mk, comb, 0.0), axis=-1, keepdims=True)
        code = jnp.where(lane == k, ek * (1 << RANK_BITS) + rk, code)
        w4 = jnp.where(lane == k, wk, w4)
    code_ref[...] = code
    w4_ref[...] = w4

    tiles = jnp.floor((carry + (TM_SLOT - 1)) * (1.0 / TM_SLOT))
    incl = (lax.broadcasted_iota(jnp.int32, (LANES, LANES), 0) <= lax.broadcasted_iota(jnp.int32, (LANES, LANES), 1))
    tiles8 = jnp.broadcast_to(tiles, (8, LANES)).astype(BF16)
    cum = _dot(tiles8, incl.astype(BF16))[0:1]
    lane1 = lax.broadcasted_iota(jnp.int32, (1, LANES), 1)
    offs = ((cum - tiles) * TM_SLOT).astype(jnp.int32)
    n_used = jnp.sum(jnp.where(lane1 == N_EXP - 1, cum, 0.0), axis=-1, keepdims=True).astype(jnp.int32)
    meta_ref[...] = jnp.where(lane1 == META_NT, n_used, jnp.where(lane1 < N_EXP, offs, 0))
    tile_id = lax.broadcasted_iota(jnp.int32, (N_SLOT_TILES, LANES), 0).astype(F32)
    done = jnp.where((lax.broadcasted_iota(jnp.int32, (N_SLOT_TILES, LANES), 1) < N_EXP) & (cum <= tile_id), 1, 0)
    texp = jnp.minimum(jnp.sum(done, axis=-1, keepdims=True), N_EXP - 1)
    texp_ref[...] = jnp.broadcast_to(texp, (N_SLOT_TILES, LANES))


def _post(x, mod3, ypre, gsig, y, weights):
    def tile(n, dt):
        return pl.BlockSpec((TM_POST, n), lambda i: (i, 0))

    def full(a):
        nd = a.ndim
        return pl.BlockSpec(a.shape, lambda i, _nd=nd: (0,) * _nd)

    def const(shape):
        return pl.BlockSpec(shape, lambda i: (0, 0))

    return pl.pallas_call(
        _post_kernel,
        out_shape=[jax.ShapeDtypeStruct((N_TOK, D), F32), jax.ShapeDtypeStruct((N_TOK, D), F32),
                   jax.ShapeDtypeStruct((N_TOK, LANES), jnp.int32), jax.ShapeDtypeStruct((N_TOK, LANES), F32),
                   jax.ShapeDtypeStruct((1, LANES), jnp.int32),
                   jax.ShapeDtypeStruct((N_SLOT_TILES, LANES), jnp.int32)],
        grid=(N_TOK // TM_POST,),
        in_specs=_path_specs(TM_POST) + [
            pl.BlockSpec((1, 1, N_MOD * D), lambda i: (_mod_row(i, TM_POST), 0, 0)),
        ] + _path_specs(TM_POST, POOL_W) + _path_specs(TM_POST, GATE_LORA) + _path_specs(TM_POST, RW)
        + [full(a) for a in weights],
        out_specs=[tile(D, F32), tile(D, F32), tile(LANES, jnp.int32), tile(LANES, F32),
                   const((1, LANES)), const((N_SLOT_TILES, LANES))],
        scratch_shapes=[pltpu.VMEM((1, LANES), F32)],
        compiler_params=pltpu.CompilerParams(dimension_semantics=("arbitrary",), vmem_limit_bytes=VMEM_LIMIT),
        name="post",
    )(*x, mod3, *ypre, *gsig, *y, *weights)


def _slot_of(code, meta_ref):
    return meta_ref[code >> RANK_BITS] + (code & ((1 << RANK_BITS) - 1))


def _load_codes(code_ref, code_smem, sem):
    cp = pltpu.make_async_copy(code_ref.at[pl.program_id(0)], code_smem, sem)
    cp.start()
    cp.wait()


def _dispatch_kernel(code_ref, meta_ref, hn2_ref, xs_ref, code_smem, zero_buf, sem_c, sem_z, sem):
    @pl.when(pl.program_id(0) == 0)
    def _():
        zero_buf[...] = jnp.zeros_like(zero_buf)

        def pad_tile(e):
            end = meta_ref[e + 1] if e + 1 < N_EXP else meta_ref[META_NT] * TM_SLOT
            start = pl.multiple_of(end - TM_SLOT, TM_SLOT)
            return end > meta_ref[e], pltpu.make_async_copy(zero_buf, xs_ref.at[pl.ds(start, TM_SLOT)], sem_z)

        def tail_tile(j):
            return pltpu.make_async_copy(zero_buf, xs_ref.at[pl.ds(pl.multiple_of(j * TM_SLOT, TM_SLOT), TM_SLOT)], sem_z)

        n_used = meta_ref[META_NT]
        for e in range(N_EXP):
            nonempty, cp = pad_tile(e)
            pl.when(nonempty)(cp.start)

        def start_tail(j, c):
            tail_tile(j).start()
            return c

        def wait_tail(j, c):
            tail_tile(j).wait()
            return c

        lax.fori_loop(n_used, N_SLOT_TILES, start_tail, 0)
        for e in range(N_EXP):
            nonempty, cp = pad_tile(e)
            pl.when(nonempty)(cp.wait)
        lax.fori_loop(n_used, N_SLOT_TILES, wait_tail, 0)

    _load_codes(code_ref, code_smem, sem_c)

    def body(t, carry):
        for k in range(TOP_K):
            slot = _slot_of(code_smem[t * TOP_K + k], meta_ref)
            pltpu.make_async_copy(hn2_ref.at[pl.ds(t, 1)], xs_ref.at[pl.ds(slot, 1)], sem).start(priority=k % 2)
        return carry

    lax.fori_loop(0, TM_DISP, body, 0, unroll=8)
    for _ in range(TOP_K):
        pltpu.make_async_copy(hn2_ref, xs_ref.at[pl.ds(0, TM_DISP)], sem).wait()


def _dispatch(code2d, meta, hn2):
    n_steps = N_TOK // TM_DISP
    return pl.pallas_call(
        _dispatch_kernel,
        out_shape=jax.ShapeDtypeStruct((N_SLOTS, D), F32),
        grid=(n_steps,),
        in_specs=[
            pl.BlockSpec(code2d.shape, lambda i: (0, 0)),
            pl.BlockSpec(memory_space=pltpu.SMEM),
            pl.BlockSpec((TM_DISP, D), lambda i: (i, 0)),
        ],
        out_specs=pl.BlockSpec(memory_space=pl.ANY),
        scratch_shapes=[pltpu.SMEM((TM_DISP * TOP_K,), jnp.int32), pltpu.VMEM((TM_SLOT, D), F32),
                        pltpu.SemaphoreType.DMA, pltpu.SemaphoreType.DMA, pltpu.SemaphoreType.DMA],
        compiler_params=pltpu.CompilerParams(dimension_semantics=("arbitrary",), vmem_limit_bytes=VMEM_LIMIT),
        name="dispatch",
    )(code2d, meta, hn2)


def _ffn_kernel(texp_ref, meta_ref, xs_ref, wgu_ref, bgu_ref, wd_ref, bd_ref, ys_ref, wgu_bf, wd_bf):
    i = pl.program_id(0)
    valid = i < meta_ref[META_NT]
    fresh = (i == 0) | (texp_ref[i] != texp_ref[jnp.maximum(i - 1, 0)])

    @pl.when(valid & fresh)
    def _():
        wgu_bf[...] = wgu_ref[0].astype(BF16)
        wd_bf[...] = wd_ref[0].astype(BF16)

    @pl.when(valid)
    def _():
        gu = _dot(xs_ref[...].astype(BF16), wgu_bf[...]) + bgu_ref[0]
        glu = jnp.minimum(gu[:, 0:D_FF], SWIGLU_LIMIT)
        lin = jnp.clip(gu[:, D_FF:2 * D_FF], -SWIGLU_LIMIT, SWIGLU_LIMIT)
        act = glu * _sigmoid(SWIGLU_ALPHA * glu) * (lin + 1.0)
        ys_ref[...] = _dot(act.astype(BF16), wd_bf[...]) + bd_ref[0]

    @pl.when(jnp.logical_not(valid))
    def _():
        ys_ref[...] = jnp.zeros_like(ys_ref)


def _ffn(texp, meta, xs, wgu, bgu, wd, bd):
    def used_tile(i, te, me):
        return (jnp.minimum(i, me[META_NT] - 1), 0)

    return pl.pallas_call(
        _ffn_kernel,
        out_shape=jax.ShapeDtypeStruct((N_SLOTS, D), F32),
        grid_spec=pltpu.PrefetchScalarGridSpec(
            num_scalar_prefetch=2,
            grid=(N_SLOT_TILES,),
            in_specs=[
                pl.BlockSpec((TM_SLOT, D), used_tile),
                pl.BlockSpec((1, D, 2 * D_FF), lambda i, te, me: (te[i], 0, 0)),
                pl.BlockSpec((1, 1, 2 * D_FF), lambda i, te, me: (te[i], 0, 0)),
                pl.BlockSpec((1, D_FF, D), lambda i, te, me: (te[i], 0, 0)),
                pl.BlockSpec((1, 1, D), lambda i, te, me: (te[i], 0, 0)),
            ],
            out_specs=pl.BlockSpec((TM_SLOT, D), lambda i, te, me: (i, 0)),
            scratch_shapes=[pltpu.VMEM((D, 2 * D_FF), BF16), pltpu.VMEM((D_FF, D), BF16)],
        ),
        compiler_params=pltpu.CompilerParams(dimension_semantics=("arbitrary",), vmem_limit_bytes=VMEM_LIMIT),
        name="ffn",
    )(texp, meta, xs, wgu, bgu, wd, bd)


def _combine_kernel(code_ref, meta_ref, x1_ref, w4_ref, mod_ref, nfw_ref, ys_ref, oc_ref, ol_ref,
                    code_smem, buf, sem_c, sem):
    _load_codes(code_ref, code_smem, sem_c)

    def body(t, carry):
        for k in range(TOP_K):
            slot = _slot_of(code_smem[t * TOP_K + k], meta_ref)
            pltpu.make_async_copy(ys_ref.at[pl.ds(slot, 1)], buf.at[k, pl.ds(t, 1)], sem).start(priority=k % 2)
        return carry

    lax.fori_loop(0, TM_DISP, body, 0, unroll=8)
    for k in range(TOP_K):
        pltpu.make_async_copy(ys_ref.at[pl.ds(0, TM_DISP)], buf.at[k], sem).wait()
    w4 = w4_ref[...]
    moe = w4[:, 0:1] * buf[0]
    for k in range(1, TOP_K):
        moe = moe + w4[:, k:k + 1] * buf[k]
    gate2 = mod_ref[0][:, 5 * D:6 * D]
    x2 = x1_ref[...] + gate2 * moe
    ms = jnp.mean(x2 * x2, axis=-1, keepdims=True)
    out = x2 * lax.rsqrt(ms + RMS_EPS) * nfw_ref[...]
    is_ctx = pl.program_id(0) < N_CTX_TOK // TM_DISP

    @pl.when(is_ctx)
    def _():
        oc_ref[...] = out

    @pl.when(jnp.logical_not(is_ctx))
    def _():
        ol_ref[...] = out


def _combine(code2d, meta, x1, w4, mod3, nfw, ys):
    n_steps = N_TOK // TM_DISP
    n_ctx = N_CTX_TOK // TM_DISP
    return pl.pallas_call(
        _combine_kernel,
        out_shape=[jax.ShapeDtypeStruct((N_CTX_TOK, D), F32), jax.ShapeDtypeStruct((N_LAT_TOK, D), F32)],
        grid=(n_steps,),
        in_specs=[
            pl.BlockSpec(code2d.shape, lambda i: (0, 0)),
            pl.BlockSpec(memory_space=pltpu.SMEM),
            pl.BlockSpec((TM_DISP, D), lambda i: (i, 0)),
            pl.BlockSpec((TM_DISP, LANES), lambda i: (i, 0)),
            pl.BlockSpec((1, 1, N_MOD * D), lambda i: (_mod_row(i, TM_DISP), 0, 0)),
            pl.BlockSpec((1, D), lambda i: (0, 0)),
            pl.BlockSpec(memory_space=pl.ANY),
        ],
        out_specs=[pl.BlockSpec((TM_DISP, D), lambda i: (jnp.minimum(i, n_ctx - 1), 0)),
                   pl.BlockSpec((TM_DISP, D), lambda i: (jnp.maximum(i - n_ctx, 0), 0))],
        scratch_shapes=[pltpu.SMEM((TM_DISP * TOP_K,), jnp.int32), pltpu.VMEM((TOP_K, TM_DISP, D), F32),
                        pltpu.SemaphoreType.DMA, pltpu.SemaphoreType.DMA],
        compiler_params=pltpu.CompilerParams(dimension_semantics=("arbitrary",), vmem_limit_bytes=VMEM_LIMIT),
        name="combine",
    )(code2d, meta, x1, w4, mod3, nfw, ys)


def _pair_blockdiag(s):
    b = s.shape[0]
    s = s.reshape(b, N_PAIR, 2, HEAD, HEAD)
    z = jnp.zeros((b, N_PAIR, HEAD, HEAD), s.dtype)
    top = jnp.concatenate([s[:, :, 0], z], axis=-1)
    bot = jnp.concatenate([z, s[:, :, 1]], axis=-1)
    return jnp.concatenate([top, bot], axis=-2)


def _pair_unblock(s):
    b = s.shape[0]
    a = s[:, :, 0:HEAD, 0:HEAD]
    c = s[:, :, HEAD:, HEAD:]
    return jnp.stack([a, c], axis=2).reshape(b, 1, N_HEAD, HEAD, HEAD)


def kernel(x_prompt, x_sample, state_fwd, state_bwd, c, c_ctx, w_mod, b_mod, norm_mix_w, w_in, b_merge, pool_w, pool_scale, w_pool_out, shift_mu, decay_w0, decay_w2, iclr_a0, iclr_a2, gate_w2, k_k, k_a, r_k, ln_x_w, ln_x_b, w_rwkv_out, w_o, norm_ffn_w, router_w, router_b, expert_w_gu, expert_b_gu, expert_w_down, expert_b_down, norm_final_w):
    l = 0
    x_ctx = x_prompt.reshape(N_CTX_TOK, D)
    x_lat = x_sample.reshape(N_LAT_TOK, D)
    cvec = jnp.concatenate([c_ctx[None, :], c, jnp.zeros((MOD_ROWS - 1 - N_LAT_SEQ, D), F32)], axis=0)
    mod = _modulation(cvec, w_mod[l], b_mod[l][None, :])
    mod3 = mod.reshape(MOD_ROWS, 1, N_MOD * D)

    w_in_l = w_in[l]
    proj = _projection(x_ctx, x_lat, mod3, norm_mix_w[l][None, :], w_in_l[:, :OFF_MERGE].astype(BF16))

    zl = jnp.zeros((LORA, RW), F32)
    dw2 = jnp.stack([jnp.concatenate([decay_w2[l, 0], zl], 0), jnp.concatenate([zl, decay_w2[l, 1]], 0)])
    a2 = jnp.stack([jnp.concatenate([iclr_a2[l, 0], zl], 0), jnp.concatenate([zl, iclr_a2[l, 1]], 0)])
    hid = jnp.arange(RW) // HEAD
    seg = (hid[:, None] == hid[None, :]).astype(BF16)
    front_w = (pool_w[l].astype(BF16), pool_scale[l][None, :], shift_mu[l][None, :],
               decay_w0[l], dw2, iclr_a0[l], a2, k_k[l][None, :], k_a[l][None, :],
               r_k[l].reshape(1, RW), seg)
    f_ctx = _front(proj, False, N_CTX_SEQ, T_CTX, 0, front_w)
    f_lat = _front(proj, True, N_LAT_SEQ, T_LAT, N_CTX_TOK, front_w)

    y_ctx, sf, sb = _scan(f_ctx[2:], N_CTX_SEQ, T_CTX, None)
    init = (_pair_blockdiag(state_fwd[:, l]), _pair_blockdiag(state_bwd[:, l]))
    y_lat, _, _ = _scan(f_lat[2:], N_LAT_SEQ, T_LAT, init)

    rw_pad = jnp.concatenate([router_w[l], jnp.zeros((D, LANES - N_EXP), F32)], axis=1)
    rb_pad = jnp.concatenate([router_b[l], jnp.full((LANES - N_EXP,), -1e30, F32)])[None, :]
    post_w = (norm_mix_w[l][None, :], w_in_l[:, OFF_MERGE:].astype(BF16), b_merge[l][None, :],
              w_pool_out[l].astype(BF16), seg, ln_x_w[l][None, :], ln_x_b[l][None, :],
              gate_w2[l].astype(BF16), w_rwkv_out[l].astype(BF16), w_o[l].astype(BF16),
              norm_ffn_w[l][None, :], rw_pad, rb_pad)
    x1, hn2, code, w4, meta, texp = _post((x_ctx, x_lat), mod3, (f_ctx[0], f_lat[0]), (f_ctx[1], f_lat[1]),
                                          (y_ctx, y_lat), post_w)

    code2d = code[:, :TOP_K].reshape(N_TOK // TM_DISP, TM_DISP * TOP_K)
    meta1 = meta.reshape(LANES)
    xs = _dispatch(code2d, meta1, hn2)
    ys = _ffn(texp[:, 0], meta1, xs, expert_w_gu[l], expert_b_gu[l][:, None, :],
              expert_w_down[l], expert_b_down[l][:, None, :])
    out_ctx, out_lat = _combine(code2d, meta1, x1, w4, mod3, norm_final_w[None, :], ys)
    y_prompt = out_ctx.reshape(N_CTX_SEQ, T_CTX, D)
    y_sample = out_lat.reshape(N_LAT_SEQ, T_LAT, D)
    return (y_prompt, y_sample, _pair_unblock(sf), _pair_unblock(sb))
```

```python
import functools
import math

import jax
import jax.numpy as jnp
from jax import lax
from jax.experimental import pallas as pl
from jax.experimental.pallas import tpu as pltpu

F32 = jnp.float32
BF16 = jnp.bfloat16

D = 1024
N_CTX_SEQ, T_CTX = 32, 256
N_LAT_SEQ, T_LAT = 4, 1024
GRID_W = 64
N_CTX_TOK = N_CTX_SEQ * T_CTX
N_LAT_TOK = N_LAT_SEQ * T_LAT
N_TOK = N_CTX_TOK + N_LAT_TOK

POOL_W = 512
POOL_G = 4
POOL_GD = POOL_W // POOL_G
POOL_WINDOWS = (2, 4, 8, 16)
RW = 512
HEAD = 64
N_HEAD = RW // HEAD
N_PAIR = N_HEAD // 2
LORA = 64
GATE_LORA = 128
N_EXP = 32
TOP_K = 4
D_FF = 1024
SWIGLU_ALPHA = 1.702
SWIGLU_LIMIT = 7.0
RMS_EPS = 1e-6
GN_EPS = 1e-5 * HEAD
N_MOD = 6

OFF_R = POOL_W
SHIFT_COLS = 3 * RW + 4 * LORA + GATE_LORA
OFF_MERGE = OFF_R + SHIFT_COLS
IN_COLS = OFF_MERGE + 2 * D

LANES = 128
CHUNK = 64
INV_BLOCK = 8
PAIRS_PER_GROUP = 4
ROWS = 256
HALO = 64
EXT = ROWS + 2 * HALO
TM_PROJ = 512
TM_POST = 256
TM_DISP = 256
TM_SLOT = 256
N_SLOT_TILES = N_TOK * TOP_K // TM_SLOT + N_EXP
N_SLOTS = N_SLOT_TILES * TM_SLOT
RANK_BITS = 14
META_NT = N_EXP
MOD_ROWS = 8
VMEM_LIMIT = 56 * 1024 * 1024


def _sigmoid(x):
    return 1.0 / (1.0 + jnp.exp(-x))


def _split2(a):
    hi = a.astype(BF16)
    lo = (a - hi.astype(F32)).astype(BF16)
    return hi, lo


_NN = (((1,), (0,)), ((), ()))
_NT = (((1,), (1,)), ((), ()))


def _dot(a, b, dims=_NN):
    return lax.dot_general(a, b, dims, preferred_element_type=F32)


def _dot3(a, b, dims=_NN):
    ah, al = _split2(a)
    bh, bl = _split2(b)
    return _dot(ah, bh, dims) + (_dot(ah, bl, dims) + _dot(al, bh, dims))


def _dot_exact_lhs(a_bf16, b, passes):
    acc = None
    rem = b
    for _ in range(passes):
        part = rem.astype(BF16)
        term = _dot(a_bf16, part)
        acc = term if acc is None else acc + term
        rem = rem - part.astype(F32)
    return acc


def _dot_exact_rhs(a, b_bf16, passes):
    acc = None
    rem = a
    for _ in range(passes):
        part = rem.astype(BF16)
        term = _dot(part, b_bf16)
        acc = term if acc is None else acc + term
        rem = rem - part.astype(F32)
    return acc


def _modnorm(x, w, scale, shift):
    ms = jnp.mean(x * x, axis=-1, keepdims=True)
    return x * lax.rsqrt(ms + RMS_EPS) * w * (1.0 + scale) + shift


def _mod_row(i, tm):
    n_ctx = N_CTX_TOK // tm
    per = T_LAT // tm
    return jnp.where(i < n_ctx, 0, 1 + (i - n_ctx) // per)


def _mod_kernel(c_ref, w_ref, b_ref, o_ref):
    c = c_ref[...]
    s = c * _sigmoid(c)
    o_ref[...] = _dot3(s, w_ref[...]) + b_ref[...]


def _modulation(cvec, w_mod, b_mod):
    return pl.pallas_call(
        _mod_kernel,
        out_shape=jax.ShapeDtypeStruct((MOD_ROWS, N_MOD * D), F32),
        grid=(N_MOD,),
        in_specs=[
            pl.BlockSpec((MOD_ROWS, D), lambda j: (0, 0)),
            pl.BlockSpec((D, D), lambda j: (0, j)),
            pl.BlockSpec((1, D), lambda j: (0, j)),
        ],
        out_specs=pl.BlockSpec((MOD_ROWS, D), lambda j: (0, j)),
        compiler_params=pltpu.CompilerParams(dimension_semantics=("arbitrary",), vmem_limit_bytes=VMEM_LIMIT),
        name="mod",
    )(cvec, w_mod, b_mod)


def _path_specs(tm, n_cols=D):
    n_ctx = N_CTX_TOK // tm
    return [pl.BlockSpec((tm, n_cols), lambda i: (jnp.minimum(i, n_ctx - 1), 0)),
            pl.BlockSpec((tm, n_cols), lambda i: (jnp.maximum(i - n_ctx, 0), 0))]


def _path_tile(c_ref, l_ref, tm):
    return jnp.where(pl.program_id(0) < N_CTX_TOK // tm, c_ref[...], l_ref[...])


def _proj_kernel(xc_ref, xl_ref, mod_ref, nw_ref, w_ref, o_ref):
    mod = mod_ref[0]
    hn = _modnorm(_path_tile(xc_ref, xl_ref, TM_PROJ), nw_ref[...], mod[:, D:2 * D], mod[:, 0:D])
    o_ref[...] = _dot(hn.astype(BF16), w_ref[...])


def _projection(x_ctx, x_lat, mod3, norm_w, w_a):
    n_cols = w_a.shape[1]
    return pl.pallas_call(
        _proj_kernel,
        out_shape=jax.ShapeDtypeStruct((N_TOK, n_cols), F32),
        grid=(N_TOK // TM_PROJ,),
        in_specs=_path_specs(TM_PROJ) + [
            pl.BlockSpec((1, 1, N_MOD * D), lambda i: (_mod_row(i, TM_PROJ), 0, 0)),
            pl.BlockSpec((1, D), lambda i: (0, 0)),
            pl.BlockSpec((D, n_cols), lambda i: (0, 0)),
        ],
        out_specs=pl.BlockSpec((TM_PROJ, n_cols), lambda i: (i, 0)),
        compiler_params=pltpu.CompilerParams(dimension_semantics=("arbitrary",), vmem_limit_bytes=VMEM_LIMIT),
        name="proj",
    )(x_ctx, x_lat, mod3, norm_w, w_a)


N_FRONT_IN = 14
N_FRONT_OUT = 12


def _front_kernel(grid_mode, n_chunks, seq_len, *refs):
    (cur_ref, prev_ref, next_ref, poolw_ref, pscale_ref, mu_ref,
     dw0_ref, dw2_ref, a0_ref, a2_ref, kk_ref, ka_ref, rk_ref, seg_ref) = refs[:N_FRONT_IN]
    (ypre_ref, gsig_ref, r_ref, v_ref, kkn_ref, bonus_ref,
     lwf_ref, lwb_ref, kdf_ref, kdb_ref, bf_ref, bb_ref) = refs[-N_FRONT_OUT:]
    c = pl.program_id(1)
    cur = cur_ref[...]
    prev = jnp.where(c > 0, prev_ref[...], 0.0)
    nxt = jnp.where(c < n_chunks - 1, next_ref[...], 0.0)
    ext = jnp.concatenate([prev, cur, nxt], axis=0)

    def down(x, s):
        return pltpu.roll(x, s, 0)

    def up(x, s):
        return pltpu.roll(x, EXT - s, 0)

    def mid(x):
        return x[HALO:HALO + ROWS]

    row = lax.broadcasted_iota(jnp.int32, (ROWS, 1), 0)
    t_seq = c * ROWS + row

    outs = []
    for gi, win in enumerate(POOL_WINDOWS):
        h = win // 2
        u = ext[:, gi * POOL_GD:(gi + 1) * POOL_GD]
        trail, lead, s = u, u, 1
        while s < h:
            trail = trail + down(trail, s)
            lead = lead + up(lead, s)
            s *= 2
        total = mid(down(trail, 1) + lead)
        cnt = (jnp.minimum(t_seq + h, seq_len) - jnp.maximum(t_seq - h, 0)).astype(F32)
        pooled = total / cnt - mid(u)
        outs.append(_dot(pooled.astype(BF16), poolw_ref[gi]))
    mixed = jnp.concatenate(outs, axis=1) * pscale_ref[...]
    ypre_ref[...] = mixed.astype(BF16)

    pe = ext[:, OFF_R:OFF_MERGE]
    p = mid(pe)
    lane = lax.broadcasted_iota(jnp.int32, (1, SHIFT_COLS), 1)
    if grid_mode:
        col = t_seq % GRID_W
        left = jnp.where(col > 0, mid(down(pe, 1)), 0.0)
        right = jnp.where(col < GRID_W - 1, mid(up(pe, 1)), 0.0)
        upn = pe[0:ROWS]
        dnn = pe[2 * HALO:2 * HALO + ROWS]
        q = lane % 4
        sh = jnp.where(q == 0, left, jnp.where(q == 1, right, jnp.where(q == 2, upn, dnn)))
    else:
        sh = jnp.where(lane % 2 == 0, mid(down(pe, 1)), mid(up(pe, 1)))
    pm = p + mu_ref[...] * (sh - p)

    r = pm[:, 0:RW]
    k = pm[:, RW:2 * RW]
    v = pm[:, 2 * RW:3 * RW]
    o = 3 * RW
    w_lo = jnp.tanh(pm[:, o:o + 2 * LORA])
    a_lo = pm[:, o + 2 * LORA:o + 4 * LORA]
    gsig_ref[...] = _sigmoid(pm[:, o + 4 * LORA:o + 4 * LORA + GATE_LORA]).astype(BF16)

    seg = seg_ref[...]
    kkr = k * kk_ref[...]
    ss = _dot_exact_rhs(kkr * kkr, seg, 2)
    kkn = kkr / jnp.maximum(jnp.sqrt(ss), 1e-12)
    r_ref[...] = r
    v_ref[...] = v
    kkn_ref[...] = kkn

    bonus = None
    for d, (lw_ref, kd_ref, b_ref) in enumerate(((lwf_ref, kdf_ref, bf_ref), (lwb_ref, kdb_ref, bb_ref))):
        z = dw0_ref[d:d + 1, :] + _dot3(w_lo, dw2_ref[d])
        lw_ref[...] = -math.exp(-0.5) * _sigmoid(z)
        a = _sigmoid(a0_ref[d:d + 1, :] + _dot3(a_lo, a2_ref[d]))
        kd = k * (1.0 + (a - 1.0) * ka_ref[...])
        kd_ref[...] = kd
        b_ref[...] = kkn * a
        bd = _dot_exact_rhs(r * kd * rk_ref[...], seg, 2) * v
        bonus = bd if bonus is None else bonus + bd
    bonus_ref[...] = bonus


def _front(proj, grid_mode, n_seq, seq_len, tok0, weights):
    n_chunks = seq_len // ROWS
    n_tok = n_seq * seq_len
    blk0 = tok0 // ROWS
    hpb = ROWS // HALO
    last_halo = N_TOK // HALO - 1

    def cur_map(b, c):
        return (blk0 + b * n_chunks + c, 0)

    def prev_map(b, c):
        return (jnp.maximum((blk0 + b * n_chunks + c) * hpb - 1, 0), 0)

    def next_map(b, c):
        return (jnp.minimum((blk0 + b * n_chunks + c + 1) * hpb, last_halo), 0)

    def full(a):
        nd = a.ndim
        return pl.BlockSpec(a.shape, lambda b, c, _nd=nd: (0,) * _nd)

    def out_map(b, c):
        return (b * n_chunks + c, 0)

    f32_out = jax.ShapeDtypeStruct((n_tok, RW), F32)
    out_shape = [jax.ShapeDtypeStruct((n_tok, POOL_W), BF16), jax.ShapeDtypeStruct((n_tok, GATE_LORA), BF16)]
    out_shape += [f32_out] * (N_FRONT_OUT - 2)
    out_specs = [pl.BlockSpec((ROWS, POOL_W), out_map), pl.BlockSpec((ROWS, GATE_LORA), out_map)]
    out_specs += [pl.BlockSpec((ROWS, RW), out_map)] * (N_FRONT_OUT - 2)
    assert len(weights) == N_FRONT_IN - 3
    return pl.pallas_call(
        functools.partial(_front_kernel, grid_mode, n_chunks, seq_len),
        out_shape=out_shape,
        grid=(n_seq, n_chunks),
        in_specs=[
            pl.BlockSpec((ROWS, OFF_MERGE), cur_map),
            pl.BlockSpec((HALO, OFF_MERGE), prev_map),
            pl.BlockSpec((HALO, OFF_MERGE), next_map),
        ] + [full(a) for a in weights],
        out_specs=out_specs,
        compiler_params=pltpu.CompilerParams(dimension_semantics=("arbitrary", "arbitrary"),
                                             vmem_limit_bytes=VMEM_LIMIT),
        name="front_grid" if grid_mode else "front_ctx",
    )(proj, proj, proj, *weights)


def _scan_kernel(n_chunks, has_init, *refs):
    (r_ref, v_ref, kk_ref, bonus_ref, lwf_ref, lwb_ref, kdf_ref, kdb_ref, bf_ref, bb_ref) = refs[:10]
    if has_init:
        s0f_ref, s0b_ref = refs[10:12]
    y_ref, sf_ref, sb_ref = refs[-3:]
    C = CHUNK
    P2 = 2 * C
    ri = lax.broadcasted_iota(jnp.int32, (P2, P2), 0)
    ci = lax.broadcasted_iota(jnp.int32, (P2, P2), 1)
    same = (ri // C) == (ci // C)
    rp, cp = ri % C, ci % C
    eye = (ri == ci).astype(F32)
    blk = {}
    s = INV_BLOCK
    while s <= C:
        blk[s] = (ri // s) == (ci // s)
        s *= 2
    ti = lax.broadcasted_iota(jnp.int32, (C, C), 0)
    tj = lax.broadcasted_iota(jnp.int32, (C, C), 1)
    head_a = lax.broadcasted_iota(jnp.int32, (C, LANES), 1) < HEAD

    def stack(x):
        return jnp.concatenate([jnp.where(head_a, x, 0.0), jnp.where(head_a, 0.0, x)], axis=0)

    def d3(a, b, dims=_NN):
        return _dot(a[0], b[0], dims) + (_dot(a[0], b[1], dims) + _dot(a[1], b[0], dims))

    def chunks(chains):
        n = range(len(chains))
        masks, sls = [], []
        for (s_ref, p, row0, lw_ref, kd_ref, b_ref, reverse) in chains:
            if reverse:
                masks.append((same & (cp > rp), same & (cp >= rp), (tj >= ti), 0))
            else:
                masks.append((same & (cp < rp), same & (cp <= rp), (tj <= ti), C - 1))
            sls.append((pl.ds(row0, C), slice(p * LANES, (p + 1) * LANES)))
        lw = [chains[i][3][sls[i]] for i in n]
        cum = [_dot_exact_lhs(masks[i][2].astype(BF16), lw[i], 3) for i in n]
        e_i = [jnp.exp(cum[i]) for i in n]
        e_n = [jnp.exp(-cum[i]) for i in n]
        QR = [_split2(jnp.concatenate([stack(kk_ref[sls[i]] * jnp.exp(cum[i] - lw[i])),
                                       stack(r_ref[sls[i]] * e_i[i])], axis=0)) for i in n]
        BK = [_split2(jnp.concatenate([stack(chains[i][5][sls[i]] * e_n[i]),
                                       stack(chains[i][4][sls[i]] * e_n[i])], axis=0)) for i in n]
        v_st = [stack(v_ref[sls[i]]) for i in n]
        Vs = [_split2(v_st[i]) for i in n]
        S = [chains[i][0][0, chains[i][1]] for i in n]
        G = [_dot(QR[i][0], BK[i][0], _NT) for i in n]
        L = [jnp.where(masks[i][0], G[i][0:P2, 0:P2], 0.0) for i in n]
        Lk = [jnp.where(masks[i][0], G[i][0:P2, P2:2 * P2], 0.0) for i in n]
        RBK = [jnp.concatenate([jnp.where(masks[i][1], G[i][P2:2 * P2, 0:P2], 0.0),
                                jnp.where(masks[i][1], G[i][P2:2 * P2, P2:2 * P2], 0.0)], axis=1) for i in n]
        L0 = [_split2(jnp.where(blk[INV_BLOCK], L[i], 0.0)) for i in n]
        X = [eye - jnp.where(blk[INV_BLOCK], L[i], 0.0) for i in n]
        Pw = L0
        s = 2
        while s < INV_BLOCK:
            Pw = [_split2(d3(Pw[i], Pw[i])) for i in n]
            X = [X[i] + d3(_split2(X[i]), Pw[i]) for i in n]
            s *= 2
        s = INV_BLOCK
        while s < C:
            Dm = [_split2(X[i]) for i in n]
            E = [d3(_split2(jnp.where(blk[2 * s] & jnp.logical_not(blk[s]), L[i], 0.0)), Dm[i]) for i in n]
            X = [X[i] - d3(Dm[i], _split2(E[i])) for i in n]
            s *= 2
        Ssp = [_split2(S[i]) for i in n]
        QA = [d3((QR[i][0][0:P2], QR[i][1][0:P2]), Ssp[i], _NT) for i in n]
        RA = [_dot(QR[i][0][P2:2 * P2], Ssp[i][0], _NT) for i in n]
        W = [QA[i] + d3(_split2(Lk[i]), Vs[i]) for i in n]
        U = [-d3(_split2(X[i]), _split2(W[i])) for i in n]
        UV = [jnp.concatenate([U[i], v_st[i]], axis=0) for i in n]
        Ys = [RA[i] + _dot(RBK[i].astype(BF16), UV[i].astype(BF16)) for i in n]
        dS = [d3(_split2(UV[i].T), BK[i]) for i in n]
        for i in n:
            last = masks[i][3]
            chains[i][0][0, chains[i][1]] = (S[i] + dS[i]) * e_i[i][last:last + 1, :]
            y_ref[sls[i]] += Ys[i][0:C] + Ys[i][C:P2]

    y_ref[...] = bonus_ref[...]
    if has_init:
        sf_ref[...] = s0f_ref[...]
        sb_ref[...] = s0b_ref[...]
    else:
        sf_ref[...] = jnp.zeros_like(sf_ref)
        sb_ref[...] = jnp.zeros_like(sb_ref)

    def body(c, carry):
        rf = pl.multiple_of(c * C, C)
        rb = pl.multiple_of((n_chunks - 1 - c) * C, C)
        for p0 in range(0, N_PAIR, PAIRS_PER_GROUP):
            chains = []
            for p in range(p0, p0 + PAIRS_PER_GROUP):
                chains.append((sf_ref, p, rf, lwf_ref, kdf_ref, bf_ref, False))
                chains.append((sb_ref, p, rb, lwb_ref, kdb_ref, bb_ref, True))
            chunks(chains)
        return carry

    lax.fori_loop(0, n_chunks, body, 0)


def _scan(arrs, n_seq, seq_len, init):
    n_tok = n_seq * seq_len
    n_chunks = seq_len // CHUNK
    tok_spec = pl.BlockSpec((seq_len, RW), lambda b: (b, 0))
    st_spec = pl.BlockSpec((1, N_PAIR, LANES, LANES), lambda b: (b, 0, 0, 0))
    st_shape = jax.ShapeDtypeStruct((n_seq, N_PAIR, LANES, LANES), F32)
    ins = list(arrs)
    in_specs = [tok_spec] * 10
    if init is not None:
        ins += list(init)
        in_specs += [st_spec, st_spec]
    return pl.pallas_call(
        functools.partial(_scan_kernel, n_chunks, init is not None),
        out_shape=[jax.ShapeDtypeStruct((n_tok, RW), F32), st_shape, st_shape],
        grid=(n_seq,),
        in_specs=in_specs,
        out_specs=[tok_spec, st_spec, st_spec],
        compiler_params=pltpu.CompilerParams(dimension_semantics=("arbitrary",), vmem_limit_bytes=VMEM_LIMIT),
        name="scan_init" if init is not None else "scan_zero",
    )(*ins)


def _post_kernel(xc_ref, xl_ref, mod_ref, yprec_ref, yprel_ref, gsigc_ref, gsigl_ref, yc_ref, yl_ref,
                 nmw_ref, wm_ref, bm_ref, wpo_ref, seg_ref, lnw_ref, lnb_ref, gw2_ref, wro_ref, wo_ref,
                 nfw_ref, rw_ref, rb_ref,
                 x1_ref, hn2_ref, code_ref, w4_ref, meta_ref, texp_ref, carry_ref):
    i = pl.program_id(0)

    @pl.when(i == 0)
    def _():
        carry_ref[...] = jnp.zeros_like(carry_ref)

    x = _path_tile(xc_ref, xl_ref, TM_POST)
    mod = mod_ref[0]
    shift1, scale1, gate1 = mod[:, 0:D], mod[:, D:2 * D], mod[:, 2 * D:3 * D]
    shift2, scale2 = mod[:, 3 * D:4 * D], mod[:, 4 * D:5 * D]
    hn = _modnorm(x, nmw_ref[...], scale1, shift1)
    merge = _sigmoid(_dot(hn.astype(BF16), wm_ref[...]) + bm_ref[...])
    y_pool = _dot(_path_tile(yprec_ref, yprel_ref, TM_POST), wpo_ref[...])

    y = _path_tile(yc_ref, yl_ref, TM_POST)
    seg = seg_ref[...]
    mu = _dot_exact_rhs(y, seg, 2) * (1.0 / HEAD)
    yc = y - mu
    var = _dot_exact_rhs(yc * yc, seg, 2) * (1.0 / HEAD)
    yn = yc * lax.rsqrt(var + GN_EPS) * lnw_ref[...] + lnb_ref[...]
    g = _dot(_path_tile(gsigc_ref, gsigl_ref, TM_POST), gw2_ref[...])
    y_rwkv = _dot((yn * g).astype(BF16), wro_ref[...])

    mixed = merge[:, 0:D] * y_pool + merge[:, D:2 * D] * y_rwkv
    x1 = x + gate1 * _dot(mixed.astype(BF16), wo_ref[...])
    x1_ref[...] = x1
    hn2 = _modnorm(x1, nfw_ref[...], scale2, shift2)
    hn2_ref[...] = hn2

    logits = _dot3(hn2, rw_ref[...]) + rb_ref[...]
    lane = lax.broadcasted_iota(jnp.int32, logits.shape, 1)
    work = logits
    sel = None
    top = None
    for j in range(TOP_K):
        m = jnp.max(work, axis=-1, keepdims=True)
        if j == 0:
            top = m
        idx = jnp.min(jnp.where(work == m, lane, LANES), axis=-1, keepdims=True)
        pick = lane == idx
        sel = pick if sel is None else (sel | pick)
        work = jnp.where(pick, -jnp.inf, work)
    e = jnp.where(sel, jnp.exp(logits - top), 0.0)
    comb = e / jnp.sum(e, axis=-1, keepdims=True)

    tm = logits.shape[0]
    sel_b = jnp.where(sel, 1.0, 0.0).astype(BF16)
    before = (lax.broadcasted_iota(jnp.int32, (tm, tm), 1) < lax.broadcasted_iota(jnp.int32, (tm, tm), 0))
    rank = carry_ref[...] + _dot(before.astype(BF16), sel_b)
    carry = carry_ref[...] + jnp.sum(sel_b.astype(F32), axis=0, keepdims=True)
    carry_ref[...] = carry
    lower_e = (lax.broadcasted_iota(jnp.int32, (LANES, LANES), 0) < lax.broadcasted_iota(jnp.int32, (LANES, LANES), 1))
    rowpos = _dot(sel_b, lower_e.astype(BF16))
    code = jnp.zeros(logits.shape, jnp.int32)
    w4 = jnp.zeros(logits.shape, F32)
    for k in range(TOP_K):
        mk = sel & (rowpos == float(k))
        ek = jnp.sum(jnp.where(mk, lane, 0), axis=-1, keepdims=True)
        rk = jnp.sum(jnp.where(mk, rank, 0.0), axis=-1, keepdims=True).astype(jnp.int32)
        wk = jnp.sum(jnp.where(mk, comb, 0.0), axis=-1, keepdims=True)
        code = jnp.where(lane == k, ek * (1 << RANK_BITS) + rk, code)
        w4 = jnp.where(lane == k, wk, w4)
    code_ref[...] = code
    w4_ref[...] = w4

    tiles = jnp.floor((carry + (TM_SLOT - 1)) * (1.0 / TM_SLOT))
    incl = (lax.broadcasted_iota(jnp.int32, (LANES, LANES), 0) <= lax.broadcasted_iota(jnp.int32, (LANES, LANES), 1))
    tiles8 = jnp.broadcast_to(tiles, (8, LANES)).astype(BF16)
    cum = _dot(tiles8, incl.astype(BF16))[0:1]
    lane1 = lax.broadcasted_iota(jnp.int32, (1, LANES), 1)
    offs = ((cum - tiles) * TM_SLOT).astype(jnp.int32)
    n_used = jnp.sum(jnp.where(lane1 == N_EXP - 1, cum, 0.0), axis=-1, keepdims=True).astype(jnp.int32)
    meta_ref[...] = jnp.where(lane1 == META_NT, n_used, jnp.where(lane1 < N_EXP, offs, 0))
    tile_id = lax.broadcasted_iota(jnp.int32, (N_SLOT_TILES, LANES), 0).astype(F32)
    done = jnp.where((lax.broadcasted_iota(jnp.int32, (N_SLOT_TILES, LANES), 1) < N_EXP) & (cum <= tile_id), 1, 0)
    texp = jnp.minimum(jnp.sum(done, axis=-1, keepdims=True), N_EXP - 1)
    texp_ref[...] = jnp.broadcast_to(texp, (N_SLOT_TILES, LANES))


def _post(x, mod3, ypre, gsig, y, weights):
    def tile(n, dt):
        return pl.BlockSpec((TM_POST, n), lambda i: (i, 0))

    def full(a):
        nd = a.ndim
        return pl.BlockSpec(a.shape, lambda i, _nd=nd: (0,) * _nd)

    def const(shape):
        return pl.BlockSpec(shape, lambda i: (0, 0))

    return pl.pallas_call(
        _post_kernel,
        out_shape=[jax.ShapeDtypeStruct((N_TOK, D), F32), jax.ShapeDtypeStruct((N_TOK, D), F32),
                   jax.ShapeDtypeStruct((N_TOK, LANES), jnp.int32), jax.ShapeDtypeStruct((N_TOK, LANES), F32),
                   jax.ShapeDtypeStruct((1, LANES), jnp.int32),
                   jax.ShapeDtypeStruct((N_SLOT_TILES, LANES), jnp.int32)],
        grid=(N_TOK // TM_POST,),
        in_specs=_path_specs(TM_POST) + [
            pl.BlockSpec((1, 1, N_MOD * D), lambda i: (_mod_row(i, TM_POST), 0, 0)),
        ] + _path_specs(TM_POST, POOL_W) + _path_specs(TM_POST, GATE_LORA) + _path_specs(TM_POST, RW)
        + [full(a) for a in weights],
        out_specs=[tile(D, F32), tile(D, F32), tile(LANES, jnp.int32), tile(LANES, F32),
                   const((1, LANES)), const((N_SLOT_TILES, LANES))],
        scratch_shapes=[pltpu.VMEM((1, LANES), F32)],
        compiler_params=pltpu.CompilerParams(dimension_semantics=("arbitrary",), vmem_limit_bytes=VMEM_LIMIT),
        name="post",
    )(*x, mod3, *ypre, *gsig, *y, *weights)


def _slot_of(code, meta_ref):
    return meta_ref[code >> RANK_BITS] + (code & ((1 << RANK_BITS) - 1))


def _load_codes(code_ref, code_smem, sem):
    cp = pltpu.make_async_copy(code_ref.at[pl.program_id(0)], code_smem, sem)
    cp.start()
    cp.wait()


def _dispatch_kernel(code_ref, meta_ref, hn2_ref, xs_ref, code_smem, zero_buf, sem_c, sem_z, sem):
    @pl.when(pl.program_id(0) == 0)
    def _():
        zero_buf[...] = jnp.zeros_like(zero_buf)

        def pad_tile(e):
            end = meta_ref[e + 1] if e + 1 < N_EXP else meta_ref[META_NT] * TM_SLOT
            start = pl.multiple_of(end - TM_SLOT, TM_SLOT)
            return end > meta_ref[e], pltpu.make_async_copy(zero_buf, xs_ref.at[pl.ds(start, TM_SLOT)], sem_z)

        def tail_tile(j):
            return pltpu.make_async_copy(zero_buf, xs_ref.at[pl.ds(pl.multiple_of(j * TM_SLOT, TM_SLOT), TM_SLOT)], sem_z)

        n_used = meta_ref[META_NT]
        for e in range(N_EXP):
            nonempty, cp = pad_tile(e)
            pl.when(nonempty)(cp.start)

        def start_tail(j, c):
            tail_tile(j).start()
            return c

        def wait_tail(j, c):
            tail_tile(j).wait()
            return c

        lax.fori_loop(n_used, N_SLOT_TILES, start_tail, 0)
        for e in range(N_EXP):
            nonempty, cp = pad_tile(e)
            pl.when(nonempty)(cp.wait)
        lax.fori_loop(n_used, N_SLOT_TILES, wait_tail, 0)

    _load_codes(code_ref, code_smem, sem_c)

    def body(t, carry):
        for k in range(TOP_K):
            slot = _slot_of(code_smem[t * TOP_K + k], meta_ref)
            pltpu.make_async_copy(hn2_ref.at[pl.ds(t, 1)], xs_ref.at[pl.ds(slot, 1)], sem).start(priority=k % 2)
        return carry

    lax.fori_loop(0, TM_DISP, body, 0, unroll=8)
    for _ in range(TOP_K):
        pltpu.make_async_copy(hn2_ref, xs_ref.at[pl.ds(0, TM_DISP)], sem).wait()


def _dispatch(code2d, meta, hn2):
    n_steps = N_TOK // TM_DISP
    return pl.pallas_call(
        _dispatch_kernel,
        out_shape=jax.ShapeDtypeStruct((N_SLOTS, D), F32),
        grid=(n_steps,),
        in_specs=[
            pl.BlockSpec(code2d.shape, lambda i: (0, 0)),
            pl.BlockSpec(memory_space=pltpu.SMEM),
            pl.BlockSpec((TM_DISP, D), lambda i: (i, 0)),
        ],
        out_specs=pl.BlockSpec(memory_space=pl.ANY),
        scratch_shapes=[pltpu.SMEM((TM_DISP * TOP_K,), jnp.int32), pltpu.VMEM((TM_SLOT, D), F32),
                        pltpu.SemaphoreType.DMA, pltpu.SemaphoreType.DMA, pltpu.SemaphoreType.DMA],
        compiler_params=pltpu.CompilerParams(dimension_semantics=("arbitrary",), vmem_limit_bytes=VMEM_LIMIT),
        name="dispatch",
    )(code2d, meta, hn2)


def _ffn_kernel(texp_ref, meta_ref, xs_ref, wgu_ref, bgu_ref, wd_ref, bd_ref, ys_ref, wgu_bf, wd_bf):
    i = pl.program_id(0)
    valid = i < meta_ref[META_NT]
    fresh = (i == 0) | (texp_ref[i] != texp_ref[jnp.maximum(i - 1, 0)])

    @pl.when(valid & fresh)
    def _():
        wgu_bf[...] = wgu_ref[0].astype(BF16)
        wd_bf[...] = wd_ref[0].astype(BF16)

    @pl.when(valid)
    def _():
        gu = _dot(xs_ref[...].astype(BF16), wgu_bf[...]) + bgu_ref[0]
        glu = jnp.minimum(gu[:, 0:D_FF], SWIGLU_LIMIT)
        lin = jnp.clip(gu[:, D_FF:2 * D_FF], -SWIGLU_LIMIT, SWIGLU_LIMIT)
        act = glu * _sigmoid(SWIGLU_ALPHA * glu) * (lin + 1.0)
        ys_ref[...] = _dot(act.astype(BF16), wd_bf[...]) + bd_ref[0]

    @pl.when(jnp.logical_not(valid))
    def _():
        ys_ref[...] = jnp.zeros_like(ys_ref)


def _ffn(texp, meta, xs, wgu, bgu, wd, bd):
    def used_tile(i, te, me):
        return (jnp.minimum(i, me[META_NT] - 1), 0)

    return pl.pallas_call(
        _ffn_kernel,
        out_shape=jax.ShapeDtypeStruct((N_SLOTS, D), F32),
        grid_spec=pltpu.PrefetchScalarGridSpec(
            num_scalar_prefetch=2,
            grid=(N_SLOT_TILES,),
            in_specs=[
                pl.BlockSpec((TM_SLOT, D), used_tile),
                pl.BlockSpec((1, D, 2 * D_FF), lambda i, te, me: (te[i], 0, 0)),
                pl.BlockSpec((1, 1, 2 * D_FF), lambda i, te, me: (te[i], 0, 0)),
                pl.BlockSpec((1, D_FF, D), lambda i, te, me: (te[i], 0, 0)),
                pl.BlockSpec((1, 1, D), lambda i, te, me: (te[i], 0, 0)),
            ],
            out_specs=pl.BlockSpec((TM_SLOT, D), lambda i, te, me: (i, 0)),
            scratch_shapes=[pltpu.VMEM((D, 2 * D_FF), BF16), pltpu.VMEM((D_FF, D), BF16)],
        ),
        compiler_params=pltpu.CompilerParams(dimension_semantics=("arbitrary",), vmem_limit_bytes=VMEM_LIMIT),
        name="ffn",
    )(texp, meta, xs, wgu, bgu, wd, bd)


def _combine_kernel(code_ref, meta_ref, x1_ref, w4_ref, mod_ref, nfw_ref, ys_ref, oc_ref, ol_ref,
                    code_smem, buf, sem_c, sem):
    _load_codes(code_ref, code_smem, sem_c)

    def body(t, carry):
        for k in range(TOP_K):
            slot = _slot_of(code_smem[t * TOP_K + k], meta_ref)
            pltpu.make_async_copy(ys_ref.at[pl.ds(slot, 1)], buf.at[k, pl.ds(t, 1)], sem).start(priority=k % 2)
        return carry

    lax.fori_loop(0, TM_DISP, body, 0, unroll=8)
    for k in range(TOP_K):
        pltpu.make_async_copy(ys_ref.at[pl.ds(0, TM_DISP)], buf.at[k], sem).wait()
    w4 = w4_ref[...]
    moe = w4[:, 0:1] * buf[0]
    for k in range(1, TOP_K):
        moe = moe + w4[:, k:k + 1] * buf[k]
    gate2 = mod_ref[0][:, 5 * D:6 * D]
    x2 = x1_ref[...] + gate2 * moe
    ms = jnp.mean(x2 * x2, axis=-1, keepdims=True)
    out = x2 * lax.rsqrt(ms + RMS_EPS) * nfw_ref[...]
    is_ctx = pl.program_id(0) < N_CTX_TOK // TM_DISP

    @pl.when(is_ctx)
    def _():
        oc_ref[...] = out

    @pl.when(jnp.logical_not(is_ctx))
    def _():
        ol_ref[...] = out


def _combine(code2d, meta, x1, w4, mod3, nfw, ys):
    n_steps = N_TOK // TM_DISP
    n_ctx = N_CTX_TOK // TM_DISP
    return pl.pallas_call(
        _combine_kernel,
        out_shape=[jax.ShapeDtypeStruct((N_CTX_TOK, D), F32), jax.ShapeDtypeStruct((N_LAT_TOK, D), F32)],
        grid=(n_steps,),
        in_specs=[
            pl.BlockSpec(code2d.shape, lambda i: (0, 0)),
            pl.BlockSpec(memory_space=pltpu.SMEM),
            pl.BlockSpec((TM_DISP, D), lambda i: (i, 0)),
            pl.BlockSpec((TM_DISP, LANES), lambda i: (i, 0)),
            pl.BlockSpec((1, 1, N_MOD * D), lambda i: (_mod_row(i, TM_DISP), 0, 0)),
            pl.BlockSpec((1, D), lambda i: (0, 0)),
            pl.BlockSpec(memory_space=pl.ANY),
        ],
        out_specs=[pl.BlockSpec((TM_DISP, D), lambda i: (jnp.minimum(i, n_ctx - 1), 0)),
                   pl.BlockSpec((TM_DISP, D), lambda i: (jnp.maximum(i - n_ctx, 0), 0))],
        scratch_shapes=[pltpu.SMEM((TM_DISP * TOP_K,), jnp.int32), pltpu.VMEM((TOP_K, TM_DISP, D), F32),
                        pltpu.SemaphoreType.DMA, pltpu.SemaphoreType.DMA],
        compiler_params=pltpu.CompilerParams(dimension_semantics=("arbitrary",), vmem_limit_bytes=VMEM_LIMIT),
        name="combine",
    )(code2d, meta, x1, w4, mod3, nfw, ys)


def _pair_blockdiag(s):
    b = s.shape[0]
    s = s.reshape(b, N_PAIR, 2, HEAD, HEAD)
    z = jnp.zeros((b, N_PAIR, HEAD, HEAD), s.dtype)
    top = jnp.concatenate([s[:, :, 0], z], axis=-1)
    bot = jnp.concatenate([z, s[:, :, 1]], axis=-1)
    return jnp.concatenate([top, bot], axis=-2)


def _pair_unblock(s):
    b = s.shape[0]
    a = s[:, :, 0:HEAD, 0:HEAD]
    c = s[:, :, HEAD:, HEAD:]
    return jnp.stack([a, c], axis=2).reshape(b, 1, N_HEAD, HEAD, HEAD)


def kernel(x_prompt, x_sample, state_fwd, state_bwd, c, c_ctx, w_mod, b_mod, norm_mix_w, w_in, b_merge, pool_w, pool_scale, w_pool_out, shift_mu, decay_w0, decay_w2, iclr_a0, iclr_a2, gate_w2, k_k, k_a, r_k, ln_x_w, ln_x_b, w_rwkv_out, w_o, norm_ffn_w, router_w, router_b, expert_w_gu, expert_b_gu, expert_w_down, expert_b_down, norm_final_w):
    l = 0
    x_ctx = x_prompt.reshape(N_CTX_TOK, D)
    x_lat = x_sample.reshape(N_LAT_TOK, D)
    cvec = jnp.concatenate([c_ctx[None, :], c, jnp.zeros((MOD_ROWS - 1 - N_LAT_SEQ, D), F32)], axis=0)
    mod = _modulation(cvec, w_mod[l], b_mod[l][None, :])
    mod3 = mod.reshape(MOD_ROWS, 1, N_MOD * D)

    w_in_l = w_in[l]
    proj = _projection(x_ctx, x_lat, mod3, norm_mix_w[l][None, :], w_in_l[:, :OFF_MERGE].astype(BF16))

    zl = jnp.zeros((LORA, RW), F32)
    dw2 = jnp.stack([jnp.concatenate([decay_w2[l, 0], zl], 0), jnp.concatenate([zl, decay_w2[l, 1]], 0)])
    a2 = jnp.stack([jnp.concatenate([iclr_a2[l, 0], zl], 0), jnp.concatenate([zl, iclr_a2[l, 1]], 0)])
    hid = jnp.arange(RW) // HEAD
    seg = (hid[:, None] == hid[None, :]).astype(BF16)
    front_w = (pool_w[l].astype(BF16), pool_scale[l][None, :], shift_mu[l][None, :],
               decay_w0[l], dw2, iclr_a0[l], a2, k_k[l][None, :], k_a[l][None, :],
               r_k[l].reshape(1, RW), seg)
    f_ctx = _front(proj, False, N_CTX_SEQ, T_CTX, 0, front_w)
    f_lat = _front(proj, True, N_LAT_SEQ, T_LAT, N_CTX_TOK, front_w)

    y_ctx, sf, sb = _scan(f_ctx[2:], N_CTX_SEQ, T_CTX, None)
    init = (_pair_blockdiag(state_fwd[:, l]), _pair_blockdiag(state_bwd[:, l]))
    y_lat, _, _ = _scan(f_lat[2:], N_LAT_SEQ, T_LAT, init)

    rw_pad = jnp.concatenate([router_w[l], jnp.zeros((D, LANES - N_EXP), F32)], axis=1)
    rb_pad = jnp.concatenate([router_b[l], jnp.full((LANES - N_EXP,), -1e30, F32)])[None, :]
    post_w = (norm_mix_w[l][None, :], w_in_l[:, OFF_MERGE:].astype(BF16), b_merge[l][None, :],
              w_pool_out[l].astype(BF16), seg, ln_x_w[l][None, :], ln_x_b[l][None, :],
              gate_w2[l].astype(BF16), w_rwkv_out[l].astype(BF16), w_o[l].astype(BF16),
              norm_ffn_w[l][None, :], rw_pad, rb_pad)
    x1, hn2, code, w4, meta, texp = _post((x_ctx, x_lat), mod3, (f_ctx[0], f_lat[0]), (f_ctx[1], f_lat[1]),
                                          (y_ctx, y_lat), post_w)

    code2d = code[:, :TOP_K].reshape(N_TOK // TM_DISP, TM_DISP * TOP_K)
    meta1 = meta.reshape(LANES)
    xs = _dispatch(code2d, meta1, hn2)
    ys = _ffn(texp[:, 0], meta1, xs, expert_w_gu[l], expert_b_gu[l][:, None, :],
              expert_w_down[l], expert_b_down[l][:, None, :])
    out_ctx, out_lat = _combine(code2d, meta1, x1, w4, mod3, norm_final_w[None, :], ys)
    y_prompt = out_ctx.reshape(N_CTX_SEQ, T_CTX, D)
    y_sample = out_lat.reshape(N_LAT_SEQ, T_LAT, D)
    return (y_prompt, y_sample, _pair_unblock(sf), _pair_unblock(sb))
```

```python
import functools
import math

import jax
import jax.numpy as jnp
from jax import lax
from jax.experimental import pallas as pl
from jax.experimental.pallas import tpu as pltpu

F32 = jnp.float32
BF16 = jnp.bfloat16

D = 1024
N_CTX_SEQ, T_CTX = 32, 256
N_LAT_SEQ, T_LAT = 4, 1024
GRID_W = 64
N_CTX_TOK = N_CTX_SEQ * T_CTX
N_LAT_TOK = N_LAT_SEQ * T_LAT
N_TOK = N_CTX_TOK + N_LAT_TOK

POOL_W = 512
POOL_G = 4
POOL_GD = POOL_W // POOL_G
POOL_WINDOWS = (2, 4, 8, 16)
RW = 512
HEAD = 64
N_HEAD = RW // HEAD
N_PAIR = N_HEAD // 2
LORA = 64
GATE_LORA = 128
N_EXP = 32
TOP_K = 4
D_FF = 1024
SWIGLU_ALPHA = 1.702
SWIGLU_LIMIT = 7.0
RMS_EPS = 1e-6
GN_EPS = 1e-5 * HEAD
N_MOD = 6

OFF_R = POOL_W
SHIFT_COLS = 3 * RW + 4 * LORA + GATE_LORA
OFF_MERGE = OFF_R + SHIFT_COLS
IN_COLS = OFF_MERGE + 2 * D

LANES = 128
CHUNK = 64
INV_BLOCK = 8
SCAN_TOKENS_PER_STEP = 512
ROWS = 256
HALO = 64
EXT = ROWS + 2 * HALO
TM_PROJ = 512
TM_POST = 256
TM_DISP = 256
TM_SLOT = 256
N_SLOT_TILES = N_TOK * TOP_K // TM_SLOT + N_EXP
N_SLOTS = N_SLOT_TILES * TM_SLOT
RANK_BITS = 14
META_NT = N_EXP
MOD_ROWS = 8
VMEM_LIMIT = 56 * 1024 * 1024


def _sigmoid(x):
    return 1.0 / (1.0 + jnp.exp(-x))


def _split2(a):
    hi = a.astype(BF16)
    lo = (a - hi.astype(F32)).astype(BF16)
    return hi, lo


_NN = (((1,), (0,)), ((), ()))
_NT = (((1,), (1,)), ((), ()))


def _dot(a, b, dims=_NN):
    return lax.dot_general(a, b, dims, preferred_element_type=F32)


def _dot3(a, b, dims=_NN):
    ah, al = _split2(a)
    bh, bl = _split2(b)
    return _dot(ah, bh, dims) + (_dot(ah, bl, dims) + _dot(al, bh, dims))


def _dot_exact_lhs(a_bf16, b, passes):
    acc = None
    rem = b
    for _ in range(passes):
        part = rem.astype(BF16)
        term = _dot(a_bf16, part)
        acc = term if acc is None else acc + term
        rem = rem - part.astype(F32)
    return acc


def _dot_exact_rhs(a, b_bf16, passes):
    acc = None
    rem = a
    for _ in range(passes):
        part = rem.astype(BF16)
        term = _dot(part, b_bf16)
        acc = term if acc is None else acc + term
        rem = rem - part.astype(F32)
    return acc


def _modnorm(x, w, scale, shift):
    ms = jnp.mean(x * x, axis=-1, keepdims=True)
    return x * lax.rsqrt(ms + RMS_EPS) * w * (1.0 + scale) + shift


def _mod_row(i, tm):
    n_ctx = N_CTX_TOK // tm
    per = T_LAT // tm
    return jnp.where(i < n_ctx, 0, 1 + (i - n_ctx) // per)


def _mod_kernel(c_ref, w_ref, b_ref, o_ref):
    c = c_ref[...]
    s = c * _sigmoid(c)
    o_ref[...] = _dot3(s, w_ref[...]) + b_ref[...]


def _modulation(cvec, w_mod, b_mod):
    return pl.pallas_call(
        _mod_kernel,
        out_shape=jax.ShapeDtypeStruct((MOD_ROWS, N_MOD * D), F32),
        grid=(N_MOD,),
        in_specs=[
            pl.BlockSpec((MOD_ROWS, D), lambda j: (0, 0)),
            pl.BlockSpec((D, D), lambda j: (0, j)),
            pl.BlockSpec((1, D), lambda j: (0, j)),
        ],
        out_specs=pl.BlockSpec((MOD_ROWS, D), lambda j: (0, j)),
        compiler_params=pltpu.CompilerParams(dimension_semantics=("arbitrary",), vmem_limit_bytes=VMEM_LIMIT),
        name="mod",
    )(cvec, w_mod, b_mod)


def _path_specs(tm, n_cols=D):
    n_ctx = N_CTX_TOK // tm
    return [pl.BlockSpec((tm, n_cols), lambda i: (jnp.minimum(i, n_ctx - 1), 0)),
            pl.BlockSpec((tm, n_cols), lambda i: (jnp.maximum(i - n_ctx, 0), 0))]


def _path_tile(c_ref, l_ref, tm):
    return jnp.where(pl.program_id(0) < N_CTX_TOK // tm, c_ref[...], l_ref[...])


def _proj_kernel(xc_ref, xl_ref, mod_ref, nw_ref, w_ref, o_ref):
    mod = mod_ref[0]
    hn = _modnorm(_path_tile(xc_ref, xl_ref, TM_PROJ), nw_ref[...], mod[:, D:2 * D], mod[:, 0:D])
    o_ref[...] = _dot(hn.astype(BF16), w_ref[...])


def _projection(x_ctx, x_lat, mod3, norm_w, w_a):
    n_cols = w_a.shape[1]
    return pl.pallas_call(
        _proj_kernel,
        out_shape=jax.ShapeDtypeStruct((N_TOK, n_cols), F32),
        grid=(N_TOK // TM_PROJ,),
        in_specs=_path_specs(TM_PROJ) + [
            pl.BlockSpec((1, 1, N_MOD * D), lambda i: (_mod_row(i, TM_PROJ), 0, 0)),
            pl.BlockSpec((1, D), lambda i: (0, 0)),
            pl.BlockSpec((D, n_cols), lambda i: (0, 0)),
        ],
        out_specs=pl.BlockSpec((TM_PROJ, n_cols), lambda i: (i, 0)),
        compiler_params=pltpu.CompilerParams(dimension_semantics=("arbitrary",), vmem_limit_bytes=VMEM_LIMIT),
        name="proj",
    )(x_ctx, x_lat, mod3, norm_w, w_a)


N_FRONT_IN = 14
N_FRONT_OUT = 12


def _front_kernel(grid_mode, n_chunks, seq_len, *refs):
    (cur_ref, prev_ref, next_ref, poolw_ref, pscale_ref, mu_ref,
     dw0_ref, dw2_ref, a0_ref, a2_ref, kk_ref, ka_ref, rk_ref, seg_ref) = refs[:N_FRONT_IN]
    (ypre_ref, gsig_ref, r_ref, v_ref, kkn_ref, bonus_ref,
     lwf_ref, lwb_ref, kdf_ref, kdb_ref, bf_ref, bb_ref) = refs[-N_FRONT_OUT:]
    c = pl.program_id(1)
    cur = cur_ref[...]
    prev = jnp.where(c > 0, prev_ref[...], 0.0)
    nxt = jnp.where(c < n_chunks - 1, next_ref[...], 0.0)
    ext = jnp.concatenate([prev, cur, nxt], axis=0)

    def down(x, s):
        return pltpu.roll(x, s, 0)

    def up(x, s):
        return pltpu.roll(x, EXT - s, 0)

    def mid(x):
        return x[HALO:HALO + ROWS]

    row = lax.broadcasted_iota(jnp.int32, (ROWS, 1), 0)
    t_seq = c * ROWS + row

    outs = []
    for gi, win in enumerate(POOL_WINDOWS):
        h = win // 2
        u = ext[:, gi * POOL_GD:(gi + 1) * POOL_GD]
        trail, lead, s = u, u, 1
        while s < h:
            trail = trail + down(trail, s)
            lead = lead + up(lead, s)
            s *= 2
        total = mid(down(trail, 1) + lead)
        cnt = (jnp.minimum(t_seq + h, seq_len) - jnp.maximum(t_seq - h, 0)).astype(F32)
        pooled = total / cnt - mid(u)
        outs.append(_dot(pooled.astype(BF16), poolw_ref[gi]))
    mixed = jnp.concatenate(outs, axis=1) * pscale_ref[...]
    ypre_ref[...] = mixed.astype(BF16)

    pe = ext[:, OFF_R:OFF_MERGE]
    p = mid(pe)
    lane = lax.broadcasted_iota(jnp.int32, (1, SHIFT_COLS), 1)
    if grid_mode:
        col = t_seq % GRID_W
        left = jnp.where(col > 0, mid(down(pe, 1)), 0.0)
        right = jnp.where(col < GRID_W - 1, mid(up(pe, 1)), 0.0)
        upn = pe[0:ROWS]
        dnn = pe[2 * HALO:2 * HALO + ROWS]
        q = lane % 4
        sh = jnp.where(q == 0, left, jnp.where(q == 1, right, jnp.where(q == 2, upn, dnn)))
    else:
        sh = jnp.where(lane % 2 == 0, mid(down(pe, 1)), mid(up(pe, 1)))
    pm = p + mu_ref[...] * (sh - p)

    r = pm[:, 0:RW]
    k = pm[:, RW:2 * RW]
    v = pm[:, 2 * RW:3 * RW]
    o = 3 * RW
    w_lo = jnp.tanh(pm[:, o:o + 2 * LORA])
    a_lo = pm[:, o + 2 * LORA:o + 4 * LORA]
    gsig_ref[...] = _sigmoid(pm[:, o + 4 * LORA:o + 4 * LORA + GATE_LORA]).astype(BF16)

    seg = seg_ref[...]
    kkr = k * kk_ref[...]
    ss = _dot_exact_rhs(kkr * kkr, seg, 2)
    kkn = kkr / jnp.maximum(jnp.sqrt(ss), 1e-12)
    r_ref[...] = r
    v_ref[...] = v
    kkn_ref[...] = kkn

    bonus = None
    for d, (lw_ref, kd_ref, b_ref) in enumerate(((lwf_ref, kdf_ref, bf_ref), (lwb_ref, kdb_ref, bb_ref))):
        z = dw0_ref[d:d + 1, :] + _dot3(w_lo, dw2_ref[d])
        lw_ref[...] = -math.exp(-0.5) * _sigmoid(z)
        a = _sigmoid(a0_ref[d:d + 1, :] + _dot3(a_lo, a2_ref[d]))
        kd = k * (1.0 + (a - 1.0) * ka_ref[...])
        kd_ref[...] = kd
        b_ref[...] = kkn * a
        bd = _dot_exact_rhs(r * kd * rk_ref[...], seg, 2) * v
        bonus = bd if bonus is None else bonus + bd
    bonus_ref[...] = bonus


def _front(proj, grid_mode, n_seq, seq_len, tok0, weights):
    n_chunks = seq_len // ROWS
    n_tok = n_seq * seq_len
    blk0 = tok0 // ROWS
    hpb = ROWS // HALO
    last_halo = N_TOK // HALO - 1

    def cur_map(b, c):
        return (blk0 + b * n_chunks + c, 0)

    def prev_map(b, c):
        return (jnp.maximum((blk0 + b * n_chunks + c) * hpb - 1, 0), 0)

    def next_map(b, c):
        return (jnp.minimum((blk0 + b * n_chunks + c + 1) * hpb, last_halo), 0)

    def full(a):
        nd = a.ndim
        return pl.BlockSpec(a.shape, lambda b, c, _nd=nd: (0,) * _nd)

    def out_map(b, c):
        return (b * n_chunks + c, 0)

    f32_out = jax.ShapeDtypeStruct((n_tok, RW), F32)
    out_shape = [jax.ShapeDtypeStruct((n_tok, POOL_W), BF16), jax.ShapeDtypeStruct((n_tok, GATE_LORA), BF16)]
    out_shape += [f32_out] * (N_FRONT_OUT - 2)
    out_specs = [pl.BlockSpec((ROWS, POOL_W), out_map), pl.BlockSpec((ROWS, GATE_LORA), out_map)]
    out_specs += [pl.BlockSpec((ROWS, RW), out_map)] * (N_FRONT_OUT - 2)
    assert len(weights) == N_FRONT_IN - 3
    return pl.pallas_call(
        functools.partial(_front_kernel, grid_mode, n_chunks, seq_len),
        out_shape=out_shape,
        grid=(n_seq, n_chunks),
        in_specs=[
            pl.BlockSpec((ROWS, OFF_MERGE), cur_map),
            pl.BlockSpec((HALO, OFF_MERGE), prev_map),
            pl.BlockSpec((HALO, OFF_MERGE), next_map),
        ] + [full(a) for a in weights],
        out_specs=out_specs,
        compiler_params=pltpu.CompilerParams(dimension_semantics=("arbitrary", "arbitrary"),
                                             vmem_limit_bytes=VMEM_LIMIT),
        name="front_grid" if grid_mode else "front_ctx",
    )(proj, proj, proj, *weights)


def _scan_kernel(n_chunks, seqs_per_step, has_init, *refs):
    (r_ref, v_ref, kk_ref, bonus_ref, lwf_ref, lwb_ref, kdf_ref, kdb_ref, bf_ref, bb_ref) = refs[:10]
    if has_init:
        s0f_ref, s0b_ref = refs[10:12]
    y_ref, sf_ref, sb_ref = refs[-3:]
    C = CHUNK
    P2 = 2 * C
    ri = lax.broadcasted_iota(jnp.int32, (P2, P2), 0)
    ci = lax.broadcasted_iota(jnp.int32, (P2, P2), 1)
    same = (ri // C) == (ci // C)
    rp, cp = ri % C, ci % C
    eye = (ri == ci).astype(F32)
    blk = {}
    s = INV_BLOCK
    while s <= C:
        blk[s] = (ri // s) == (ci // s)
        s *= 2
    ti = lax.broadcasted_iota(jnp.int32, (C, C), 0)
    tj = lax.broadcasted_iota(jnp.int32, (C, C), 1)
    head_a = lax.broadcasted_iota(jnp.int32, (C, LANES), 1) < HEAD

    def stack(x):
        return jnp.concatenate([jnp.where(head_a, x, 0.0), jnp.where(head_a, 0.0, x)], axis=0)

    def d3(a, b, dims=_NN):
        return _dot(a[0], b[0], dims) + (_dot(a[0], b[1], dims) + _dot(a[1], b[0], dims))

    def chunks(chains):
        n = range(len(chains))
        masks, sls = [], []
        for (s_ref, q, p, row0, lw_ref, kd_ref, b_ref, reverse) in chains:
            if reverse:
                masks.append((same & (cp > rp), same & (cp >= rp), (tj >= ti), 0))
            else:
                masks.append((same & (cp < rp), same & (cp <= rp), (tj <= ti), C - 1))
            sls.append((pl.ds(row0, C), slice(p * LANES, (p + 1) * LANES)))
        lw = [chains[i][4][sls[i]] for i in n]
        cum = [_dot_exact_lhs(masks[i][2].astype(BF16), lw[i], 3) for i in n]
        e_i = [jnp.exp(cum[i]) for i in n]
        e_n = [jnp.exp(-cum[i]) for i in n]
        QR = [jnp.concatenate([stack(kk_ref[sls[i]] * jnp.exp(cum[i] - lw[i])),
                               stack(r_ref[sls[i]] * e_i[i])], axis=0).astype(BF16) for i in n]
        BK = [jnp.concatenate([stack(chains[i][6][sls[i]] * e_n[i]),
                               stack(chains[i][5][sls[i]] * e_n[i])], axis=0).astype(BF16) for i in n]
        v_st = [stack(v_ref[sls[i]]) for i in n]
        Vs = [v_st[i].astype(BF16) for i in n]
        S = [chains[i][0][chains[i][1], chains[i][2]] for i in n]
        G = [_dot(QR[i], BK[i], _NT) for i in n]
        L = [jnp.where(masks[i][0], G[i][0:P2, 0:P2], 0.0) for i in n]
        Lk = [jnp.where(masks[i][0], G[i][0:P2, P2:2 * P2], 0.0) for i in n]
        RBK = [jnp.concatenate([jnp.where(masks[i][1], G[i][P2:2 * P2, 0:P2], 0.0),
                                jnp.where(masks[i][1], G[i][P2:2 * P2, P2:2 * P2], 0.0)], axis=1) for i in n]
        L0 = [_split2(jnp.where(blk[INV_BLOCK], L[i], 0.0)) for i in n]
        X = [eye - jnp.where(blk[INV_BLOCK], L[i], 0.0) for i in n]
        Pw = L0
        s = 2
        while s < INV_BLOCK:
            Pw = [_split2(d3(Pw[i], Pw[i])) for i in n]
            X = [X[i] + d3(_split2(X[i]), Pw[i]) for i in n]
            s *= 2
        s = INV_BLOCK
        while s < C:
            Dm = [_split2(X[i]) for i in n]
            E = [d3(_split2(jnp.where(blk[2 * s] & jnp.logical_not(blk[s]), L[i], 0.0)), Dm[i]) for i in n]
            X = [X[i] - d3(Dm[i], _split2(E[i])) for i in n]
            s *= 2
        QRA = [_dot(QR[i], S[i].astype(BF16), _NT) for i in n]
        W = [QRA[i][0:P2] + _dot(Lk[i].astype(BF16), Vs[i]) for i in n]
        U = [-_dot(X[i].astype(BF16), W[i].astype(BF16)) for i in n]
        UV = [jnp.concatenate([U[i], v_st[i]], axis=0) for i in n]
        Ys = [QRA[i][P2:2 * P2] + _dot(RBK[i].astype(BF16), UV[i].astype(BF16)) for i in n]
        dS = [_dot(UV[i].T.astype(BF16), BK[i]) for i in n]
        for i in n:
            last = masks[i][3]
            chains[i][0][chains[i][1], chains[i][2]] = (S[i] + dS[i]) * e_i[i][last:last + 1, :]
            y_ref[sls[i]] += Ys[i][0:C] + Ys[i][C:P2]

    y_ref[...] = bonus_ref[...]
    if has_init:
        sf_ref[...] = s0f_ref[...]
        sb_ref[...] = s0b_ref[...]
    else:
        sf_ref[...] = jnp.zeros_like(sf_ref)
        sb_ref[...] = jnp.zeros_like(sb_ref)

    seq_len = n_chunks * C

    def body(c, carry):
        chains = []
        for q in range(seqs_per_step):
            rf = pl.multiple_of(q * seq_len + c * C, C)
            rb = pl.multiple_of(q * seq_len + (n_chunks - 1 - c) * C, C)
            for p in range(N_PAIR):
                chains.append((sf_ref, q, p, rf, lwf_ref, kdf_ref, bf_ref, False))
                chains.append((sb_ref, q, p, rb, lwb_ref, kdb_ref, bb_ref, True))
        chunks(chains)
        return carry

    lax.fori_loop(0, n_chunks, body, 0)


def _scan(arrs, n_seq, seq_len, init):
    n_tok = n_seq * seq_len
    n_chunks = seq_len // CHUNK
    sps = max(1, SCAN_TOKENS_PER_STEP // seq_len)
    tok_spec = pl.BlockSpec((sps * seq_len, RW), lambda b: (b, 0))
    st_spec = pl.BlockSpec((sps, N_PAIR, LANES, LANES), lambda b: (b, 0, 0, 0))
    st_shape = jax.ShapeDtypeStruct((n_seq, N_PAIR, LANES, LANES), F32)
    ins = list(arrs)
    in_specs = [tok_spec] * 10
    if init is not None:
        ins += list(init)
        in_specs += [st_spec, st_spec]
    return pl.pallas_call(
        functools.partial(_scan_kernel, n_chunks, sps, init is not None),
        out_shape=[jax.ShapeDtypeStruct((n_tok, RW), F32), st_shape, st_shape],
        grid=(n_seq // sps,),
        in_specs=in_specs,
        out_specs=[tok_spec, st_spec, st_spec],
        compiler_params=pltpu.CompilerParams(dimension_semantics=("arbitrary",), vmem_limit_bytes=VMEM_LIMIT),
        name="scan_init" if init is not None else "scan_zero",
    )(*ins)


def _post_kernel(xc_ref, xl_ref, mod_ref, yprec_ref, yprel_ref, gsigc_ref, gsigl_ref, yc_ref, yl_ref,
                 nmw_ref, wm_ref, bm_ref, wpo_ref, seg_ref, lnw_ref, lnb_ref, gw2_ref, wro_ref, wo_ref,
                 nfw_ref, rw_ref, rb_ref,
                 x1_ref, hn2_ref, code_ref, w4_ref, meta_ref, texp_ref, carry_ref):
    i = pl.program_id(0)

    @pl.when(i == 0)
    def _():
        carry_ref[...] = jnp.zeros_like(carry_ref)

    x = _path_tile(xc_ref, xl_ref, TM_POST)
    mod = mod_ref[0]
    shift1, scale1, gate1 = mod[:, 0:D], mod[:, D:2 * D], mod[:, 2 * D:3 * D]
    shift2, scale2 = mod[:, 3 * D:4 * D], mod[:, 4 * D:5 * D]
    hn = _modnorm(x, nmw_ref[...], scale1, shift1)
    merge = _sigmoid(_dot(hn.astype(BF16), wm_ref[...]) + bm_ref[...])
    y_pool = _dot(_path_tile(yprec_ref, yprel_ref, TM_POST), wpo_ref[...])

    y = _path_tile(yc_ref, yl_ref, TM_POST)
    seg = seg_ref[...]
    mu = _dot_exact_rhs(y, seg, 2) * (1.0 / HEAD)
    yc = y - mu
    var = _dot_exact_rhs(yc * yc, seg, 2) * (1.0 / HEAD)
    yn = yc * lax.rsqrt(var + GN_EPS) * lnw_ref[...] + lnb_ref[...]
    g = _dot(_path_tile(gsigc_ref, gsigl_ref, TM_POST), gw2_ref[...])
    y_rwkv = _dot((yn * g).astype(BF16), wro_ref[...])

    mixed = merge[:, 0:D] * y_pool + merge[:, D:2 * D] * y_rwkv
    x1 = x + gate1 * _dot(mixed.astype(BF16), wo_ref[...])
    x1_ref[...] = x1
    hn2 = _modnorm(x1, nfw_ref[...], scale2, shift2)
    hn2_ref[...] = hn2

    logits = _dot3(hn2, rw_ref[...]) + rb_ref[...]
    lane = lax.broadcasted_iota(jnp.int32, logits.shape, 1)
    work = logits
    sel = None
    top = None
    for j in range(TOP_K):
        m = jnp.max(work, axis=-1, keepdims=True)
        if j == 0:
            top = m
        idx = jnp.min(jnp.where(work == m, lane, LANES), axis=-1, keepdims=True)
        pick = lane == idx
        sel = pick if sel is None else (sel | pick)
        work = jnp.where(pick, -jnp.inf, work)
    e = jnp.where(sel, jnp.exp(logits - top), 0.0)
    comb = e / jnp.sum(e, axis=-1, keepdims=True)

    tm = logits.shape[0]
    sel_b = jnp.where(sel, 1.0, 0.0).astype(BF16)
    before = (lax.broadcasted_iota(jnp.int32, (tm, tm), 1) < lax.broadcasted_iota(jnp.int32, (tm, tm), 0))
    rank = carry_ref[...] + _dot(before.astype(BF16), sel_b)
    carry = carry_ref[...] + jnp.sum(sel_b.astype(F32), axis=0, keepdims=True)
    carry_ref[...] = carry
    lower_e = (lax.broadcasted_iota(jnp.int32, (LANES, LANES), 0) < lax.broadcasted_iota(jnp.int32, (LANES, LANES), 1))
    rowpos = _dot(sel_b, lower_e.astype(BF16))
    code = jnp.zeros(logits.shape, jnp.int32)
    w4 = jnp.zeros(logits.shape, F32)
    for k in range(TOP_K):
        mk = sel & (rowpos == float(k))
        ek = jnp.sum(jnp.where(mk, lane, 0), axis=-1, keepdims=True)
        rk = jnp.sum(jnp.where(mk, rank, 0.0), axis=-1, keepdims=True).astype(jnp.int32)
        wk = jnp.sum(jnp.where(mk, comb, 0.0), axis=-1, keepdims=True)
        code = jnp.where(lane == k, ek * (1 << RANK_BITS) + rk, code)
        w4 = jnp.where(lane == k, wk, w4)
    code_ref[...] = code
    w4_ref[...] = w4

    tiles = jnp.floor((carry + (TM_SLOT - 1)) * (1.0 / TM_SLOT))
    incl = (lax.broadcasted_iota(jnp.int32, (LANES, LANES), 0) <= lax.broadcasted_iota(jnp.int32, (LANES, LANES), 1))
    tiles8 = jnp.broadcast_to(tiles, (8, LANES)).astype(BF16)
    cum = _dot(tiles8, incl.astype(BF16))[0:1]
    lane1 = lax.broadcasted_iota(jnp.int32, (1, LANES), 1)
    offs = ((cum - tiles) * TM_SLOT).astype(jnp.int32)
    n_used = jnp.sum(jnp.where(lane1 == N_EXP - 1, cum, 0.0), axis=-1, keepdims=True).astype(jnp.int32)
    meta_ref[...] = jnp.where(lane1 == META_NT, n_used, jnp.where(lane1 < N_EXP, offs, 0))
    tile_id = lax.broadcasted_iota(jnp.int32, (N_SLOT_TILES, LANES), 0).astype(F32)
    done = jnp.where((lax.broadcasted_iota(jnp.int32, (N_SLOT_TILES, LANES), 1) < N_EXP) & (cum <= tile_id), 1, 0)
    texp = jnp.minimum(jnp.sum(done, axis=-1, keepdims=True), N_EXP - 1)
    texp_ref[...] = jnp.broadcast_to(texp, (N_SLOT_TILES, LANES))


def _post(x, mod3, ypre, gsig, y, weights):
    def tile(n, dt):
        return pl.BlockSpec((TM_POST, n), lambda i: (i, 0))

    def full(a):
        nd = a.ndim
        return pl.BlockSpec(a.shape, lambda i, _nd=nd: (0,) * _nd)

    def const(shape):
        return pl.BlockSpec(shape, lambda i: (0, 0))

    return pl.pallas_call(
        _post_kernel,
        out_shape=[jax.ShapeDtypeStruct((N_TOK, D), F32), jax.ShapeDtypeStruct((N_TOK, D), F32),
                   jax.ShapeDtypeStruct((N_TOK, LANES), jnp.int32), jax.ShapeDtypeStruct((N_TOK, LANES), F32),
                   jax.ShapeDtypeStruct((1, LANES), jnp.int32),
                   jax.ShapeDtypeStruct((N_SLOT_TILES, LANES), jnp.int32)],
        grid=(N_TOK // TM_POST,),
        in_specs=_path_specs(TM_POST) + [
            pl.BlockSpec((1, 1, N_MOD * D), lambda i: (_mod_row(i, TM_POST), 0, 0)),
        ] + _path_specs(TM_POST, POOL_W) + _path_specs(TM_POST, GATE_LORA) + _path_specs(TM_POST, RW)
        + [full(a) for a in weights],
        out_specs=[tile(D, F32), tile(D, F32), tile(LANES, jnp.int32), tile(LANES, F32),
                   const((1, LANES)), const((N_SLOT_TILES, LANES))],
        scratch_shapes=[pltpu.VMEM((1, LANES), F32)],
        compiler_params=pltpu.CompilerParams(dimension_semantics=("arbitrary",), vmem_limit_bytes=VMEM_LIMIT),
        name="post",
    )(*x, mod3, *ypre, *gsig, *y, *weights)


def _slot_of(code, meta_ref):
    return meta_ref[code >> RANK_BITS] + (code & ((1 << RANK_BITS) - 1))


def _load_codes(code_ref, code_smem, sem):
    cp = pltpu.make_async_copy(code_ref.at[pl.program_id(0)], code_smem, sem)
    cp.start()
    cp.wait()


def _dispatch_kernel(code_ref, meta_ref, hn2_ref, xs_ref, code_smem, zero_buf, sem_c, sem_z, sem):
    @pl.when(pl.program_id(0) == 0)
    def _():
        zero_buf[...] = jnp.zeros_like(zero_buf)

        def pad_tile(e):
            end = meta_ref[e + 1] if e + 1 < N_EXP else meta_ref[META_NT] * TM_SLOT
            start = pl.multiple_of(end - TM_SLOT, TM_SLOT)
            return end > meta_ref[e], pltpu.make_async_copy(zero_buf, xs_ref.at[pl.ds(start, TM_SLOT)], sem_z)

        def tail_tile(j):
            return pltpu.make_async_copy(zero_buf, xs_ref.at[pl.ds(pl.multiple_of(j * TM_SLOT, TM_SLOT), TM_SLOT)], sem_z)

        n_used = meta_ref[META_NT]
        for e in range(N_EXP):
            nonempty, cp = pad_tile(e)
            pl.when(nonempty)(cp.start)

        def start_tail(j, c):
            tail_tile(j).start()
            return c

        def wait_tail(j, c):
            tail_tile(j).wait()
            return c

        lax.fori_loop(n_used, N_SLOT_TILES, start_tail, 0)
        for e in range(N_EXP):
            nonempty, cp = pad_tile(e)
            pl.when(nonempty)(cp.wait)
        lax.fori_loop(n_used, N_SLOT_TILES, wait_tail, 0)

    _load_codes(code_ref, code_smem, sem_c)

    def body(t, carry):
        for k in range(TOP_K):
            slot = _slot_of(code_smem[t * TOP_K + k], meta_ref)
            pltpu.make_async_copy(hn2_ref.at[pl.ds(t, 1)], xs_ref.at[pl.ds(slot, 1)], sem).start(priority=k % 2)
        return carry

    lax.fori_loop(0, TM_DISP, body, 0, unroll=8)
    for _ in range(TOP_K):
        pltpu.make_async_copy(hn2_ref, xs_ref.at[pl.ds(0, TM_DISP)], sem).wait()


def _dispatch(code2d, meta, hn2):
    n_steps = N_TOK // TM_DISP
    return pl.pallas_call(
        _dispatch_kernel,
        out_shape=jax.ShapeDtypeStruct((N_SLOTS, D), F32),
        grid=(n_steps,),
        in_specs=[
            pl.BlockSpec(code2d.shape, lambda i: (0, 0)),
            pl.BlockSpec(memory_space=pltpu.SMEM),
            pl.BlockSpec((TM_DISP, D), lambda i: (i, 0)),
        ],
        out_specs=pl.BlockSpec(memory_space=pl.ANY),
        scratch_shapes=[pltpu.SMEM((TM_DISP * TOP_K,), jnp.int32), pltpu.VMEM((TM_SLOT, D), F32),
                        pltpu.SemaphoreType.DMA, pltpu.SemaphoreType.DMA, pltpu.SemaphoreType.DMA],
        compiler_params=pltpu.CompilerParams(dimension_semantics=("arbitrary",), vmem_limit_bytes=VMEM_LIMIT),
        name="dispatch",
    )(code2d, meta, hn2)


def _ffn_kernel(texp_ref, meta_ref, xs_ref, wgu_ref, bgu_ref, wd_ref, bd_ref, ys_ref, wgu_bf, wd_bf):
    i = pl.program_id(0)
    valid = i < meta_ref[META_NT]
    fresh = (i == 0) | (texp_ref[i] != texp_ref[jnp.maximum(i - 1, 0)])

    @pl.when(valid & fresh)
    def _():
        wgu_bf[...] = wgu_ref[0].astype(BF16)
        wd_bf[...] = wd_ref[0].astype(BF16)

    @pl.when(valid)
    def _():
        gu = _dot(xs_ref[...].astype(BF16), wgu_bf[...]) + bgu_ref[0]
        glu = jnp.minimum(gu[:, 0:D_FF], SWIGLU_LIMIT)
        lin = jnp.clip(gu[:, D_FF:2 * D_FF], -SWIGLU_LIMIT, SWIGLU_LIMIT)
        act = glu * _sigmoid(SWIGLU_ALPHA * glu) * (lin + 1.0)
        ys_ref[...] = _dot(act.astype(BF16), wd_bf[...]) + bd_ref[0]

    @pl.when(jnp.logical_not(valid))
    def _():
        ys_ref[...] = jnp.zeros_like(ys_ref)


def _ffn(texp, meta, xs, wgu, bgu, wd, bd):
    def used_tile(i, te, me):
        return (jnp.minimum(i, me[META_NT] - 1), 0)

    return pl.pallas_call(
        _ffn_kernel,
        out_shape=jax.ShapeDtypeStruct((N_SLOTS, D), F32),
        grid_spec=pltpu.PrefetchScalarGridSpec(
            num_scalar_prefetch=2,
            grid=(N_SLOT_TILES,),
            in_specs=[
                pl.BlockSpec((TM_SLOT, D), used_tile),
                pl.BlockSpec((1, D, 2 * D_FF), lambda i, te, me: (te[i], 0, 0)),
                pl.BlockSpec((1, 1, 2 * D_FF), lambda i, te, me: (te[i], 0, 0)),
                pl.BlockSpec((1, D_FF, D), lambda i, te, me: (te[i], 0, 0)),
                pl.BlockSpec((1, 1, D), lambda i, te, me: (te[i], 0, 0)),
            ],
            out_specs=pl.BlockSpec((TM_SLOT, D), lambda i, te, me: (i, 0)),
            scratch_shapes=[pltpu.VMEM((D, 2 * D_FF), BF16), pltpu.VMEM((D_FF, D), BF16)],
        ),
        compiler_params=pltpu.CompilerParams(dimension_semantics=("arbitrary",), vmem_limit_bytes=VMEM_LIMIT),
        name="ffn",
    )(texp, meta, xs, wgu, bgu, wd, bd)


def _combine_kernel(code_ref, meta_ref, x1_ref, w4_ref, mod_ref, nfw_ref, ys_ref, oc_ref, ol_ref,
                    code_smem, buf, sem_c, sem):
    _load_codes(code_ref, code_smem, sem_c)

    def body(t, carry):
        for k in range(TOP_K):
            slot = _slot_of(code_smem[t * TOP_K + k], meta_ref)
            pltpu.make_async_copy(ys_ref.at[pl.ds(slot, 1)], buf.at[k, pl.ds(t, 1)], sem).start(priority=k % 2)
        return carry

    lax.fori_loop(0, TM_DISP, body, 0, unroll=8)
    for k in range(TOP_K):
        pltpu.make_async_copy(ys_ref.at[pl.ds(0, TM_DISP)], buf.at[k], sem).wait()
    w4 = w4_ref[...]
    moe = w4[:, 0:1] * buf[0]
    for k in range(1, TOP_K):
        moe = moe + w4[:, k:k + 1] * buf[k]
    gate2 = mod_ref[0][:, 5 * D:6 * D]
    x2 = x1_ref[...] + gate2 * moe
    ms = jnp.mean(x2 * x2, axis=-1, keepdims=True)
    out = x2 * lax.rsqrt(ms + RMS_EPS) * nfw_ref[...]
    is_ctx = pl.program_id(0) < N_CTX_TOK // TM_DISP

    @pl.when(is_ctx)
    def _():
        oc_ref[...] = out

    @pl.when(jnp.logical_not(is_ctx))
    def _():
        ol_ref[...] = out


def _combine(code2d, meta, x1, w4, mod3, nfw, ys):
    n_steps = N_TOK // TM_DISP
    n_ctx = N_CTX_TOK // TM_DISP
    return pl.pallas_call(
        _combine_kernel,
        out_shape=[jax.ShapeDtypeStruct((N_CTX_TOK, D), F32), jax.ShapeDtypeStruct((N_LAT_TOK, D), F32)],
        grid=(n_steps,),
        in_specs=[
            pl.BlockSpec(code2d.shape, lambda i: (0, 0)),
            pl.BlockSpec(memory_space=pltpu.SMEM),
            pl.BlockSpec((TM_DISP, D), lambda i: (i, 0)),
            pl.BlockSpec((TM_DISP, LANES), lambda i: (i, 0)),
            pl.BlockSpec((1, 1, N_MOD * D), lambda i: (_mod_row(i, TM_DISP), 0, 0)),
            pl.BlockSpec((1, D), lambda i: (0, 0)),
            pl.BlockSpec(memory_space=pl.ANY),
        ],
        out_specs=[pl.BlockSpec((TM_DISP, D), lambda i: (jnp.minimum(i, n_ctx - 1), 0)),
                   pl.BlockSpec((TM_DISP, D), lambda i: (jnp.maximum(i - n_ctx, 0), 0))],
        scratch_shapes=[pltpu.SMEM((TM_DISP * TOP_K,), jnp.int32), pltpu.VMEM((TOP_K, TM_DISP, D), F32),
                        pltpu.SemaphoreType.DMA, pltpu.SemaphoreType.DMA],
        compiler_params=pltpu.CompilerParams(dimension_semantics=("arbitrary",), vmem_limit_bytes=VMEM_LIMIT),
        name="combine",
    )(code2d, meta, x1, w4, mod3, nfw, ys)


def _pair_blockdiag(s):
    b = s.shape[0]
    s = s.reshape(b, N_PAIR, 2, HEAD, HEAD)
    z = jnp.zeros((b, N_PAIR, HEAD, HEAD), s.dtype)
    top = jnp.concatenate([s[:, :, 0], z], axis=-1)
    bot = jnp.concatenate([z, s[:, :, 1]], axis=-1)
    return jnp.concatenate([top, bot], axis=-2)


def _pair_unblock(s):
    b = s.shape[0]
    a = s[:, :, 0:HEAD, 0:HEAD]
    c = s[:, :, HEAD:, HEAD:]
    return jnp.stack([a, c], axis=2).reshape(b, 1, N_HEAD, HEAD, HEAD)


def kernel(x_prompt, x_sample, state_fwd, state_bwd, c, c_ctx, w_mod, b_mod, norm_mix_w, w_in, b_merge, pool_w, pool_scale, w_pool_out, shift_mu, decay_w0, decay_w2, iclr_a0, iclr_a2, gate_w2, k_k, k_a, r_k, ln_x_w, ln_x_b, w_rwkv_out, w_o, norm_ffn_w, router_w, router_b, expert_w_gu, expert_b_gu, expert_w_down, expert_b_down, norm_final_w):
    l = 0
    x_ctx = x_prompt.reshape(N_CTX_TOK, D)
    x_lat = x_sample.reshape(N_LAT_TOK, D)
    cvec = jnp.concatenate([c_ctx[None, :], c, jnp.zeros((MOD_ROWS - 1 - N_LAT_SEQ, D), F32)], axis=0)
    mod = _modulation(cvec, w_mod[l], b_mod[l][None, :])
    mod3 = mod.reshape(MOD_ROWS, 1, N_MOD * D)

    w_in_l = w_in[l]
    proj = _projection(x_ctx, x_lat, mod3, norm_mix_w[l][None, :], w_in_l[:, :OFF_MERGE].astype(BF16))

    zl = jnp.zeros((LORA, RW), F32)
    dw2 = jnp.stack([jnp.concatenate([decay_w2[l, 0], zl], 0), jnp.concatenate([zl, decay_w2[l, 1]], 0)])
    a2 = jnp.stack([jnp.concatenate([iclr_a2[l, 0], zl], 0), jnp.concatenate([zl, iclr_a2[l, 1]], 0)])
    hid = jnp.arange(RW) // HEAD
    seg = (hid[:, None] == hid[None, :]).astype(BF16)
    front_w = (pool_w[l].astype(BF16), pool_scale[l][None, :], shift_mu[l][None, :],
               decay_w0[l], dw2, iclr_a0[l], a2, k_k[l][None, :], k_a[l][None, :],
               r_k[l].reshape(1, RW), seg)
    f_ctx = _front(proj, False, N_CTX_SEQ, T_CTX, 0, front_w)
    f_lat = _front(proj, True, N_LAT_SEQ, T_LAT, N_CTX_TOK, front_w)

    y_ctx, sf, sb = _scan(f_ctx[2:], N_CTX_SEQ, T_CTX, None)
    init = (_pair_blockdiag(state_fwd[:, l]), _pair_blockdiag(state_bwd[:, l]))
    y_lat, _, _ = _scan(f_lat[2:], N_LAT_SEQ, T_LAT, init)

    rw_pad = jnp.concatenate([router_w[l], jnp.zeros((D, LANES - N_EXP), F32)], axis=1)
    rb_pad = jnp.concatenate([router_b[l], jnp.full((LANES - N_EXP,), -1e30, F32)])[None, :]
    post_w = (norm_mix_w[l][None, :], w_in_l[:, OFF_MERGE:].astype(BF16), b_merge[l][None, :],
              w_pool_out[l].astype(BF16), seg, ln_x_w[l][None, :], ln_x_b[l][None, :],
              gate_w2[l].astype(BF16), w_rwkv_out[l].astype(BF16), w_o[l].astype(BF16),
              norm_ffn_w[l][None, :], rw_pad, rb_pad)
    x1, hn2, code, w4, meta, texp = _post((x_ctx, x_lat), mod3, (f_ctx[0], f_lat[0]), (f_ctx[1], f_lat[1]),
                                          (y_ctx, y_lat), post_w)

    code2d = code[:, :TOP_K].reshape(N_TOK // TM_DISP, TM_DISP * TOP_K)
    meta1 = meta.reshape(LANES)
    xs = _dispatch(code2d, meta1, hn2)
    ys = _ffn(texp[:, 0], meta1, xs, expert_w_gu[l], expert_b_gu[l][:, None, :],
              expert_w_down[l], expert_b_down[l][:, None, :])
    out_ctx, out_lat = _combine(code2d, meta1, x1, w4, mod3, norm_final_w[None, :], ys)
    y_prompt = out_ctx.reshape(N_CTX_SEQ, T_CTX, D)
    y_sample = out_lat.reshape(N_LAT_SEQ, T_LAT, D)
    return (y_prompt, y_sample, _pair_unblock(sf), _pair_unblock(sb))
```

```python
import functools
import math

import jax
import jax.numpy as jnp
from jax import lax
from jax.experimental import pallas as pl
from jax.experimental.pallas import tpu as pltpu

F32 = jnp.float32
BF16 = jnp.bfloat16

D = 1024
N_CTX_SEQ, T_CTX = 32, 256
N_LAT_SEQ, T_LAT = 4, 1024
GRID_W = 64
N_CTX_TOK = N_CTX_SEQ * T_CTX
N_LAT_TOK = N_LAT_SEQ * T_LAT
N_TOK = N_CTX_TOK + N_LAT_TOK

POOL_W = 512
POOL_G = 4
POOL_GD = POOL_W // POOL_G
POOL_WINDOWS = (2, 4, 8, 16)
RW = 512
HEAD = 64
N_HEAD = RW // HEAD
N_PAIR = N_HEAD // 2
LORA = 64
GATE_LORA = 128
N_EXP = 32
TOP_K = 4
D_FF = 1024
SWIGLU_ALPHA = 1.702
SWIGLU_LIMIT = 7.0
RMS_EPS = 1e-6
GN_EPS = 1e-5 * HEAD
N_MOD = 6

OFF_R = POOL_W
SHIFT_COLS = 3 * RW + 4 * LORA + GATE_LORA
OFF_MERGE = OFF_R + SHIFT_COLS
IN_COLS = OFF_MERGE + 2 * D

LANES = 128
CHUNK = 64
INV_BLOCK = 8
SCAN_TOKENS_PER_STEP = 512
ROWS = 256
HALO = 64
EXT = ROWS + 2 * HALO
TM_PROJ = 512
TM_POST = 256
TM_DISP = 256
TM_SLOT = 512
N_SLOT_TILES = N_TOK * TOP_K // TM_SLOT + N_EXP
N_SLOTS = N_SLOT_TILES * TM_SLOT
RANK_BITS = 14
META_NT = N_EXP
MOD_ROWS = 8
VMEM_LIMIT = 56 * 1024 * 1024


def _sigmoid(x):
    return 1.0 / (1.0 + jnp.exp(-x))


def _split2(a):
    hi = a.astype(BF16)
    lo = (a - hi.astype(F32)).astype(BF16)
    return hi, lo


_NN = (((1,), (0,)), ((), ()))
_NT = (((1,), (1,)), ((), ()))


def _dot(a, b, dims=_NN):
    return lax.dot_general(a, b, dims, preferred_element_type=F32)


def _dot3(a, b, dims=_NN):
    ah, al = _split2(a)
    bh, bl = _split2(b)
    return _dot(ah, bh, dims) + (_dot(ah, bl, dims) + _dot(al, bh, dims))


def _dot_exact_lhs(a_bf16, b, passes):
    acc = None
    rem = b
    for _ in range(passes):
        part = rem.astype(BF16)
        term = _dot(a_bf16, part)
        acc = term if acc is None else acc + term
        rem = rem - part.astype(F32)
    return acc


def _dot_exact_rhs(a, b_bf16, passes):
    acc = None
    rem = a
    for _ in range(passes):
        part = rem.astype(BF16)
        term = _dot(part, b_bf16)
        acc = term if acc is None else acc + term
        rem = rem - part.astype(F32)
    return acc


def _modnorm(x, w, scale, shift):
    ms = jnp.mean(x * x, axis=-1, keepdims=True)
    return x * lax.rsqrt(ms + RMS_EPS) * w * (1.0 + scale) + shift


def _mod_row(i, tm):
    n_ctx = N_CTX_TOK // tm
    per = T_LAT // tm
    return jnp.where(i < n_ctx, 0, 1 + (i - n_ctx) // per)


def _mod_kernel(c_ref, w_ref, b_ref, o_ref):
    c = c_ref[...]
    s = c * _sigmoid(c)
    o_ref[...] = _dot3(s, w_ref[...]) + b_ref[...]


def _modulation(cvec, w_mod, b_mod):
    return pl.pallas_call(
        _mod_kernel,
        out_shape=jax.ShapeDtypeStruct((MOD_ROWS, N_MOD * D), F32),
        grid=(N_MOD,),
        in_specs=[
            pl.BlockSpec((MOD_ROWS, D), lambda j: (0, 0)),
            pl.BlockSpec((D, D), lambda j: (0, j)),
            pl.BlockSpec((1, D), lambda j: (0, j)),
        ],
        out_specs=pl.BlockSpec((MOD_ROWS, D), lambda j: (0, j)),
        compiler_params=pltpu.CompilerParams(dimension_semantics=("arbitrary",), vmem_limit_bytes=VMEM_LIMIT),
        name="mod",
    )(cvec, w_mod, b_mod)


def _path_specs(tm, n_cols=D):
    n_ctx = N_CTX_TOK // tm
    return [pl.BlockSpec((tm, n_cols), lambda i: (jnp.minimum(i, n_ctx - 1), 0)),
            pl.BlockSpec((tm, n_cols), lambda i: (jnp.maximum(i - n_ctx, 0), 0))]


def _path_tile(c_ref, l_ref, tm):
    return jnp.where(pl.program_id(0) < N_CTX_TOK // tm, c_ref[...], l_ref[...])


def _proj_kernel(xc_ref, xl_ref, mod_ref, nw_ref, w_ref, o_ref):
    mod = mod_ref[0]
    hn = _modnorm(_path_tile(xc_ref, xl_ref, TM_PROJ), nw_ref[...], mod[:, D:2 * D], mod[:, 0:D])
    o_ref[...] = _dot(hn.astype(BF16), w_ref[...])


def _projection(x_ctx, x_lat, mod3, norm_w, w_a):
    n_cols = w_a.shape[1]
    return pl.pallas_call(
        _proj_kernel,
        out_shape=jax.ShapeDtypeStruct((N_TOK, n_cols), F32),
        grid=(N_TOK // TM_PROJ,),
        in_specs=_path_specs(TM_PROJ) + [
            pl.BlockSpec((1, 1, N_MOD * D), lambda i: (_mod_row(i, TM_PROJ), 0, 0)),
            pl.BlockSpec((1, D), lambda i: (0, 0)),
            pl.BlockSpec((D, n_cols), lambda i: (0, 0)),
        ],
        out_specs=pl.BlockSpec((TM_PROJ, n_cols), lambda i: (i, 0)),
        compiler_params=pltpu.CompilerParams(dimension_semantics=("arbitrary",), vmem_limit_bytes=VMEM_LIMIT),
        name="proj",
    )(x_ctx, x_lat, mod3, norm_w, w_a)


N_FRONT_IN = 14
N_FRONT_OUT = 12


def _front_kernel(grid_mode, n_chunks, seq_len, *refs):
    (cur_ref, prev_ref, next_ref, poolw_ref, pscale_ref, mu_ref,
     dw0_ref, dw2_ref, a0_ref, a2_ref, kk_ref, ka_ref, rk_ref, seg_ref) = refs[:N_FRONT_IN]
    (ypre_ref, gsig_ref, r_ref, v_ref, kkn_ref, bonus_ref,
     lwf_ref, lwb_ref, kdf_ref, kdb_ref, bf_ref, bb_ref) = refs[-N_FRONT_OUT:]
    c = pl.program_id(1)
    cur = cur_ref[...]
    prev = jnp.where(c > 0, prev_ref[...], 0.0)
    nxt = jnp.where(c < n_chunks - 1, next_ref[...], 0.0)
    ext = jnp.concatenate([prev, cur, nxt], axis=0)

    def down(x, s):
        return pltpu.roll(x, s, 0)

    def up(x, s):
        return pltpu.roll(x, EXT - s, 0)

    def mid(x):
        return x[HALO:HALO + ROWS]

    row = lax.broadcasted_iota(jnp.int32, (ROWS, 1), 0)
    t_seq = c * ROWS + row

    outs = []
    for gi, win in enumerate(POOL_WINDOWS):
        h = win // 2
        u = ext[:, gi * POOL_GD:(gi + 1) * POOL_GD]
        trail, lead, s = u, u, 1
        while s < h:
            trail = trail + down(trail, s)
            lead = lead + up(lead, s)
            s *= 2
        total = mid(down(trail, 1) + lead)
        cnt = (jnp.minimum(t_seq + h, seq_len) - jnp.maximum(t_seq - h, 0)).astype(F32)
        pooled = total / cnt - mid(u)
        outs.append(_dot(pooled.astype(BF16), poolw_ref[gi]))
    mixed = jnp.concatenate(outs, axis=1) * pscale_ref[...]
    ypre_ref[...] = mixed.astype(BF16)

    pe = ext[:, OFF_R:OFF_MERGE]
    p = mid(pe)
    lane = lax.broadcasted_iota(jnp.int32, (1, SHIFT_COLS), 1)
    if grid_mode:
        col = t_seq % GRID_W
        left = jnp.where(col > 0, mid(down(pe, 1)), 0.0)
        right = jnp.where(col < GRID_W - 1, mid(up(pe, 1)), 0.0)
        upn = pe[0:ROWS]
        dnn = pe[2 * HALO:2 * HALO + ROWS]
        q = lane % 4
        sh = jnp.where(q == 0, left, jnp.where(q == 1, right, jnp.where(q == 2, upn, dnn)))
    else:
        sh = jnp.where(lane % 2 == 0, mid(down(pe, 1)), mid(up(pe, 1)))
    pm = p + mu_ref[...] * (sh - p)

    r = pm[:, 0:RW]
    k = pm[:, RW:2 * RW]
    v = pm[:, 2 * RW:3 * RW]
    o = 3 * RW
    w_lo = jnp.tanh(pm[:, o:o + 2 * LORA])
    a_lo = pm[:, o + 2 * LORA:o + 4 * LORA]
    gsig_ref[...] = _sigmoid(pm[:, o + 4 * LORA:o + 4 * LORA + GATE_LORA]).astype(BF16)

    seg = seg_ref[...]
    kkr = k * kk_ref[...]
    ss = _dot_exact_rhs(kkr * kkr, seg, 2)
    kkn = kkr / jnp.maximum(jnp.sqrt(ss), 1e-12)
    r_ref[...] = r
    v_ref[...] = v
    kkn_ref[...] = kkn

    bonus = None
    for d, (lw_ref, kd_ref, b_ref) in enumerate(((lwf_ref, kdf_ref, bf_ref), (lwb_ref, kdb_ref, bb_ref))):
        z = dw0_ref[d:d + 1, :] + _dot3(w_lo, dw2_ref[d])
        lw_ref[...] = -math.exp(-0.5) * _sigmoid(z)
        a = _sigmoid(a0_ref[d:d + 1, :] + _dot3(a_lo, a2_ref[d]))
        kd = k * (1.0 + (a - 1.0) * ka_ref[...])
        kd_ref[...] = kd
        b_ref[...] = kkn * a
        bd = _dot_exact_rhs(r * kd * rk_ref[...], seg, 2) * v
        bonus = bd if bonus is None else bonus + bd
    bonus_ref[...] = bonus


def _front(proj, grid_mode, n_seq, seq_len, tok0, weights):
    n_chunks = seq_len // ROWS
    n_tok = n_seq * seq_len
    blk0 = tok0 // ROWS
    hpb = ROWS // HALO
    last_halo = N_TOK // HALO - 1

    def cur_map(b, c):
        return (blk0 + b * n_chunks + c, 0)

    def prev_map(b, c):
        return (jnp.maximum((blk0 + b * n_chunks + c) * hpb - 1, 0), 0)

    def next_map(b, c):
        return (jnp.minimum((blk0 + b * n_chunks + c + 1) * hpb, last_halo), 0)

    def full(a):
        nd = a.ndim
        return pl.BlockSpec(a.shape, lambda b, c, _nd=nd: (0,) * _nd)

    def out_map(b, c):
        return (b * n_chunks + c, 0)

    f32_out = jax.ShapeDtypeStruct((n_tok, RW), F32)
    out_shape = [jax.ShapeDtypeStruct((n_tok, POOL_W), BF16), jax.ShapeDtypeStruct((n_tok, GATE_LORA), BF16)]
    out_shape += [f32_out] * (N_FRONT_OUT - 2)
    out_specs = [pl.BlockSpec((ROWS, POOL_W), out_map), pl.BlockSpec((ROWS, GATE_LORA), out_map)]
    out_specs += [pl.BlockSpec((ROWS, RW), out_map)] * (N_FRONT_OUT - 2)
    assert len(weights) == N_FRONT_IN - 3
    return pl.pallas_call(
        functools.partial(_front_kernel, grid_mode, n_chunks, seq_len),
        out_shape=out_shape,
        grid=(n_seq, n_chunks),
        in_specs=[
            pl.BlockSpec((ROWS, OFF_MERGE), cur_map),
            pl.BlockSpec((HALO, OFF_MERGE), prev_map),
            pl.BlockSpec((HALO, OFF_MERGE), next_map),
        ] + [full(a) for a in weights],
        out_specs=out_specs,
        compiler_params=pltpu.CompilerParams(dimension_semantics=("arbitrary", "arbitrary"),
                                             vmem_limit_bytes=VMEM_LIMIT),
        name="front_grid" if grid_mode else "front_ctx",
    )(proj, proj, proj, *weights)


def _scan_kernel(n_chunks, seqs_per_step, has_init, *refs):
    (r_ref, v_ref, kk_ref, bonus_ref, lwf_ref, lwb_ref, kdf_ref, kdb_ref, bf_ref, bb_ref) = refs[:10]
    if has_init:
        s0f_ref, s0b_ref = refs[10:12]
    y_ref, sf_ref, sb_ref = refs[-3:]
    C = CHUNK
    P2 = 2 * C
    ri = lax.broadcasted_iota(jnp.int32, (P2, P2), 0)
    ci = lax.broadcasted_iota(jnp.int32, (P2, P2), 1)
    same = (ri // C) == (ci // C)
    rp, cp = ri % C, ci % C
    eye = (ri == ci).astype(F32)
    blk = {}
    s = INV_BLOCK
    while s <= C:
        blk[s] = (ri // s) == (ci // s)
        s *= 2
    ti = lax.broadcasted_iota(jnp.int32, (C, C), 0)
    tj = lax.broadcasted_iota(jnp.int32, (C, C), 1)
    head_a = lax.broadcasted_iota(jnp.int32, (C, LANES), 1) < HEAD

    def stack(x):
        return jnp.concatenate([jnp.where(head_a, x, 0.0), jnp.where(head_a, 0.0, x)], axis=0)

    def d3(a, b, dims=_NN):
        return _dot(a[0], b[0], dims) + (_dot(a[0], b[1], dims) + _dot(a[1], b[0], dims))

    def chunks(chains):
        n = range(len(chains))
        masks, sls = [], []
        for (s_ref, q, p, row0, lw_ref, kd_ref, b_ref, reverse) in chains:
            if reverse:
                masks.append((same & (cp > rp), same & (cp >= rp), (tj >= ti), 0))
            else:
                masks.append((same & (cp < rp), same & (cp <= rp), (tj <= ti), C - 1))
            sls.append((pl.ds(row0, C), slice(p * LANES, (p + 1) * LANES)))
        lw = [chains[i][4][sls[i]] for i in n]
        cum = [_dot_exact_lhs(masks[i][2].astype(BF16), lw[i], 3) for i in n]
        e_i = [jnp.exp(cum[i]) for i in n]
        e_n = [jnp.exp(-cum[i]) for i in n]
        QR = [jnp.concatenate([stack(kk_ref[sls[i]] * jnp.exp(cum[i] - lw[i])),
                               stack(r_ref[sls[i]] * e_i[i])], axis=0).astype(BF16) for i in n]
        BK = [jnp.concatenate([stack(chains[i][6][sls[i]] * e_n[i]),
                               stack(chains[i][5][sls[i]] * e_n[i])], axis=0).astype(BF16) for i in n]
        v_st = [stack(v_ref[sls[i]]) for i in n]
        Vs = [v_st[i].astype(BF16) for i in n]
        S = [chains[i][0][chains[i][1], chains[i][2]] for i in n]
        G = [_dot(QR[i], BK[i], _NT) for i in n]
        L = [jnp.where(masks[i][0], G[i][0:P2, 0:P2], 0.0) for i in n]
        Lk = [jnp.where(masks[i][0], G[i][0:P2, P2:2 * P2], 0.0) for i in n]
        RBK = [jnp.concatenate([jnp.where(masks[i][1], G[i][P2:2 * P2, 0:P2], 0.0),
                                jnp.where(masks[i][1], G[i][P2:2 * P2, P2:2 * P2], 0.0)], axis=1) for i in n]
        L0 = [_split2(jnp.where(blk[INV_BLOCK], L[i], 0.0)) for i in n]
        X = [eye - jnp.where(blk[INV_BLOCK], L[i], 0.0) for i in n]
        Pw = L0
        s = 2
        while s < INV_BLOCK:
            Pw = [_split2(d3(Pw[i], Pw[i])) for i in n]
            X = [X[i] + d3(_split2(X[i]), Pw[i]) for i in n]
            s *= 2
        s = INV_BLOCK
        while s < C:
            Dm = [_split2(X[i]) for i in n]
            E = [d3(_split2(jnp.where(blk[2 * s] & jnp.logical_not(blk[s]), L[i], 0.0)), Dm[i]) for i in n]
            X = [X[i] - d3(Dm[i], _split2(E[i])) for i in n]
            s *= 2
        QRA = [_dot(QR[i], S[i].astype(BF16), _NT) for i in n]
        W = [QRA[i][0:P2] + _dot(Lk[i].astype(BF16), Vs[i]) for i in n]
        U = [-_dot(X[i].astype(BF16), W[i].astype(BF16)) for i in n]
        UV = [jnp.concatenate([U[i], v_st[i]], axis=0) for i in n]
        Ys = [QRA[i][P2:2 * P2] + _dot(RBK[i].astype(BF16), UV[i].astype(BF16)) for i in n]
        dS = [_dot(UV[i].T.astype(BF16), BK[i]) for i in n]
        for i in n:
            last = masks[i][3]
            chains[i][0][chains[i][1], chains[i][2]] = (S[i] + dS[i]) * e_i[i][last:last + 1, :]
            y_ref[sls[i]] += Ys[i][0:C] + Ys[i][C:P2]

    y_ref[...] = bonus_ref[...]
    if has_init:
        sf_ref[...] = s0f_ref[...]
        sb_ref[...] = s0b_ref[...]
    else:
        sf_ref[...] = jnp.zeros_like(sf_ref)
        sb_ref[...] = jnp.zeros_like(sb_ref)

    seq_len = n_chunks * C

    def body(c, carry):
        chains = []
        for q in range(seqs_per_step):
            rf = pl.multiple_of(q * seq_len + c * C, C)
            rb = pl.multiple_of(q * seq_len + (n_chunks - 1 - c) * C, C)
            for p in range(N_PAIR):
                chains.append((sf_ref, q, p, rf, lwf_ref, kdf_ref, bf_ref, False))
                chains.append((sb_ref, q, p, rb, lwb_ref, kdb_ref, bb_ref, True))
        chunks(chains)
        return carry

    lax.fori_loop(0, n_chunks, body, 0)


def _scan(arrs, n_seq, seq_len, init):
    n_tok = n_seq * seq_len
    n_chunks = seq_len // CHUNK
    sps = max(1, SCAN_TOKENS_PER_STEP // seq_len)
    tok_spec = pl.BlockSpec((sps * seq_len, RW), lambda b: (b, 0))
    st_spec = pl.BlockSpec((sps, N_PAIR, LANES, LANES), lambda b: (b, 0, 0, 0))
    st_shape = jax.ShapeDtypeStruct((n_seq, N_PAIR, LANES, LANES), F32)
    ins = list(arrs)
    in_specs = [tok_spec] * 10
    if init is not None:
        ins += list(init)
        in_specs += [st_spec, st_spec]
    return pl.pallas_call(
        functools.partial(_scan_kernel, n_chunks, sps, init is not None),
        out_shape=[jax.ShapeDtypeStruct((n_tok, RW), F32), st_shape, st_shape],
        grid=(n_seq // sps,),
        in_specs=in_specs,
        out_specs=[tok_spec, st_spec, st_spec],
        compiler_params=pltpu.CompilerParams(dimension_semantics=("arbitrary",), vmem_limit_bytes=VMEM_LIMIT),
        name="scan_init" if init is not None else "scan_zero",
    )(*ins)


def _post_kernel(xc_ref, xl_ref, mod_ref, yprec_ref, yprel_ref, gsigc_ref, gsigl_ref, yc_ref, yl_ref,
                 nmw_ref, wm_ref, bm_ref, wpo_ref, seg_ref, lnw_ref, lnb_ref, gw2_ref, wro_ref, wo_ref,
                 nfw_ref, rw_ref, rb_ref,
                 x1_ref, hn2_ref, code_ref, w4_ref, meta_ref, texp_ref, carry_ref):
    i = pl.program_id(0)

    @pl.when(i == 0)
    def _():
        carry_ref[...] = jnp.zeros_like(carry_ref)

    x = _path_tile(xc_ref, xl_ref, TM_POST)
    mod = mod_ref[0]
    shift1, scale1, gate1 = mod[:, 0:D], mod[:, D:2 * D], mod[:, 2 * D:3 * D]
    shift2, scale2 = mod[:, 3 * D:4 * D], mod[:, 4 * D:5 * D]
    hn = _modnorm(x, nmw_ref[...], scale1, shift1)
    merge = _sigmoid(_dot(hn.astype(BF16), wm_ref[...]) + bm_ref[...])
    y_pool = _dot(_path_tile(yprec_ref, yprel_ref, TM_POST), wpo_ref[...])

    y = _path_tile(yc_ref, yl_ref, TM_POST)
    seg = seg_ref[...]
    mu = _dot_exact_rhs(y, seg, 2) * (1.0 / HEAD)
    yc = y - mu
    var = _dot_exact_rhs(yc * yc, seg, 2) * (1.0 / HEAD)
    yn = yc * lax.rsqrt(var + GN_EPS) * lnw_ref[...] + lnb_ref[...]
    g = _dot(_path_tile(gsigc_ref, gsigl_ref, TM_POST), gw2_ref[...])
    y_rwkv = _dot((yn * g).astype(BF16), wro_ref[...])

    mixed = merge[:, 0:D] * y_pool + merge[:, D:2 * D] * y_rwkv
    x1 = x + gate1 * _dot(mixed.astype(BF16), wo_ref[...])
    x1_ref[...] = x1
    hn2 = _modnorm(x1, nfw_ref[...], scale2, shift2)
    hn2_ref[...] = hn2

    logits = _dot3(hn2, rw_ref[...]) + rb_ref[...]
    lane = lax.broadcasted_iota(jnp.int32, logits.shape, 1)
    work = logits
    sel = None
    top = None
    for j in range(TOP_K):
        m = jnp.max(work, axis=-1, keepdims=True)
        if j == 0:
            top = m
        idx = jnp.min(jnp.where(work == m, lane, LANES), axis=-1, keepdims=True)
        pick = lane == idx
        sel = pick if sel is None else (sel | pick)
        work = jnp.where(pick, -jnp.inf, work)
    e = jnp.where(sel, jnp.exp(logits - top), 0.0)
    comb = e / jnp.sum(e, axis=-1, keepdims=True)

    tm = logits.shape[0]
    sel_b = jnp.where(sel, 1.0, 0.0).astype(BF16)
    before = (lax.broadcasted_iota(jnp.int32, (tm, tm), 1) < lax.broadcasted_iota(jnp.int32, (tm, tm), 0))
    rank = carry_ref[...] + _dot(before.astype(BF16), sel_b)
    carry = carry_ref[...] + jnp.sum(sel_b.astype(F32), axis=0, keepdims=True)
    carry_ref[...] = carry
    lower_e = (lax.broadcasted_iota(jnp.int32, (LANES, LANES), 0) < lax.broadcasted_iota(jnp.int32, (LANES, LANES), 1))
    rowpos = _dot(sel_b, lower_e.astype(BF16))
    code = jnp.zeros(logits.shape, jnp.int32)
    w4 = jnp.zeros(logits.shape, F32)
    for k in range(TOP_K):
        mk = sel & (rowpos == float(k))
        ek = jnp.sum(jnp.where(mk, lane, 0), axis=-1, keepdims=True)
        rk = jnp.sum(jnp.where(mk, rank, 0.0), axis=-1, keepdims=True).astype(jnp.int32)
        wk = jnp.sum(jnp.where(mk, comb, 0.0), axis=-1, keepdims=True)
        code = jnp.where(lane == k, ek * (1 << RANK_BITS) + rk, code)
        w4 = jnp.where(lane == k, wk, w4)
    code_ref[...] = code
    w4_ref[...] = w4

    tiles = jnp.floor((carry + (TM_SLOT - 1)) * (1.0 / TM_SLOT))
    incl = (lax.broadcasted_iota(jnp.int32, (LANES, LANES), 0) <= lax.broadcasted_iota(jnp.int32, (LANES, LANES), 1))
    tiles8 = jnp.broadcast_to(tiles, (8, LANES)).astype(BF16)
    cum = _dot(tiles8, incl.astype(BF16))[0:1]
    lane1 = lax.broadcasted_iota(jnp.int32, (1, LANES), 1)
    offs = ((cum - tiles) * TM_SLOT).astype(jnp.int32)
    n_used = jnp.sum(jnp.where(lane1 == N_EXP - 1, cum, 0.0), axis=-1, keepdims=True).astype(jnp.int32)
    meta_ref[...] = jnp.where(lane1 == META_NT, n_used, jnp.where(lane1 < N_EXP, offs, 0))
    tile_id = lax.broadcasted_iota(jnp.int32, (N_SLOT_TILES, LANES), 0).astype(F32)
    done = jnp.where((lax.broadcasted_iota(jnp.int32, (N_SLOT_TILES, LANES), 1) < N_EXP) & (cum <= tile_id), 1, 0)
    texp = jnp.minimum(jnp.sum(done, axis=-1, keepdims=True), N_EXP - 1)
    texp_ref[...] = jnp.broadcast_to(texp, (N_SLOT_TILES, LANES))


def _post(x, mod3, ypre, gsig, y, weights):
    def tile(n, dt):
        return pl.BlockSpec((TM_POST, n), lambda i: (i, 0))

    def full(a):
        nd = a.ndim
        return pl.BlockSpec(a.shape, lambda i, _nd=nd: (0,) * _nd)

    def const(shape):
        return pl.BlockSpec(shape, lambda i: (0, 0))

    return pl.pallas_call(
        _post_kernel,
        out_shape=[jax.ShapeDtypeStruct((N_TOK, D), F32), jax.ShapeDtypeStruct((N_TOK, D), F32),
                   jax.ShapeDtypeStruct((N_TOK, LANES), jnp.int32), jax.ShapeDtypeStruct((N_TOK, LANES), F32),
                   jax.ShapeDtypeStruct((1, LANES), jnp.int32),
                   jax.ShapeDtypeStruct((N_SLOT_TILES, LANES), jnp.int32)],
        grid=(N_TOK // TM_POST,),
        in_specs=_path_specs(TM_POST) + [
            pl.BlockSpec((1, 1, N_MOD * D), lambda i: (_mod_row(i, TM_POST), 0, 0)),
        ] + _path_specs(TM_POST, POOL_W) + _path_specs(TM_POST, GATE_LORA) + _path_specs(TM_POST, RW)
        + [full(a) for a in weights],
        out_specs=[tile(D, F32), tile(D, F32), tile(LANES, jnp.int32), tile(LANES, F32),
                   const((1, LANES)), const((N_SLOT_TILES, LANES))],
        scratch_shapes=[pltpu.VMEM((1, LANES), F32)],
        compiler_params=pltpu.CompilerParams(dimension_semantics=("arbitrary",), vmem_limit_bytes=VMEM_LIMIT),
        name="post",
    )(*x, mod3, *ypre, *gsig, *y, *weights)


def _slot_of(code, meta_ref):
    return meta_ref[code >> RANK_BITS] + (code & ((1 << RANK_BITS) - 1))


def _load_codes(code_ref, code_smem, sem):
    cp = pltpu.make_async_copy(code_ref.at[pl.program_id(0)], code_smem, sem)
    cp.start()
    cp.wait()


def _dispatch_kernel(code_ref, meta_ref, hn2_ref, xs_ref, code_smem, zero_buf, sem_c, sem_z, sem):
    @pl.when(pl.program_id(0) == 0)
    def _():
        zero_buf[...] = jnp.zeros_like(zero_buf)

        def pad_tile(e):
            end = meta_ref[e + 1] if e + 1 < N_EXP else meta_ref[META_NT] * TM_SLOT
            start = pl.multiple_of(end - TM_SLOT, TM_SLOT)
            return end > meta_ref[e], pltpu.make_async_copy(zero_buf, xs_ref.at[pl.ds(start, TM_SLOT)], sem_z)

        def tail_tile(j):
            return pltpu.make_async_copy(zero_buf, xs_ref.at[pl.ds(pl.multiple_of(j * TM_SLOT, TM_SLOT), TM_SLOT)], sem_z)

        n_used = meta_ref[META_NT]
        for e in range(N_EXP):
            nonempty, cp = pad_tile(e)
            pl.when(nonempty)(cp.start)

        def start_tail(j, c):
            tail_tile(j).start()
            return c

        def wait_tail(j, c):
            tail_tile(j).wait()
            return c

        lax.fori_loop(n_used, N_SLOT_TILES, start_tail, 0)
        for e in range(N_EXP):
            nonempty, cp = pad_tile(e)
            pl.when(nonempty)(cp.wait)
        lax.fori_loop(n_used, N_SLOT_TILES, wait_tail, 0)

    _load_codes(code_ref, code_smem, sem_c)

    for t in range(TM_DISP):
        for k in range(TOP_K):
            slot = _slot_of(code_smem[t * TOP_K + k], meta_ref)
            pltpu.make_async_copy(hn2_ref.at[pl.ds(t, 1)], xs_ref.at[pl.ds(slot, 1)], sem).start()
    for _ in range(TOP_K):
        pltpu.make_async_copy(hn2_ref, xs_ref.at[pl.ds(0, TM_DISP)], sem).wait()


def _dispatch(code2d, meta, hn2):
    n_steps = N_TOK // TM_DISP
    return pl.pallas_call(
        _dispatch_kernel,
        out_shape=jax.ShapeDtypeStruct((N_SLOTS, D), F32),
        grid=(n_steps,),
        in_specs=[
            pl.BlockSpec(code2d.shape, lambda i: (0, 0)),
            pl.BlockSpec(memory_space=pltpu.SMEM),
            pl.BlockSpec((TM_DISP, D), lambda i: (i, 0)),
        ],
        out_specs=pl.BlockSpec(memory_space=pl.ANY),
        scratch_shapes=[pltpu.SMEM((TM_DISP * TOP_K,), jnp.int32), pltpu.VMEM((TM_SLOT, D), F32),
                        pltpu.SemaphoreType.DMA, pltpu.SemaphoreType.DMA, pltpu.SemaphoreType.DMA],
        compiler_params=pltpu.CompilerParams(dimension_semantics=("arbitrary",), vmem_limit_bytes=VMEM_LIMIT),
        name="dispatch",
    )(code2d, meta, hn2)


def _ffn_kernel(texp_ref, meta_ref, xs_ref, wgu_ref, bgu_ref, wd_ref, bd_ref, ys_ref, wgu_bf, wd_bf):
    i = pl.program_id(0)
    valid = i < meta_ref[META_NT]
    fresh = (i == 0) | (texp_ref[i] != texp_ref[jnp.maximum(i - 1, 0)])

    @pl.when(valid & fresh)
    def _():
        wgu_bf[...] = wgu_ref[0].astype(BF16)
        wd_bf[...] = wd_ref[0].astype(BF16)

    @pl.when(valid)
    def _():
        gu = _dot(xs_ref[...].astype(BF16), wgu_bf[...]) + bgu_ref[0]
        glu = jnp.minimum(gu[:, 0:D_FF], SWIGLU_LIMIT)
        lin = jnp.clip(gu[:, D_FF:2 * D_FF], -SWIGLU_LIMIT, SWIGLU_LIMIT)
        act = glu * _sigmoid(SWIGLU_ALPHA * glu) * (lin + 1.0)
        ys_ref[...] = _dot(act.astype(BF16), wd_bf[...]) + bd_ref[0]

    @pl.when(jnp.logical_not(valid))
    def _():
        ys_ref[...] = jnp.zeros_like(ys_ref)


def _ffn(texp, meta, xs, wgu, bgu, wd, bd):
    def used_tile(i, te, me):
        return (jnp.minimum(i, me[META_NT] - 1), 0)

    return pl.pallas_call(
        _ffn_kernel,
        out_shape=jax.ShapeDtypeStruct((N_SLOTS, D), F32),
        grid_spec=pltpu.PrefetchScalarGridSpec(
            num_scalar_prefetch=2,
            grid=(N_SLOT_TILES,),
            in_specs=[
                pl.BlockSpec((TM_SLOT, D), used_tile),
                pl.BlockSpec((1, D, 2 * D_FF), lambda i, te, me: (te[i], 0, 0)),
                pl.BlockSpec((1, 1, 2 * D_FF), lambda i, te, me: (te[i], 0, 0)),
                pl.BlockSpec((1, D_FF, D), lambda i, te, me: (te[i], 0, 0)),
                pl.BlockSpec((1, 1, D), lambda i, te, me: (te[i], 0, 0)),
            ],
            out_specs=pl.BlockSpec((TM_SLOT, D), lambda i, te, me: (i, 0)),
            scratch_shapes=[pltpu.VMEM((D, 2 * D_FF), BF16), pltpu.VMEM((D_FF, D), BF16)],
        ),
        compiler_params=pltpu.CompilerParams(dimension_semantics=("arbitrary",), vmem_limit_bytes=VMEM_LIMIT),
        name="ffn",
    )(texp, meta, xs, wgu, bgu, wd, bd)


def _combine_kernel(code_ref, meta_ref, x1_ref, w4_ref, mod_ref, nfw_ref, ys_ref, oc_ref, ol_ref,
                    code_smem, buf, sem_c, sem):
    _load_codes(code_ref, code_smem, sem_c)

    for t in range(TM_DISP):
        for k in range(TOP_K):
            slot = _slot_of(code_smem[t * TOP_K + k], meta_ref)
            pltpu.make_async_copy(ys_ref.at[pl.ds(slot, 1)], buf.at[k, pl.ds(t, 1)], sem).start()
    for k in range(TOP_K):
        pltpu.make_async_copy(ys_ref.at[pl.ds(0, TM_DISP)], buf.at[k], sem).wait()
    w4 = w4_ref[...]
    moe = w4[:, 0:1] * buf[0]
    for k in range(1, TOP_K):
        moe = moe + w4[:, k:k + 1] * buf[k]
    gate2 = mod_ref[0][:, 5 * D:6 * D]
    x2 = x1_ref[...] + gate2 * moe
    ms = jnp.mean(x2 * x2, axis=-1, keepdims=True)
    out = x2 * lax.rsqrt(ms + RMS_EPS) * nfw_ref[...]
    is_ctx = pl.program_id(0) < N_CTX_TOK // TM_DISP

    @pl.when(is_ctx)
    def _():
        oc_ref[...] = out

    @pl.when(jnp.logical_not(is_ctx))
    def _():
        ol_ref[...] = out


def _combine(code2d, meta, x1, w4, mod3, nfw, ys):
    n_steps = N_TOK // TM_DISP
    n_ctx = N_CTX_TOK // TM_DISP
    return pl.pallas_call(
        _combine_kernel,
        out_shape=[jax.ShapeDtypeStruct((N_CTX_TOK, D), F32), jax.ShapeDtypeStruct((N_LAT_TOK, D), F32)],
        grid=(n_steps,),
        in_specs=[
            pl.BlockSpec(code2d.shape, lambda i: (0, 0)),
            pl.BlockSpec(memory_space=pltpu.SMEM),
            pl.BlockSpec((TM_DISP, D), lambda i: (i, 0)),
            pl.BlockSpec((TM_DISP, LANES), lambda i: (i, 0)),
            pl.BlockSpec((1, 1, N_MOD * D), lambda i: (_mod_row(i, TM_DISP), 0, 0)),
            pl.BlockSpec((1, D), lambda i: (0, 0)),
            pl.BlockSpec(memory_space=pl.ANY),
        ],
        out_specs=[pl.BlockSpec((TM_DISP, D), lambda i: (jnp.minimum(i, n_ctx - 1), 0)),
                   pl.BlockSpec((TM_DISP, D), lambda i: (jnp.maximum(i - n_ctx, 0), 0))],
        scratch_shapes=[pltpu.SMEM((TM_DISP * TOP_K,), jnp.int32), pltpu.VMEM((TOP_K, TM_DISP, D), F32),
                        pltpu.SemaphoreType.DMA, pltpu.SemaphoreType.DMA],
        compiler_params=pltpu.CompilerParams(dimension_semantics=("arbitrary",), vmem_limit_bytes=VMEM_LIMIT),
        name="combine",
    )(code2d, meta, x1, w4, mod3, nfw, ys)


def _pair_blockdiag(s):
    b = s.shape[0]
    s = s.reshape(b, N_PAIR, 2, HEAD, HEAD)
    z = jnp.zeros((b, N_PAIR, HEAD, HEAD), s.dtype)
    top = jnp.concatenate([s[:, :, 0], z], axis=-1)
    bot = jnp.concatenate([z, s[:, :, 1]], axis=-1)
    return jnp.concatenate([top, bot], axis=-2)


def _pair_unblock(s):
    b = s.shape[0]
    a = s[:, :, 0:HEAD, 0:HEAD]
    c = s[:, :, HEAD:, HEAD:]
    return jnp.stack([a, c], axis=2).reshape(b, 1, N_HEAD, HEAD, HEAD)


def kernel(x_prompt, x_sample, state_fwd, state_bwd, c, c_ctx, w_mod, b_mod, norm_mix_w, w_in, b_merge, pool_w, pool_scale, w_pool_out, shift_mu, decay_w0, decay_w2, iclr_a0, iclr_a2, gate_w2, k_k, k_a, r_k, ln_x_w, ln_x_b, w_rwkv_out, w_o, norm_ffn_w, router_w, router_b, expert_w_gu, expert_b_gu, expert_w_down, expert_b_down, norm_final_w):
    l = 0
    x_ctx = x_prompt.reshape(N_CTX_TOK, D)
    x_lat = x_sample.reshape(N_LAT_TOK, D)
    cvec = jnp.concatenate([c_ctx[None, :], c, jnp.zeros((MOD_ROWS - 1 - N_LAT_SEQ, D), F32)], axis=0)
    mod = _modulation(cvec, w_mod[l], b_mod[l][None, :])
    mod3 = mod.reshape(MOD_ROWS, 1, N_MOD * D)

    w_in_l = w_in[l]
    proj = _projection(x_ctx, x_lat, mod3, norm_mix_w[l][None, :], w_in_l[:, :OFF_MERGE].astype(BF16))

    zl = jnp.zeros((LORA, RW), F32)
    dw2 = jnp.stack([jnp.concatenate([decay_w2[l, 0], zl], 0), jnp.concatenate([zl, decay_w2[l, 1]], 0)])
    a2 = jnp.stack([jnp.concatenate([iclr_a2[l, 0], zl], 0), jnp.concatenate([zl, iclr_a2[l, 1]], 0)])
    hid = jnp.arange(RW) // HEAD
    seg = (hid[:, None] == hid[None, :]).astype(BF16)
    front_w = (pool_w[l].astype(BF16), pool_scale[l][None, :], shift_mu[l][None, :],
               decay_w0[l], dw2, iclr_a0[l], a2, k_k[l][None, :], k_a[l][None, :],
               r_k[l].reshape(1, RW), seg)
    f_ctx = _front(proj, False, N_CTX_SEQ, T_CTX, 0, front_w)
    f_lat = _front(proj, True, N_LAT_SEQ, T_LAT, N_CTX_TOK, front_w)

    y_ctx, sf, sb = _scan(f_ctx[2:], N_CTX_SEQ, T_CTX, None)
    init = (_pair_blockdiag(state_fwd[:, l]), _pair_blockdiag(state_bwd[:, l]))
    y_lat, _, _ = _scan(f_lat[2:], N_LAT_SEQ, T_LAT, init)

    rw_pad = jnp.concatenate([router_w[l], jnp.zeros((D, LANES - N_EXP), F32)], axis=1)
    rb_pad = jnp.concatenate([router_b[l], jnp.full((LANES - N_EXP,), -1e30, F32)])[None, :]
    post_w = (norm_mix_w[l][None, :], w_in_l[:, OFF_MERGE:].astype(BF16), b_merge[l][None, :],
              w_pool_out[l].astype(BF16), seg, ln_x_w[l][None, :], ln_x_b[l][None, :],
              gate_w2[l].astype(BF16), w_rwkv_out[l].astype(BF16), w_o[l].astype(BF16),
              norm_ffn_w[l][None, :], rw_pad, rb_pad)
    x1, hn2, code, w4, meta, texp = _post((x_ctx, x_lat), mod3, (f_ctx[0], f_lat[0]), (f_ctx[1], f_lat[1]),
                                          (y_ctx, y_lat), post_w)

    code2d = code[:, :TOP_K].reshape(N_TOK // TM_DISP, TM_DISP * TOP_K)
    meta1 = meta.reshape(LANES)
    xs = _dispatch(code2d, meta1, hn2)
    ys = _ffn(texp[:, 0], meta1, xs, expert_w_gu[l], expert_b_gu[l][:, None, :],
              expert_w_down[l], expert_b_down[l][:, None, :])
    out_ctx, out_lat = _combine(code2d, meta1, x1, w4, mod3, norm_final_w[None, :], ys)
    y_prompt = out_ctx.reshape(N_CTX_SEQ, T_CTX, D)
    y_sample = out_lat.reshape(N_LAT_SEQ, T_LAT, D)
    return (y_prompt, y_sample, _pair_unblock(sf), _pair_unblock(sb))
```

```python
import functools
import math

import jax
import jax.numpy as jnp
from jax import lax
from jax.experimental import pallas as pl
from jax.experimental.pallas import tpu as pltpu

F32 = jnp.float32
BF16 = jnp.bfloat16

D = 1024
N_CTX_SEQ, T_CTX = 32, 256
N_LAT_SEQ, T_LAT = 4, 1024
GRID_W = 64
N_CTX_TOK = N_CTX_SEQ * T_CTX
N_LAT_TOK = N_LAT_SEQ * T_LAT
N_TOK = N_CTX_TOK + N_LAT_TOK

POOL_W = 512
POOL_G = 4
POOL_GD = POOL_W // POOL_G
POOL_WINDOWS = (2, 4, 8, 16)
RW = 512
HEAD = 64
N_HEAD = RW // HEAD
N_PAIR = N_HEAD // 2
LORA = 64
GATE_LORA = 128
N_EXP = 32
TOP_K = 4
D_FF = 1024
SWIGLU_ALPHA = 1.702
SWIGLU_LIMIT = 7.0
RMS_EPS = 1e-6
GN_EPS = 1e-5 * HEAD
N_MOD = 6

OFF_R = POOL_W
SHIFT_COLS = 3 * RW + 4 * LORA + GATE_LORA
OFF_MERGE = OFF_R + SHIFT_COLS
IN_COLS = OFF_MERGE + 2 * D

LANES = 128
CHUNK = 64
INV_BLOCK = 8
SCAN_TOKENS_PER_STEP = 512
ROWS = 256
HALO = 64
EXT = ROWS + 2 * HALO
TM_PROJ = 512
TM_POST = 256
TM_DISP = 256
TM_SLOT = 512
N_SLOT_TILES = N_TOK * TOP_K // TM_SLOT + N_EXP
N_SLOTS = N_SLOT_TILES * TM_SLOT
RANK_BITS = 14
META_NT = N_EXP
MOD_ROWS = 8
VMEM_LIMIT = 56 * 1024 * 1024


def _sigmoid(x):
    return 1.0 / (1.0 + jnp.exp(-x))


def _split2(a):
    hi = a.astype(BF16)
    lo = (a - hi.astype(F32)).astype(BF16)
    return hi, lo


_NN = (((1,), (0,)), ((), ()))
_NT = (((1,), (1,)), ((), ()))


def _dot(a, b, dims=_NN):
    return lax.dot_general(a, b, dims, preferred_element_type=F32)


def _dot3(a, b, dims=_NN):
    ah, al = _split2(a)
    bh, bl = _split2(b)
    return _dot(ah, bh, dims) + (_dot(ah, bl, dims) + _dot(al, bh, dims))


def _dot_exact_lhs(a_bf16, b, passes):
    acc = None
    rem = b
    for _ in range(passes):
        part = rem.astype(BF16)
        term = _dot(a_bf16, part)
        acc = term if acc is None else acc + term
        rem = rem - part.astype(F32)
    return acc


def _dot_exact_rhs(a, b_bf16, passes):
    acc = None
    rem = a
    for _ in range(passes):
        part = rem.astype(BF16)
        term = _dot(part, b_bf16)
        acc = term if acc is None else acc + term
        rem = rem - part.astype(F32)
    return acc


def _modnorm(x, w, scale, shift):
    ms = jnp.mean(x * x, axis=-1, keepdims=True)
    return x * lax.rsqrt(ms + RMS_EPS) * w * (1.0 + scale) + shift


def _mod_row(i, tm):
    n_ctx = N_CTX_TOK // tm
    per = T_LAT // tm
    return jnp.where(i < n_ctx, 0, 1 + (i - n_ctx) // per)


def _mod_kernel(c_ref, w_ref, b_ref, o_ref):
    c = c_ref[...]
    s = c * _sigmoid(c)
    o_ref[...] = _dot3(s, w_ref[...]) + b_ref[...]


def _modulation(cvec, w_mod, b_mod):
    return pl.pallas_call(
        _mod_kernel,
        out_shape=jax.ShapeDtypeStruct((MOD_ROWS, N_MOD * D), F32),
        grid=(N_MOD,),
        in_specs=[
            pl.BlockSpec((MOD_ROWS, D), lambda j: (0, 0)),
            pl.BlockSpec((D, D), lambda j: (0, j)),
            pl.BlockSpec((1, D), lambda j: (0, j)),
        ],
        out_specs=pl.BlockSpec((MOD_ROWS, D), lambda j: (0, j)),
        compiler_params=pltpu.CompilerParams(dimension_semantics=("arbitrary",), vmem_limit_bytes=VMEM_LIMIT),
        name="mod",
    )(cvec, w_mod, b_mod)


def _path_specs(tm, n_cols=D):
    n_ctx = N_CTX_TOK // tm
    return [pl.BlockSpec((tm, n_cols), lambda i: (jnp.minimum(i, n_ctx - 1), 0)),
            pl.BlockSpec((tm, n_cols), lambda i: (jnp.maximum(i - n_ctx, 0), 0))]


def _path_tile(c_ref, l_ref, tm):
    return jnp.where(pl.program_id(0) < N_CTX_TOK // tm, c_ref[...], l_ref[...])


def _proj_kernel(xc_ref, xl_ref, mod_ref, nw_ref, w_ref, o_ref):
    mod = mod_ref[0]
    hn = _modnorm(_path_tile(xc_ref, xl_ref, TM_PROJ), nw_ref[...], mod[:, D:2 * D], mod[:, 0:D])
    o_ref[...] = _dot(hn.astype(BF16), w_ref[...])


def _projection(x_ctx, x_lat, mod3, norm_w, w_a):
    n_cols = w_a.shape[1]
    return pl.pallas_call(
        _proj_kernel,
        out_shape=jax.ShapeDtypeStruct((N_TOK, n_cols), F32),
        grid=(N_TOK // TM_PROJ,),
        in_specs=_path_specs(TM_PROJ) + [
            pl.BlockSpec((1, 1, N_MOD * D), lambda i: (_mod_row(i, TM_PROJ), 0, 0)),
            pl.BlockSpec((1, D), lambda i: (0, 0)),
            pl.BlockSpec((D, n_cols), lambda i: (0, 0)),
        ],
        out_specs=pl.BlockSpec((TM_PROJ, n_cols), lambda i: (i, 0)),
        compiler_params=pltpu.CompilerParams(dimension_semantics=("arbitrary",), vmem_limit_bytes=VMEM_LIMIT),
        name="proj",
    )(x_ctx, x_lat, mod3, norm_w, w_a)


N_FRONT_IN = 14
N_FRONT_OUT = 12


def _front_kernel(grid_mode, n_chunks, seq_len, *refs):
    (cur_ref, prev_ref, next_ref, poolw_ref, pscale_ref, mu_ref,
     dw0_ref, dw2_ref, a0_ref, a2_ref, kk_ref, ka_ref, rk_ref, seg_ref) = refs[:N_FRONT_IN]
    (ypre_ref, gsig_ref, r_ref, v_ref, kkn_ref, bonus_ref,
     lwf_ref, lwb_ref, kdf_ref, kdb_ref, bf_ref, bb_ref) = refs[-N_FRONT_OUT:]
    c = pl.program_id(1)
    cur = cur_ref[...]
    prev = jnp.where(c > 0, prev_ref[...], 0.0)
    nxt = jnp.where(c < n_chunks - 1, next_ref[...], 0.0)
    ext = jnp.concatenate([prev, cur, nxt], axis=0)

    def down(x, s):
        return pltpu.roll(x, s, 0)

    def up(x, s):
        return pltpu.roll(x, EXT - s, 0)

    def mid(x):
        return x[HALO:HALO + ROWS]

    row = lax.broadcasted_iota(jnp.int32, (ROWS, 1), 0)
    t_seq = c * ROWS + row

    outs = []
    for gi, win in enumerate(POOL_WINDOWS):
        h = win // 2
        u = ext[:, gi * POOL_GD:(gi + 1) * POOL_GD]
        trail, lead, s = u, u, 1
        while s < h:
            trail = trail + down(trail, s)
            lead = lead + up(lead, s)
            s *= 2
        total = mid(down(trail, 1) + lead)
        cnt = (jnp.minimum(t_seq + h, seq_len) - jnp.maximum(t_seq - h, 0)).astype(F32)
        pooled = total / cnt - mid(u)
        outs.append(_dot(pooled.astype(BF16), poolw_ref[gi]))
    mixed = jnp.concatenate(outs, axis=1) * pscale_ref[...]
    ypre_ref[...] = mixed.astype(BF16)

    pe = ext[:, OFF_R:OFF_MERGE]
    p = mid(pe)
    lane = lax.broadcasted_iota(jnp.int32, (1, SHIFT_COLS), 1)
    if grid_mode:
        col = t_seq % GRID_W
        left = jnp.where(col > 0, mid(down(pe, 1)), 0.0)
        right = jnp.where(col < GRID_W - 1, mid(up(pe, 1)), 0.0)
        upn = pe[0:ROWS]
        dnn = pe[2 * HALO:2 * HALO + ROWS]
        q = lane % 4
        sh = jnp.where(q == 0, left, jnp.where(q == 1, right, jnp.where(q == 2, upn, dnn)))
    else:
        sh = jnp.where(lane % 2 == 0, mid(down(pe, 1)), mid(up(pe, 1)))
    pm = p + mu_ref[...] * (sh - p)

    r = pm[:, 0:RW]
    k = pm[:, RW:2 * RW]
    v = pm[:, 2 * RW:3 * RW]
    o = 3 * RW
    w_lo = jnp.tanh(pm[:, o:o + 2 * LORA])
    a_lo = pm[:, o + 2 * LORA:o + 4 * LORA]
    gsig_ref[...] = _sigmoid(pm[:, o + 4 * LORA:o + 4 * LORA + GATE_LORA]).astype(BF16)

    seg = seg_ref[...]
    kkr = k * kk_ref[...]
    ss = _dot_exact_rhs(kkr * kkr, seg, 2)
    kkn = kkr / jnp.maximum(jnp.sqrt(ss), 1e-12)
    r_ref[...] = r
    v_ref[...] = v
    kkn_ref[...] = kkn

    bonus = None
    for d, (lw_ref, kd_ref, b_ref) in enumerate(((lwf_ref, kdf_ref, bf_ref), (lwb_ref, kdb_ref, bb_ref))):
        z = dw0_ref[d:d + 1, :] + _dot3(w_lo, dw2_ref[d])
        lw_ref[...] = -math.exp(-0.5) * _sigmoid(z)
        a = _sigmoid(a0_ref[d:d + 1, :] + _dot3(a_lo, a2_ref[d]))
        kd = k * (1.0 + (a - 1.0) * ka_ref[...])
        kd_ref[...] = kd
        b_ref[...] = kkn * a
        bd = _dot_exact_rhs(r * kd * rk_ref[...], seg, 2) * v
        bonus = bd if bonus is None else bonus + bd
    bonus_ref[...] = bonus


def _front(proj, grid_mode, n_seq, seq_len, tok0, weights):
    n_chunks = seq_len // ROWS
    n_tok = n_seq * seq_len
    blk0 = tok0 // ROWS
    hpb = ROWS // HALO
    last_halo = N_TOK // HALO - 1

    def cur_map(b, c):
        return (blk0 + b * n_chunks + c, 0)

    def prev_map(b, c):
        return (jnp.maximum((blk0 + b * n_chunks + c) * hpb - 1, 0), 0)

    def next_map(b, c):
        return (jnp.minimum((blk0 + b * n_chunks + c + 1) * hpb, last_halo), 0)

    def full(a):
        nd = a.ndim
        return pl.BlockSpec(a.shape, lambda b, c, _nd=nd: (0,) * _nd)

    def out_map(b, c):
        return (b * n_chunks + c, 0)

    f32_out = jax.ShapeDtypeStruct((n_tok, RW), F32)
    out_shape = [jax.ShapeDtypeStruct((n_tok, POOL_W), BF16), jax.ShapeDtypeStruct((n_tok, GATE_LORA), BF16)]
    out_shape += [f32_out] * (N_FRONT_OUT - 2)
    out_specs = [pl.BlockSpec((ROWS, POOL_W), out_map), pl.BlockSpec((ROWS, GATE_LORA), out_map)]
    out_specs += [pl.BlockSpec((ROWS, RW), out_map)] * (N_FRONT_OUT - 2)
    assert len(weights) == N_FRONT_IN - 3
    return pl.pallas_call(
        functools.partial(_front_kernel, grid_mode, n_chunks, seq_len),
        out_shape=out_shape,
        grid=(n_seq, n_chunks),
        in_specs=[
            pl.BlockSpec((ROWS, OFF_MERGE), cur_map),
            pl.BlockSpec((HALO, OFF_MERGE), prev_map),
            pl.BlockSpec((HALO, OFF_MERGE), next_map),
        ] + [full(a) for a in weights],
        out_specs=out_specs,
        compiler_params=pltpu.CompilerParams(dimension_semantics=("arbitrary", "arbitrary"),
                                             vmem_limit_bytes=VMEM_LIMIT),
        name="front_grid" if grid_mode else "front_ctx",
    )(proj, proj, proj, *weights)


def _scan_kernel(n_chunks, seqs_per_step, has_init, *refs):
    (r_ref, v_ref, kk_ref, bonus_ref, lwf_ref, lwb_ref, kdf_ref, kdb_ref, bf_ref, bb_ref) = refs[:10]
    if has_init:
        s0f_ref, s0b_ref = refs[10:12]
    y_ref, sf_ref, sb_ref = refs[-3:]
    C = CHUNK
    P2 = 2 * C
    ri = lax.broadcasted_iota(jnp.int32, (P2, P2), 0)
    ci = lax.broadcasted_iota(jnp.int32, (P2, P2), 1)
    same = (ri // C) == (ci // C)
    rp, cp = ri % C, ci % C
    eye = (ri == ci).astype(F32)
    blk = {}
    s = INV_BLOCK
    while s <= C:
        blk[s] = (ri // s) == (ci // s)
        s *= 2
    ti = lax.broadcasted_iota(jnp.int32, (C, C), 0)
    tj = lax.broadcasted_iota(jnp.int32, (C, C), 1)
    head_a = lax.broadcasted_iota(jnp.int32, (C, LANES), 1) < HEAD

    def stack(x):
        return jnp.concatenate([jnp.where(head_a, x, 0.0), jnp.where(head_a, 0.0, x)], axis=0)

    def d3(a, b, dims=_NN):
        return _dot(a[0], b[0], dims) + (_dot(a[0], b[1], dims) + _dot(a[1], b[0], dims))

    def chunks(chains):
        n = range(len(chains))
        masks, sls = [], []
        for (s_ref, q, p, row0, lw_ref, kd_ref, b_ref, reverse) in chains:
            if reverse:
                masks.append((same & (cp > rp), same & (cp >= rp), (tj >= ti), 0))
            else:
                masks.append((same & (cp < rp), same & (cp <= rp), (tj <= ti), C - 1))
            sls.append((pl.ds(row0, C), slice(p * LANES, (p + 1) * LANES)))
        lw = [chains[i][4][sls[i]] for i in n]
        cum = [_dot_exact_lhs(masks[i][2].astype(BF16), lw[i], 3) for i in n]
        e_i = [jnp.exp(cum[i]) for i in n]
        e_n = [jnp.exp(-cum[i]) for i in n]
        QR = [jnp.concatenate([stack(kk_ref[sls[i]] * jnp.exp(cum[i] - lw[i])),
                               stack(r_ref[sls[i]] * e_i[i])], axis=0).astype(BF16) for i in n]
        BK = [jnp.concatenate([stack(chains[i][6][sls[i]] * e_n[i]),
                               stack(chains[i][5][sls[i]] * e_n[i])], axis=0).astype(BF16) for i in n]
        v_st = [stack(v_ref[sls[i]]) for i in n]
        Vs = [v_st[i].astype(BF16) for i in n]
        S = [chains[i][0][chains[i][1], chains[i][2]] for i in n]
        G = [_dot(QR[i], BK[i], _NT) for i in n]
        L = [jnp.where(masks[i][0], G[i][0:P2, 0:P2], 0.0) for i in n]
        Lk = [jnp.where(masks[i][0], G[i][0:P2, P2:2 * P2], 0.0) for i in n]
        RBK = [jnp.concatenate([jnp.where(masks[i][1], G[i][P2:2 * P2, 0:P2], 0.0),
                                jnp.where(masks[i][1], G[i][P2:2 * P2, P2:2 * P2], 0.0)], axis=1) for i in n]
        L0 = [_split2(jnp.where(blk[INV_BLOCK], L[i], 0.0)) for i in n]
        X = [eye - jnp.where(blk[INV_BLOCK], L[i], 0.0) for i in n]
        Pw = L0
        s = 2
        while s < INV_BLOCK:
            Pw = [_split2(d3(Pw[i], Pw[i])) for i in n]
            X = [X[i] + d3(_split2(X[i]), Pw[i]) for i in n]
            s *= 2
        s = INV_BLOCK
        while s < C:
            Dm = [_split2(X[i]) for i in n]
            E = [d3(_split2(jnp.where(blk[2 * s] & jnp.logical_not(blk[s]), L[i], 0.0)), Dm[i]) for i in n]
            X = [X[i] - d3(Dm[i], _split2(E[i])) for i in n]
            s *= 2
        QRA = [_dot(QR[i], S[i].astype(BF16), _NT) for i in n]
        W = [QRA[i][0:P2] + _dot(Lk[i].astype(BF16), Vs[i]) for i in n]
        U = [-_dot(X[i].astype(BF16), W[i].astype(BF16)) for i in n]
        UV = [jnp.concatenate([U[i], v_st[i]], axis=0) for i in n]
        Ys = [QRA[i][P2:2 * P2] + _dot(RBK[i].astype(BF16), UV[i].astype(BF16)) for i in n]
        dS = [_dot(UV[i].T.astype(BF16), BK[i]) for i in n]
        for i in n:
            last = masks[i][3]
            chains[i][0][chains[i][1], chains[i][2]] = (S[i] + dS[i]) * e_i[i][last:last + 1, :]
            y_ref[sls[i]] += Ys[i][0:C] + Ys[i][C:P2]

    y_ref[...] = bonus_ref[...]
    if has_init:
        sf_ref[...] = s0f_ref[...]
        sb_ref[...] = s0b_ref[...]
    else:
        sf_ref[...] = jnp.zeros_like(sf_ref)
        sb_ref[...] = jnp.zeros_like(sb_ref)

    seq_len = n_chunks * C

    def body(c, carry):
        chains = []
        for q in range(seqs_per_step):
            rf = pl.multiple_of(q * seq_len + c * C, C)
            rb = pl.multiple_of(q * seq_len + (n_chunks - 1 - c) * C, C)
            for p in range(N_PAIR):
                chains.append((sf_ref, q, p, rf, lwf_ref, kdf_ref, bf_ref, False))
                chains.append((sb_ref, q, p, rb, lwb_ref, kdb_ref, bb_ref, True))
        chunks(chains)
        return carry

    lax.fori_loop(0, n_chunks, body, 0)


def _scan(arrs, n_seq, seq_len, init):
    n_tok = n_seq * seq_len
    n_chunks = seq_len // CHUNK
    sps = max(1, SCAN_TOKENS_PER_STEP // seq_len)
    tok_spec = pl.BlockSpec((sps * seq_len, RW), lambda b: (b, 0))
    st_spec = pl.BlockSpec((sps, N_PAIR, LANES, LANES), lambda b: (b, 0, 0, 0))
    st_shape = jax.ShapeDtypeStruct((n_seq, N_PAIR, LANES, LANES), F32)
    ins = list(arrs)
    in_specs = [tok_spec] * 10
    if init is not None:
        ins += list(init)
        in_specs += [st_spec, st_spec]
    return pl.pallas_call(
        functools.partial(_scan_kernel, n_chunks, sps, init is not None),
        out_shape=[jax.ShapeDtypeStruct((n_tok, RW), F32), st_shape, st_shape],
        grid=(n_seq // sps,),
        in_specs=in_specs,
        out_specs=[tok_spec, st_spec, st_spec],
        compiler_params=pltpu.CompilerParams(dimension_semantics=("arbitrary",), vmem_limit_bytes=VMEM_LIMIT),
        name="scan_init" if init is not None else "scan_zero",
    )(*ins)


def _post_kernel(xc_ref, xl_ref, mod_ref, yprec_ref, yprel_ref, gsigc_ref, gsigl_ref, yc_ref, yl_ref,
                 nmw_ref, wm_ref, bm_ref, wpo_ref, seg_ref, lnw_ref, lnb_ref, gw2_ref, wro_ref, wo_ref,
                 nfw_ref, rw_ref, rb_ref,
                 x1_ref, hn2_ref, code_ref, w4_ref, meta_ref, texp_ref, carry_ref):
    i = pl.program_id(0)

    @pl.when(i == 0)
    def _():
        carry_ref[...] = jnp.zeros_like(carry_ref)

    x = _path_tile(xc_ref, xl_ref, TM_POST)
    mod = mod_ref[0]
    shift1, scale1, gate1 = mod[:, 0:D], mod[:, D:2 * D], mod[:, 2 * D:3 * D]
    shift2, scale2 = mod[:, 3 * D:4 * D], mod[:, 4 * D:5 * D]
    hn = _modnorm(x, nmw_ref[...], scale1, shift1)
    merge = _sigmoid(_dot(hn.astype(BF16), wm_ref[...]) + bm_ref[...])
    y_pool = _dot(_path_tile(yprec_ref, yprel_ref, TM_POST), wpo_ref[...])

    y = _path_tile(yc_ref, yl_ref, TM_POST)
    seg = seg_ref[...]
    mu = _dot_exact_rhs(y, seg, 2) * (1.0 / HEAD)
    yc = y - mu
    var = _dot_exact_rhs(yc * yc, seg, 2) * (1.0 / HEAD)
    yn = yc * lax.rsqrt(var + GN_EPS) * lnw_ref[...] + lnb_ref[...]
    g = _dot(_path_tile(gsigc_ref, gsigl_ref, TM_POST), gw2_ref[...])
    y_rwkv = _dot((yn * g).astype(BF16), wro_ref[...])

    mixed = merge[:, 0:D] * y_pool + merge[:, D:2 * D] * y_rwkv
    x1 = x + gate1 * _dot(mixed.astype(BF16), wo_ref[...])
    x1_ref[...] = x1
    hn2 = _modnorm(x1, nfw_ref[...], scale2, shift2)
    hn2_ref[...] = hn2

    logits = _dot3(hn2, rw_ref[...]) + rb_ref[...]
    lane = lax.broadcasted_iota(jnp.int32, logits.shape, 1)
    work = logits
    sel = None
    top = None
    for j in range(TOP_K):
        m = jnp.max(work, axis=-1, keepdims=True)
        if j == 0:
            top = m
        idx = jnp.min(jnp.where(work == m, lane, LANES), axis=-1, keepdims=True)
        pick = lane == idx
        sel = pick if sel is None else (sel | pick)
        work = jnp.where(pick, -jnp.inf, work)
    e = jnp.where(sel, jnp.exp(logits - top), 0.0)
    comb = e / jnp.sum(e, axis=-1, keepdims=True)

    tm = logits.shape[0]
    sel_b = jnp.where(sel, 1.0, 0.0).astype(BF16)
    before = (lax.broadcasted_iota(jnp.int32, (tm, tm), 1) < lax.broadcasted_iota(jnp.int32, (tm, tm), 0))
    rank = carry_ref[...] + _dot(before.astype(BF16), sel_b)
    carry = carry_ref[...] + jnp.sum(sel_b.astype(F32), axis=0, keepdims=True)
    carry_ref[...] = carry
    lower_e = (lax.broadcasted_iota(jnp.int32, (LANES, LANES), 0) < lax.broadcasted_iota(jnp.int32, (LANES, LANES), 1))
    rowpos = _dot(sel_b, lower_e.astype(BF16))
    code = jnp.zeros(logits.shape, jnp.int32)
    w4 = jnp.zeros(logits.shape, F32)
    for k in range(TOP_K):
        mk = sel & (rowpos == float(k))
        ek = jnp.sum(jnp.where(mk, lane, 0), axis=-1, keepdims=True)
        rk = jnp.sum(jnp.where(mk, rank, 0.0), axis=-1, keepdims=True).astype(jnp.int32)
        wk = jnp.sum(jnp.where(mk, comb, 0.0), axis=-1, keepdims=True)
        code = jnp.where(lane == k, ek * (1 << RANK_BITS) + rk, code)
        w4 = jnp.where(lane == k, wk, w4)
    code_ref[...] = code
    w4_ref[...] = w4

    tiles = jnp.floor((carry + (TM_SLOT - 1)) * (1.0 / TM_SLOT))
    incl = (lax.broadcasted_iota(jnp.int32, (LANES, LANES), 0) <= lax.broadcasted_iota(jnp.int32, (LANES, LANES), 1))
    tiles8 = jnp.broadcast_to(tiles, (8, LANES)).astype(BF16)
    cum = _dot(tiles8, incl.astype(BF16))[0:1]
    lane1 = lax.broadcasted_iota(jnp.int32, (1, LANES), 1)
    offs = ((cum - tiles) * TM_SLOT).astype(jnp.int32)
    n_used = jnp.sum(jnp.where(lane1 == N_EXP - 1, cum, 0.0), axis=-1, keepdims=True).astype(jnp.int32)
    meta_ref[...] = jnp.where(lane1 == META_NT, n_used, jnp.where(lane1 < N_EXP, offs, 0))
    tile_id = lax.broadcasted_iota(jnp.int32, (N_SLOT_TILES, LANES), 0).astype(F32)
    done = jnp.where((lax.broadcasted_iota(jnp.int32, (N_SLOT_TILES, LANES), 1) < N_EXP) & (cum <= tile_id), 1, 0)
    texp = jnp.minimum(jnp.sum(done, axis=-1, keepdims=True), N_EXP - 1)
    texp_ref[...] = jnp.broadcast_to(texp, (N_SLOT_TILES, LANES))


def _post(x, mod3, ypre, gsig, y, weights):
    def tile(n, dt):
        return pl.BlockSpec((TM_POST, n), lambda i: (i, 0))

    def full(a):
        nd = a.ndim
        return pl.BlockSpec(a.shape, lambda i, _nd=nd: (0,) * _nd)

    def const(shape):
        return pl.BlockSpec(shape, lambda i: (0, 0))

    return pl.pallas_call(
        _post_kernel,
        out_shape=[jax.ShapeDtypeStruct((N_TOK, D), F32), jax.ShapeDtypeStruct((N_TOK, D), F32),
                   jax.ShapeDtypeStruct((N_TOK, LANES), jnp.int32), jax.ShapeDtypeStruct((N_TOK, LANES), F32),
                   jax.ShapeDtypeStruct((1, LANES), jnp.int32),
                   jax.ShapeDtypeStruct((N_SLOT_TILES, LANES), jnp.int32)],
        grid=(N_TOK // TM_POST,),
        in_specs=_path_specs(TM_POST) + [
            pl.BlockSpec((1, 1, N_MOD * D), lambda i: (_mod_row(i, TM_POST), 0, 0)),
        ] + _path_specs(TM_POST, POOL_W) + _path_specs(TM_POST, GATE_LORA) + _path_specs(TM_POST, RW)
        + [full(a) for a in weights],
        out_specs=[tile(D, F32), tile(D, F32), tile(LANES, jnp.int32), tile(LANES, F32),
                   const((1, LANES)), const((N_SLOT_TILES, LANES))],
        scratch_shapes=[pltpu.VMEM((1, LANES), F32)],
        compiler_params=pltpu.CompilerParams(dimension_semantics=("arbitrary",), vmem_limit_bytes=VMEM_LIMIT),
        name="post",
    )(*x, mod3, *ypre, *gsig, *y, *weights)


def _slots_kernel(code_ref, meta_ref, slot_ref):
    code = code_ref[...]
    meta = meta_ref[...]
    lane = lax.broadcasted_iota(jnp.int32, code.shape, 1)
    expert = code >> RANK_BITS
    rank = code & ((1 << RANK_BITS) - 1)
    slot = jnp.zeros(code.shape, jnp.int32)
    for k in range(TOP_K):
        ek = jnp.sum(jnp.where(lane == k, expert, 0), axis=-1, keepdims=True)
        off = jnp.sum(jnp.where(lane == ek, meta, 0), axis=-1, keepdims=True)
        slot = jnp.where(lane == k, off + rank, slot)
    slot_ref[...] = slot


def _slots(code, meta):
    tm = 1024
    return pl.pallas_call(
        _slots_kernel,
        out_shape=jax.ShapeDtypeStruct((N_TOK, LANES), jnp.int32),
        grid=(N_TOK // tm,),
        in_specs=[pl.BlockSpec((tm, LANES), lambda i: (i, 0)), pl.BlockSpec((1, LANES), lambda i: (0, 0))],
        out_specs=pl.BlockSpec((tm, LANES), lambda i: (i, 0)),
        compiler_params=pltpu.CompilerParams(dimension_semantics=("arbitrary",), vmem_limit_bytes=VMEM_LIMIT),
        name="slots",
    )(code, meta)


def _load_slots(slot_ref, slot_smem, sem):
    cp = pltpu.make_async_copy(slot_ref.at[pl.program_id(0)], slot_smem, sem)
    cp.start()
    cp.wait()


def _dispatch_kernel(slot_ref, meta_ref, hn2_ref, xs_ref, slot_smem, zero_buf, sem_c, sem_z, sem):
    @pl.when(pl.program_id(0) == 0)
    def _():
        zero_buf[...] = jnp.zeros_like(zero_buf)

        def pad_tile(e):
            end = meta_ref[e + 1] if e + 1 < N_EXP else meta_ref[META_NT] * TM_SLOT
            start = pl.multiple_of(end - TM_SLOT, TM_SLOT)
            return end > meta_ref[e], pltpu.make_async_copy(zero_buf, xs_ref.at[pl.ds(start, TM_SLOT)], sem_z)

        def tail_tile(j):
            return pltpu.make_async_copy(zero_buf, xs_ref.at[pl.ds(pl.multiple_of(j * TM_SLOT, TM_SLOT), TM_SLOT)], sem_z)

        n_used = meta_ref[META_NT]
        for e in range(N_EXP):
            nonempty, cp = pad_tile(e)
            pl.when(nonempty)(cp.start)

        def start_tail(j, c):
            tail_tile(j).start()
            return c

        def wait_tail(j, c):
            tail_tile(j).wait()
            return c

        lax.fori_loop(n_used, N_SLOT_TILES, start_tail, 0)
        for e in range(N_EXP):
            nonempty, cp = pad_tile(e)
            pl.when(nonempty)(cp.wait)
        lax.fori_loop(n_used, N_SLOT_TILES, wait_tail, 0)

    _load_slots(slot_ref, slot_smem, sem_c)

    for t in range(TM_DISP):
        for k in range(TOP_K):
            slot = slot_smem[t * TOP_K + k]
            pltpu.make_async_copy(hn2_ref.at[pl.ds(t, 1)], xs_ref.at[pl.ds(slot, 1)], sem).start()
    for _ in range(TOP_K):
        pltpu.make_async_copy(hn2_ref, xs_ref.at[pl.ds(0, TM_DISP)], sem).wait()


def _dispatch(slot2d, meta, hn2):
    n_steps = N_TOK // TM_DISP
    return pl.pallas_call(
        _dispatch_kernel,
        out_shape=jax.ShapeDtypeStruct((N_SLOTS, D), F32),
        grid=(n_steps,),
        in_specs=[
            pl.BlockSpec(slot2d.shape, lambda i: (0, 0)),
            pl.BlockSpec(memory_space=pltpu.SMEM),
            pl.BlockSpec((TM_DISP, D), lambda i: (i, 0)),
        ],
        out_specs=pl.BlockSpec(memory_space=pl.ANY),
        scratch_shapes=[pltpu.SMEM((TM_DISP * TOP_K,), jnp.int32), pltpu.VMEM((TM_SLOT, D), F32),
                        pltpu.SemaphoreType.DMA, pltpu.SemaphoreType.DMA, pltpu.SemaphoreType.DMA],
        compiler_params=pltpu.CompilerParams(dimension_semantics=("arbitrary",), vmem_limit_bytes=VMEM_LIMIT),
        name="dispatch",
    )(slot2d, meta, hn2)


def _ffn_kernel(texp_ref, meta_ref, xs_ref, wgu_ref, bgu_ref, wd_ref, bd_ref, ys_ref, wgu_bf, wd_bf):
    i = pl.program_id(0)
    valid = i < meta_ref[META_NT]
    fresh = (i == 0) | (texp_ref[i] != texp_ref[jnp.maximum(i - 1, 0)])

    @pl.when(valid & fresh)
    def _():
        wgu_bf[...] = wgu_ref[0].astype(BF16)
        wd_bf[...] = wd_ref[0].astype(BF16)

    @pl.when(valid)
    def _():
        gu = _dot(xs_ref[...].astype(BF16), wgu_bf[...]) + bgu_ref[0]
        glu = jnp.minimum(gu[:, 0:D_FF], SWIGLU_LIMIT)
        lin = jnp.clip(gu[:, D_FF:2 * D_FF], -SWIGLU_LIMIT, SWIGLU_LIMIT)
        act = glu * _sigmoid(SWIGLU_ALPHA * glu) * (lin + 1.0)
        ys_ref[...] = _dot(act.astype(BF16), wd_bf[...]) + bd_ref[0]

    @pl.when(jnp.logical_not(valid))
    def _():
        ys_ref[...] = jnp.zeros_like(ys_ref)


def _ffn(texp, meta, xs, wgu, bgu, wd, bd):
    def used_tile(i, te, me):
        return (jnp.minimum(i, me[META_NT] - 1), 0)

    return pl.pallas_call(
        _ffn_kernel,
        out_shape=jax.ShapeDtypeStruct((N_SLOTS, D), F32),
        grid_spec=pltpu.PrefetchScalarGridSpec(
            num_scalar_prefetch=2,
            grid=(N_SLOT_TILES,),
            in_specs=[
                pl.BlockSpec((TM_SLOT, D), used_tile),
                pl.BlockSpec((1, D, 2 * D_FF), lambda i, te, me: (te[i], 0, 0)),
                pl.BlockSpec((1, 1, 2 * D_FF), lambda i, te, me: (te[i], 0, 0)),
                pl.BlockSpec((1, D_FF, D), lambda i, te, me: (te[i], 0, 0)),
                pl.BlockSpec((1, 1, D), lambda i, te, me: (te[i], 0, 0)),
            ],
            out_specs=pl.BlockSpec((TM_SLOT, D), lambda i, te, me: (i, 0)),
            scratch_shapes=[pltpu.VMEM((D, 2 * D_FF), BF16), pltpu.VMEM((D_FF, D), BF16)],
        ),
        compiler_params=pltpu.CompilerParams(dimension_semantics=("arbitrary",), vmem_limit_bytes=VMEM_LIMIT),
        name="ffn",
    )(texp, meta, xs, wgu, bgu, wd, bd)


def _combine_kernel(slot_ref, x1_ref, w4_ref, mod_ref, nfw_ref, ys_ref, oc_ref, ol_ref,
                    slot_smem, buf, sem_c, sem):
    _load_slots(slot_ref, slot_smem, sem_c)

    for t in range(TM_DISP):
        for k in range(TOP_K):
            slot = slot_smem[t * TOP_K + k]
            pltpu.make_async_copy(ys_ref.at[pl.ds(slot, 1)], buf.at[k, pl.ds(t, 1)], sem).start()
    for k in range(TOP_K):
        pltpu.make_async_copy(ys_ref.at[pl.ds(0, TM_DISP)], buf.at[k], sem).wait()
    w4 = w4_ref[...]
    moe = w4[:, 0:1] * buf[0]
    for k in range(1, TOP_K):
        moe = moe + w4[:, k:k + 1] * buf[k]
    gate2 = mod_ref[0][:, 5 * D:6 * D]
    x2 = x1_ref[...] + gate2 * moe
    ms = jnp.mean(x2 * x2, axis=-1, keepdims=True)
    out = x2 * lax.rsqrt(ms + RMS_EPS) * nfw_ref[...]
    is_ctx = pl.program_id(0) < N_CTX_TOK // TM_DISP

    @pl.when(is_ctx)
    def _():
        oc_ref[...] = out

    @pl.when(jnp.logical_not(is_ctx))
    def _():
        ol_ref[...] = out


def _combine(slot2d, x1, w4, mod3, nfw, ys):
    n_steps = N_TOK // TM_DISP
    n_ctx = N_CTX_TOK // TM_DISP
    return pl.pallas_call(
        _combine_kernel,
        out_shape=[jax.ShapeDtypeStruct((N_CTX_TOK, D), F32), jax.ShapeDtypeStruct((N_LAT_TOK, D), F32)],
        grid=(n_steps,),
        in_specs=[
            pl.BlockSpec(slot2d.shape, lambda i: (0, 0)),
            pl.BlockSpec((TM_DISP, D), lambda i: (i, 0)),
            pl.BlockSpec((TM_DISP, LANES), lambda i: (i, 0)),
            pl.BlockSpec((1, 1, N_MOD * D), lambda i: (_mod_row(i, TM_DISP), 0, 0)),
            pl.BlockSpec((1, D), lambda i: (0, 0)),
            pl.BlockSpec(memory_space=pl.ANY),
        ],
        out_specs=[pl.BlockSpec((TM_DISP, D), lambda i: (jnp.minimum(i, n_ctx - 1), 0)),
                   pl.BlockSpec((TM_DISP, D), lambda i: (jnp.maximum(i - n_ctx, 0), 0))],
        scratch_shapes=[pltpu.SMEM((TM_DISP * TOP_K,), jnp.int32), pltpu.VMEM((TOP_K, TM_DISP, D), F32),
                        pltpu.SemaphoreType.DMA, pltpu.SemaphoreType.DMA],
        compiler_params=pltpu.CompilerParams(dimension_semantics=("arbitrary",), vmem_limit_bytes=VMEM_LIMIT),
        name="combine",
    )(slot2d, x1, w4, mod3, nfw, ys)


def _pair_blockdiag(s):
    b = s.shape[0]
    s = s.reshape(b, N_PAIR, 2, HEAD, HEAD)
    z = jnp.zeros((b, N_PAIR, HEAD, HEAD), s.dtype)
    top = jnp.concatenate([s[:, :, 0], z], axis=-1)
    bot = jnp.concatenate([z, s[:, :, 1]], axis=-1)
    return jnp.concatenate([top, bot], axis=-2)


def _pair_unblock(s):
    b = s.shape[0]
    a = s[:, :, 0:HEAD, 0:HEAD]
    c = s[:, :, HEAD:, HEAD:]
    return jnp.stack([a, c], axis=2).reshape(b, 1, N_HEAD, HEAD, HEAD)


def kernel(x_prompt, x_sample, state_fwd, state_bwd, c, c_ctx, w_mod, b_mod, norm_mix_w, w_in, b_merge, pool_w, pool_scale, w_pool_out, shift_mu, decay_w0, decay_w2, iclr_a0, iclr_a2, gate_w2, k_k, k_a, r_k, ln_x_w, ln_x_b, w_rwkv_out, w_o, norm_ffn_w, router_w, router_b, expert_w_gu, expert_b_gu, expert_w_down, expert_b_down, norm_final_w):
    l = 0
    x_ctx = x_prompt.reshape(N_CTX_TOK, D)
    x_lat = x_sample.reshape(N_LAT_TOK, D)
    cvec = jnp.concatenate([c_ctx[None, :], c, jnp.zeros((MOD_ROWS - 1 - N_LAT_SEQ, D), F32)], axis=0)
    mod = _modulation(cvec, w_mod[l], b_mod[l][None, :])
    mod3 = mod.reshape(MOD_ROWS, 1, N_MOD * D)

    w_in_l = w_in[l]
    proj = _projection(x_ctx, x_lat, mod3, norm_mix_w[l][None, :], w_in_l[:, :OFF_MERGE].astype(BF16))

    zl = jnp.zeros((LORA, RW), F32)
    dw2 = jnp.stack([jnp.concatenate([decay_w2[l, 0], zl], 0), jnp.concatenate([zl, decay_w2[l, 1]], 0)])
    a2 = jnp.stack([jnp.concatenate([iclr_a2[l, 0], zl], 0), jnp.concatenate([zl, iclr_a2[l, 1]], 0)])
    hid = jnp.arange(RW) // HEAD
    seg = (hid[:, None] == hid[None, :]).astype(BF16)
    front_w = (pool_w[l].astype(BF16), pool_scale[l][None, :], shift_mu[l][None, :],
               decay_w0[l], dw2, iclr_a0[l], a2, k_k[l][None, :], k_a[l][None, :],
               r_k[l].reshape(1, RW), seg)
    f_ctx = _front(proj, False, N_CTX_SEQ, T_CTX, 0, front_w)
    f_lat = _front(proj, True, N_LAT_SEQ, T_LAT, N_CTX_TOK, front_w)

    y_ctx, sf, sb = _scan(f_ctx[2:], N_CTX_SEQ, T_CTX, None)
    init = (_pair_blockdiag(state_fwd[:, l]), _pair_blockdiag(state_bwd[:, l]))
    y_lat, _, _ = _scan(f_lat[2:], N_LAT_SEQ, T_LAT, init)

    rw_pad = jnp.concatenate([router_w[l], jnp.zeros((D, LANES - N_EXP), F32)], axis=1)
    rb_pad = jnp.concatenate([router_b[l], jnp.full((LANES - N_EXP,), -1e30, F32)])[None, :]
    post_w = (norm_mix_w[l][None, :], w_in_l[:, OFF_MERGE:].astype(BF16), b_merge[l][None, :],
              w_pool_out[l].astype(BF16), seg, ln_x_w[l][None, :], ln_x_b[l][None, :],
              gate_w2[l].astype(BF16), w_rwkv_out[l].astype(BF16), w_o[l].astype(BF16),
              norm_ffn_w[l][None, :], rw_pad, rb_pad)
    x1, hn2, code, w4, meta, texp = _post((x_ctx, x_lat), mod3, (f_ctx[0], f_lat[0]), (f_ctx[1], f_lat[1]),
                                          (y_ctx, y_lat), post_w)

    slot2d = _slots(code, meta)[:, :TOP_K].reshape(N_TOK // TM_DISP, TM_DISP * TOP_K)
    meta1 = meta.reshape(LANES)
    xs = _dispatch(slot2d, meta1, hn2)
    ys = _ffn(texp[:, 0], meta1, xs, expert_w_gu[l], expert_b_gu[l][:, None, :],
              expert_w_down[l], expert_b_down[l][:, None, :])
    out_ctx, out_lat = _combine(slot2d, x1, w4, mod3, norm_final_w[None, :], ys)
    y_prompt = out_ctx.reshape(N_CTX_SEQ, T_CTX, D)
    y_sample = out_lat.reshape(N_LAT_SEQ, T_LAT, D)
    return (y_prompt, y_sample, _pair_unblock(sf), _pair_unblock(sb))
```

```python
import functools
import math

import jax
import jax.numpy as jnp
from jax import lax
from jax.experimental import pallas as pl
from jax.experimental.pallas import tpu as pltpu

F32 = jnp.float32
BF16 = jnp.bfloat16

D = 1024
N_CTX_SEQ, T_CTX = 32, 256
N_LAT_SEQ, T_LAT = 4, 1024
GRID_W = 64
N_CTX_TOK = N_CTX_SEQ * T_CTX
N_LAT_TOK = N_LAT_SEQ * T_LAT
N_TOK = N_CTX_TOK + N_LAT_TOK

POOL_W = 512
POOL_G = 4
POOL_GD = POOL_W // POOL_G
POOL_WINDOWS = (2, 4, 8, 16)
RW = 512
HEAD = 64
N_HEAD = RW // HEAD
N_PAIR = N_HEAD // 2
LORA = 64
GATE_LORA = 128
N_EXP = 32
TOP_K = 4
D_FF = 1024
SWIGLU_ALPHA = 1.702
SWIGLU_LIMIT = 7.0
RMS_EPS = 1e-6
GN_EPS = 1e-5 * HEAD
N_MOD = 6

OFF_R = POOL_W
SHIFT_COLS = 3 * RW + 4 * LORA + GATE_LORA
OFF_MERGE = OFF_R + SHIFT_COLS
IN_COLS = OFF_MERGE + 2 * D

LANES = 128
CHUNK = 64
INV_BLOCK = 8
SCAN_TOKENS_PER_STEP = 512
ROWS = 256
HALO = 64
EXT = ROWS + 2 * HALO
TM_PROJ = 512
TM_POST = 512
TM_DISP = 256
TM_SLOT = 512
N_SLOT_TILES = N_TOK * TOP_K // TM_SLOT + N_EXP
N_SLOTS = N_SLOT_TILES * TM_SLOT
RANK_BITS = 14
META_NT = N_EXP
MOD_ROWS = 8
VMEM_LIMIT = 56 * 1024 * 1024


def _sigmoid(x):
    return 1.0 / (1.0 + jnp.exp(-x))


def _split2(a):
    hi = a.astype(BF16)
    lo = (a - hi.astype(F32)).astype(BF16)
    return hi, lo


_NN = (((1,), (0,)), ((), ()))
_NT = (((1,), (1,)), ((), ()))


def _dot(a, b, dims=_NN):
    return lax.dot_general(a, b, dims, preferred_element_type=F32)


def _dot3(a, b, dims=_NN):
    ah, al = _split2(a)
    bh, bl = _split2(b)
    return _dot(ah, bh, dims) + (_dot(ah, bl, dims) + _dot(al, bh, dims))


def _dot_exact_lhs(a_bf16, b, passes):
    acc = None
    rem = b
    for _ in range(passes):
        part = rem.astype(BF16)
        term = _dot(a_bf16, part)
        acc = term if acc is None else acc + term
        rem = rem - part.astype(F32)
    return acc


def _dot_exact_rhs(a, b_bf16, passes):
    acc = None
    rem = a
    for _ in range(passes):
        part = rem.astype(BF16)
        term = _dot(part, b_bf16)
        acc = term if acc is None else acc + term
        rem = rem - part.astype(F32)
    return acc


def _modnorm(x, w, scale, shift):
    ms = jnp.mean(x * x, axis=-1, keepdims=True)
    return x * lax.rsqrt(ms + RMS_EPS) * w * (1.0 + scale) + shift


def _mod_row(i, tm):
    n_ctx = N_CTX_TOK // tm
    per = T_LAT // tm
    return jnp.where(i < n_ctx, 0, 1 + (i - n_ctx) // per)


def _mod_kernel(c_ref, w_ref, b_ref, o_ref):
    c = c_ref[...]
    s = c * _sigmoid(c)
    o_ref[...] = _dot3(s, w_ref[...]) + b_ref[...]


def _modulation(cvec, w_mod, b_mod):
    return pl.pallas_call(
        _mod_kernel,
        out_shape=jax.ShapeDtypeStruct((MOD_ROWS, N_MOD * D), F32),
        grid=(N_MOD,),
        in_specs=[
            pl.BlockSpec((MOD_ROWS, D), lambda j: (0, 0)),
            pl.BlockSpec((D, D), lambda j: (0, j)),
            pl.BlockSpec((1, D), lambda j: (0, j)),
        ],
        out_specs=pl.BlockSpec((MOD_ROWS, D), lambda j: (0, j)),
        compiler_params=pltpu.CompilerParams(dimension_semantics=("arbitrary",), vmem_limit_bytes=VMEM_LIMIT),
        name="mod",
    )(cvec, w_mod, b_mod)


def _path_specs(tm, n_cols=D):
    n_ctx = N_CTX_TOK // tm
    return [pl.BlockSpec((tm, n_cols), lambda i: (jnp.minimum(i, n_ctx - 1), 0)),
            pl.BlockSpec((tm, n_cols), lambda i: (jnp.maximum(i - n_ctx, 0), 0))]


def _path_tile(c_ref, l_ref, tm):
    return jnp.where(pl.program_id(0) < N_CTX_TOK // tm, c_ref[...], l_ref[...])


def _proj_kernel(xc_ref, xl_ref, mod_ref, nw_ref, w_ref, o_ref):
    mod = mod_ref[0]
    hn = _modnorm(_path_tile(xc_ref, xl_ref, TM_PROJ), nw_ref[...], mod[:, D:2 * D], mod[:, 0:D])
    o_ref[...] = _dot(hn.astype(BF16), w_ref[...])


def _projection(x_ctx, x_lat, mod3, norm_w, w_a):
    n_cols = w_a.shape[1]
    return pl.pallas_call(
        _proj_kernel,
        out_shape=jax.ShapeDtypeStruct((N_TOK, n_cols), F32),
        grid=(N_TOK // TM_PROJ,),
        in_specs=_path_specs(TM_PROJ) + [
            pl.BlockSpec((1, 1, N_MOD * D), lambda i: (_mod_row(i, TM_PROJ), 0, 0)),
            pl.BlockSpec((1, D), lambda i: (0, 0)),
            pl.BlockSpec((D, n_cols), lambda i: (0, 0)),
        ],
        out_specs=pl.BlockSpec((TM_PROJ, n_cols), lambda i: (i, 0)),
        compiler_params=pltpu.CompilerParams(dimension_semantics=("arbitrary",), vmem_limit_bytes=VMEM_LIMIT),
        name="proj",
    )(x_ctx, x_lat, mod3, norm_w, w_a)


N_FRONT_IN = 14
N_FRONT_OUT = 12


def _front_kernel(grid_mode, n_chunks, seq_len, *refs):
    (cur_ref, prev_ref, next_ref, poolw_ref, pscale_ref, mu_ref,
     dw0_ref, dw2_ref, a0_ref, a2_ref, kk_ref, ka_ref, rk_ref, seg_ref) = refs[:N_FRONT_IN]
    (ypre_ref, gsig_ref, r_ref, v_ref, kkn_ref, bonus_ref,
     lwf_ref, lwb_ref, kdf_ref, kdb_ref, bf_ref, bb_ref) = refs[-N_FRONT_OUT:]
    c = pl.program_id(1)
    cur = cur_ref[...]
    prev = jnp.where(c > 0, prev_ref[...], 0.0)
    nxt = jnp.where(c < n_chunks - 1, next_ref[...], 0.0)
    ext = jnp.concatenate([prev, cur, nxt], axis=0)

    def down(x, s):
        return pltpu.roll(x, s, 0)

    def up(x, s):
        return pltpu.roll(x, EXT - s, 0)

    def mid(x):
        return x[HALO:HALO + ROWS]

    row = lax.broadcasted_iota(jnp.int32, (ROWS, 1), 0)
    t_seq = c * ROWS + row

    outs = []
    for gi, win in enumerate(POOL_WINDOWS):
        h = win // 2
        u = ext[:, gi * POOL_GD:(gi + 1) * POOL_GD]
        trail, lead, s = u, u, 1
        while s < h:
            trail = trail + down(trail, s)
            lead = lead + up(lead, s)
            s *= 2
        total = mid(down(trail, 1) + lead)
        cnt = (jnp.minimum(t_seq + h, seq_len) - jnp.maximum(t_seq - h, 0)).astype(F32)
        pooled = total / cnt - mid(u)
        outs.append(_dot(pooled.astype(BF16), poolw_ref[gi]))
    mixed = jnp.concatenate(outs, axis=1) * pscale_ref[...]
    ypre_ref[...] = mixed.astype(BF16)

    pe = ext[:, OFF_R:OFF_MERGE]
    p = mid(pe)
    lane = lax.broadcasted_iota(jnp.int32, (1, SHIFT_COLS), 1)
    if grid_mode:
        col = t_seq % GRID_W
        left = jnp.where(col > 0, mid(down(pe, 1)), 0.0)
        right = jnp.where(col < GRID_W - 1, mid(up(pe, 1)), 0.0)
        upn = pe[0:ROWS]
        dnn = pe[2 * HALO:2 * HALO + ROWS]
        q = lane % 4
        sh = jnp.where(q == 0, left, jnp.where(q == 1, right, jnp.where(q == 2, upn, dnn)))
    else:
        sh = jnp.where(lane % 2 == 0, mid(down(pe, 1)), mid(up(pe, 1)))
    pm = p + mu_ref[...] * (sh - p)

    r = pm[:, 0:RW]
    k = pm[:, RW:2 * RW]
    v = pm[:, 2 * RW:3 * RW]
    o = 3 * RW
    w_lo = jnp.tanh(pm[:, o:o + 2 * LORA])
    a_lo = pm[:, o + 2 * LORA:o + 4 * LORA]
    gsig_ref[...] = _sigmoid(pm[:, o + 4 * LORA:o + 4 * LORA + GATE_LORA]).astype(BF16)

    seg = seg_ref[...]
    kkr = k * kk_ref[...]
    ss = _dot_exact_rhs(kkr * kkr, seg, 2)
    kkn = kkr / jnp.maximum(jnp.sqrt(ss), 1e-12)
    r_ref[...] = r
    v_ref[...] = v
    kkn_ref[...] = kkn

    bonus = None
    for d, (lw_ref, kd_ref, b_ref) in enumerate(((lwf_ref, kdf_ref, bf_ref), (lwb_ref, kdb_ref, bb_ref))):
        z = dw0_ref[d:d + 1, :] + _dot3(w_lo, dw2_ref[d])
        lw_ref[...] = -math.exp(-0.5) * _sigmoid(z)
        a = _sigmoid(a0_ref[d:d + 1, :] + _dot3(a_lo, a2_ref[d]))
        kd = k * (1.0 + (a - 1.0) * ka_ref[...])
        kd_ref[...] = kd
        b_ref[...] = kkn * a
        bd = _dot_exact_rhs(r * kd * rk_ref[...], seg, 2) * v
        bonus = bd if bonus is None else bonus + bd
    bonus_ref[...] = bonus


def _front(proj, grid_mode, n_seq, seq_len, tok0, weights):
    n_chunks = seq_len // ROWS
    n_tok = n_seq * seq_len
    blk0 = tok0 // ROWS
    hpb = ROWS // HALO
    last_halo = N_TOK // HALO - 1

    def cur_map(b, c):
        return (blk0 + b * n_chunks + c, 0)

    def prev_map(b, c):
        return (jnp.maximum((blk0 + b * n_chunks + c) * hpb - 1, 0), 0)

    def next_map(b, c):
        return (jnp.minimum((blk0 + b * n_chunks + c + 1) * hpb, last_halo), 0)

    def full(a):
        nd = a.ndim
        return pl.BlockSpec(a.shape, lambda b, c, _nd=nd: (0,) * _nd)

    def out_map(b, c):
        return (b * n_chunks + c, 0)

    f32_out = jax.ShapeDtypeStruct((n_tok, RW), F32)
    out_shape = [jax.ShapeDtypeStruct((n_tok, POOL_W), BF16), jax.ShapeDtypeStruct((n_tok, GATE_LORA), BF16)]
    out_shape += [f32_out] * (N_FRONT_OUT - 2)
    out_specs = [pl.BlockSpec((ROWS, POOL_W), out_map), pl.BlockSpec((ROWS, GATE_LORA), out_map)]
    out_specs += [pl.BlockSpec((ROWS, RW), out_map)] * (N_FRONT_OUT - 2)
    assert len(weights) == N_FRONT_IN - 3
    return pl.pallas_call(
        functools.partial(_front_kernel, grid_mode, n_chunks, seq_len),
        out_shape=out_shape,
        grid=(n_seq, n_chunks),
        in_specs=[
            pl.BlockSpec((ROWS, OFF_MERGE), cur_map),
            pl.BlockSpec((HALO, OFF_MERGE), prev_map),
            pl.BlockSpec((HALO, OFF_MERGE), next_map),
        ] + [full(a) for a in weights],
        out_specs=out_specs,
        compiler_params=pltpu.CompilerParams(dimension_semantics=("arbitrary", "arbitrary"),
                                             vmem_limit_bytes=VMEM_LIMIT),
        name="front_grid" if grid_mode else "front_ctx",
    )(proj, proj, proj, *weights)


def _scan_kernel(n_chunks, seqs_per_step, has_init, *refs):
    (r_ref, v_ref, kk_ref, bonus_ref, lwf_ref, lwb_ref, kdf_ref, kdb_ref, bf_ref, bb_ref) = refs[:10]
    if has_init:
        s0f_ref, s0b_ref = refs[10:12]
    y_ref, sf_ref, sb_ref = refs[-3:]
    C = CHUNK
    P2 = 2 * C
    ri = lax.broadcasted_iota(jnp.int32, (P2, P2), 0)
    ci = lax.broadcasted_iota(jnp.int32, (P2, P2), 1)
    same = (ri // C) == (ci // C)
    rp, cp = ri % C, ci % C
    eye = (ri == ci).astype(F32)
    blk = {}
    s = INV_BLOCK
    while s <= C:
        blk[s] = (ri // s) == (ci // s)
        s *= 2
    ti = lax.broadcasted_iota(jnp.int32, (C, C), 0)
    tj = lax.broadcasted_iota(jnp.int32, (C, C), 1)
    head_a = lax.broadcasted_iota(jnp.int32, (C, LANES), 1) < HEAD

    def stack(x):
        return jnp.concatenate([jnp.where(head_a, x, 0.0), jnp.where(head_a, 0.0, x)], axis=0)

    def d3(a, b, dims=_NN):
        return _dot(a[0], b[0], dims) + (_dot(a[0], b[1], dims) + _dot(a[1], b[0], dims))

    def chunks(chains):
        n = range(len(chains))
        masks, sls = [], []
        for (s_ref, q, p, row0, lw_ref, kd_ref, b_ref, reverse) in chains:
            if reverse:
                masks.append((same & (cp > rp), same & (cp >= rp), (tj >= ti), 0))
            else:
                masks.append((same & (cp < rp), same & (cp <= rp), (tj <= ti), C - 1))
            sls.append((pl.ds(row0, C), slice(p * LANES, (p + 1) * LANES)))
        lw = [chains[i][4][sls[i]] for i in n]
        cum = [_dot_exact_lhs(masks[i][2].astype(BF16), lw[i], 3) for i in n]
        e_i = [jnp.exp(cum[i]) for i in n]
        e_n = [jnp.exp(-cum[i]) for i in n]
        QR = [jnp.concatenate([stack(kk_ref[sls[i]] * jnp.exp(cum[i] - lw[i])),
                               stack(r_ref[sls[i]] * e_i[i])], axis=0).astype(BF16) for i in n]
        BK = [jnp.concatenate([stack(chains[i][6][sls[i]] * e_n[i]),
                               stack(chains[i][5][sls[i]] * e_n[i])], axis=0).astype(BF16) for i in n]
        v_st = [stack(v_ref[sls[i]]) for i in n]
        Vs = [v_st[i].astype(BF16) for i in n]
        S = [chains[i][0][chains[i][1], chains[i][2]] for i in n]
        G = [_dot(QR[i], BK[i], _NT) for i in n]
        L = [jnp.where(masks[i][0], G[i][0:P2, 0:P2], 0.0) for i in n]
        Lk = [jnp.where(masks[i][0], G[i][0:P2, P2:2 * P2], 0.0) for i in n]
        RBK = [jnp.concatenate([jnp.where(masks[i][1], G[i][P2:2 * P2, 0:P2], 0.0),
                                jnp.where(masks[i][1], G[i][P2:2 * P2, P2:2 * P2], 0.0)], axis=1) for i in n]
        L0 = [_split2(jnp.where(blk[INV_BLOCK], L[i], 0.0)) for i in n]
        X = [eye - jnp.where(blk[INV_BLOCK], L[i], 0.0) for i in n]
        Pw = L0
        s = 2
        while s < INV_BLOCK:
            Pw = [_split2(d3(Pw[i], Pw[i])) for i in n]
            X = [X[i] + d3(_split2(X[i]), Pw[i]) for i in n]
            s *= 2
        s = INV_BLOCK
        while s < C:
            Dm = [_split2(X[i]) for i in n]
            E = [d3(_split2(jnp.where(blk[2 * s] & jnp.logical_not(blk[s]), L[i], 0.0)), Dm[i]) for i in n]
            X = [X[i] - d3(Dm[i], _split2(E[i])) for i in n]
            s *= 2
        QRA = [_dot(QR[i], S[i].astype(BF16), _NT) for i in n]
        W = [QRA[i][0:P2] + _dot(Lk[i].astype(BF16), Vs[i]) for i in n]
        U = [-_dot(X[i].astype(BF16), W[i].astype(BF16)) for i in n]
        UV = [jnp.concatenate([U[i], v_st[i]], axis=0) for i in n]
        Ys = [QRA[i][P2:2 * P2] + _dot(RBK[i].astype(BF16), UV[i].astype(BF16)) for i in n]
        dS = [_dot(UV[i].T.astype(BF16), BK[i]) for i in n]
        for i in n:
            last = masks[i][3]
            chains[i][0][chains[i][1], chains[i][2]] = (S[i] + dS[i]) * e_i[i][last:last + 1, :]
            y_ref[sls[i]] += Ys[i][0:C] + Ys[i][C:P2]

    y_ref[...] = bonus_ref[...]
    if has_init:
        sf_ref[...] = s0f_ref[...]
        sb_ref[...] = s0b_ref[...]
    else:
        sf_ref[...] = jnp.zeros_like(sf_ref)
        sb_ref[...] = jnp.zeros_like(sb_ref)

    seq_len = n_chunks * C

    def body(c, carry):
        chains = []
        for q in range(seqs_per_step):
            rf = pl.multiple_of(q * seq_len + c * C, C)
            rb = pl.multiple_of(q * seq_len + (n_chunks - 1 - c) * C, C)
            for p in range(N_PAIR):
                chains.append((sf_ref, q, p, rf, lwf_ref, kdf_ref, bf_ref, False))
                chains.append((sb_ref, q, p, rb, lwb_ref, kdb_ref, bb_ref, True))
        chunks(chains)
        return carry

    lax.fori_loop(0, n_chunks, body, 0)


def _scan(arrs, n_seq, seq_len, init):
    n_tok = n_seq * seq_len
    n_chunks = seq_len // CHUNK
    sps = max(1, SCAN_TOKENS_PER_STEP // seq_len)
    tok_spec = pl.BlockSpec((sps * seq_len, RW), lambda b: (b, 0))
    st_spec = pl.BlockSpec((sps, N_PAIR, LANES, LANES), lambda b: (b, 0, 0, 0))
    st_shape = jax.ShapeDtypeStruct((n_seq, N_PAIR, LANES, LANES), F32)
    ins = list(arrs)
    in_specs = [tok_spec] * 10
    if init is not None:
        ins += list(init)
        in_specs += [st_spec, st_spec]
    return pl.pallas_call(
        functools.partial(_scan_kernel, n_chunks, sps, init is not None),
        out_shape=[jax.ShapeDtypeStruct((n_tok, RW), F32), st_shape, st_shape],
        grid=(n_seq // sps,),
        in_specs=in_specs,
        out_specs=[tok_spec, st_spec, st_spec],
        compiler_params=pltpu.CompilerParams(dimension_semantics=("arbitrary",), vmem_limit_bytes=VMEM_LIMIT),
        name="scan_init" if init is not None else "scan_zero",
    )(*ins)


def _post_kernel(xc_ref, xl_ref, mod_ref, yprec_ref, yprel_ref, gsigc_ref, gsigl_ref, yc_ref, yl_ref,
                 nmw_ref, wm_ref, bm_ref, wpo_ref, seg_ref, lnw_ref, lnb_ref, gw2_ref, wro_ref, wo_ref,
                 nfw_ref, rw_ref, rb_ref,
                 x1_ref, hn2_ref, code_ref, w4_ref, meta_ref, texp_ref, carry_ref):
    i = pl.program_id(0)

    @pl.when(i == 0)
    def _():
        carry_ref[...] = jnp.zeros_like(carry_ref)

    x = _path_tile(xc_ref, xl_ref, TM_POST)
    mod = mod_ref[0]
    shift1, scale1, gate1 = mod[:, 0:D], mod[:, D:2 * D], mod[:, 2 * D:3 * D]
    shift2, scale2 = mod[:, 3 * D:4 * D], mod[:, 4 * D:5 * D]
    hn = _modnorm(x, nmw_ref[...], scale1, shift1)
    merge = _sigmoid(_dot(hn.astype(BF16), wm_ref[...]) + bm_ref[...])
    y_pool = _dot(_path_tile(yprec_ref, yprel_ref, TM_POST), wpo_ref[...])

    y = _path_tile(yc_ref, yl_ref, TM_POST)
    seg = seg_ref[...]
    mu = _dot_exact_rhs(y, seg, 2) * (1.0 / HEAD)
    yc = y - mu
    var = _dot_exact_rhs(yc * yc, seg, 2) * (1.0 / HEAD)
    yn = yc * lax.rsqrt(var + GN_EPS) * lnw_ref[...] + lnb_ref[...]
    g = _dot(_path_tile(gsigc_ref, gsigl_ref, TM_POST), gw2_ref[...])
    y_rwkv = _dot((yn * g).astype(BF16), wro_ref[...])

    mixed = merge[:, 0:D] * y_pool + merge[:, D:2 * D] * y_rwkv
    x1 = x + gate1 * _dot(mixed.astype(BF16), wo_ref[...])
    x1_ref[...] = x1
    hn2 = _modnorm(x1, nfw_ref[...], scale2, shift2)
    hn2_ref[...] = hn2

    logits = _dot3(hn2, rw_ref[...]) + rb_ref[...]
    lane = lax.broadcasted_iota(jnp.int32, logits.shape, 1)
    work = logits
    sel = None
    top = None
    for j in range(TOP_K):
        m = jnp.max(work, axis=-1, keepdims=True)
        if j == 0:
            top = m
        idx = jnp.min(jnp.where(work == m, lane, LANES), axis=-1, keepdims=True)
        pick = lane == idx
        sel = pick if sel is None else (sel | pick)
        work = jnp.where(pick, -jnp.inf, work)
    e = jnp.where(sel, jnp.exp(logits - top), 0.0)
    comb = e / jnp.sum(e, axis=-1, keepdims=True)

    tm = logits.shape[0]
    sel_b = jnp.where(sel, 1.0, 0.0).astype(BF16)
    before = (lax.broadcasted_iota(jnp.int32, (tm, tm), 1) < lax.broadcasted_iota(jnp.int32, (tm, tm), 0))
    rank = carry_ref[...] + _dot(before.astype(BF16), sel_b)
    carry = carry_ref[...] + jnp.sum(sel_b.astype(F32), axis=0, keepdims=True)
    carry_ref[...] = carry
    lower_e = (lax.broadcasted_iota(jnp.int32, (LANES, LANES), 0) < lax.broadcasted_iota(jnp.int32, (LANES, LANES), 1))
    rowpos = _dot(sel_b, lower_e.astype(BF16))
    code = jnp.zeros(logits.shape, jnp.int32)
    w4 = jnp.zeros(logits.shape, F32)
    for k in range(TOP_K):
        mk = sel & (rowpos == float(k))
        ek = jnp.sum(jnp.where(mk, lane, 0), axis=-1, keepdims=True)
        rk = jnp.sum(jnp.where(mk, rank, 0.0), axis=-1, keepdims=True).astype(jnp.int32)
        wk = jnp.sum(jnp.where(mk, comb, 0.0), axis=-1, keepdims=True)
        code = jnp.where(lane == k, ek * (1 << RANK_BITS) + rk, code)
        w4 = jnp.where(lane == k, wk, w4)
    code_ref[...] = code
    w4_ref[...] = w4

    tiles = jnp.floor((carry + (TM_SLOT - 1)) * (1.0 / TM_SLOT))
    incl = (lax.broadcasted_iota(jnp.int32, (LANES, LANES), 0) <= lax.broadcasted_iota(jnp.int32, (LANES, LANES), 1))
    tiles8 = jnp.broadcast_to(tiles, (8, LANES)).astype(BF16)
    cum = _dot(tiles8, incl.astype(BF16))[0:1]
    lane1 = lax.broadcasted_iota(jnp.int32, (1, LANES), 1)
    offs = ((cum - tiles) * TM_SLOT).astype(jnp.int32)
    n_used = jnp.sum(jnp.where(lane1 == N_EXP - 1, cum, 0.0), axis=-1, keepdims=True).astype(jnp.int32)
    meta_ref[...] = jnp.where(lane1 == META_NT, n_used, jnp.where(lane1 < N_EXP, offs, 0))
    tile_id = lax.broadcasted_iota(jnp.int32, (N_SLOT_TILES, LANES), 0).astype(F32)
    done = jnp.where((lax.broadcasted_iota(jnp.int32, (N_SLOT_TILES, LANES), 1) < N_EXP) & (cum <= tile_id), 1, 0)
    texp = jnp.minimum(jnp.sum(done, axis=-1, keepdims=True), N_EXP - 1)
    texp_ref[...] = jnp.broadcast_to(texp, (N_SLOT_TILES, LANES))


def _post(x, mod3, ypre, gsig, y, weights):
    def tile(n, dt):
        return pl.BlockSpec((TM_POST, n), lambda i: (i, 0))

    def full(a):
        nd = a.ndim
        return pl.BlockSpec(a.shape, lambda i, _nd=nd: (0,) * _nd)

    def const(shape):
        return pl.BlockSpec(shape, lambda i: (0, 0))

    return pl.pallas_call(
        _post_kernel,
        out_shape=[jax.ShapeDtypeStruct((N_TOK, D), F32), jax.ShapeDtypeStruct((N_TOK, D), F32),
                   jax.ShapeDtypeStruct((N_TOK, LANES), jnp.int32), jax.ShapeDtypeStruct((N_TOK, LANES), F32),
                   jax.ShapeDtypeStruct((1, LANES), jnp.int32),
                   jax.ShapeDtypeStruct((N_SLOT_TILES, LANES), jnp.int32)],
        grid=(N_TOK // TM_POST,),
        in_specs=_path_specs(TM_POST) + [
            pl.BlockSpec((1, 1, N_MOD * D), lambda i: (_mod_row(i, TM_POST), 0, 0)),
        ] + _path_specs(TM_POST, POOL_W) + _path_specs(TM_POST, GATE_LORA) + _path_specs(TM_POST, RW)
        + [full(a) for a in weights],
        out_specs=[tile(D, F32), tile(D, F32), tile(LANES, jnp.int32), tile(LANES, F32),
                   const((1, LANES)), const((N_SLOT_TILES, LANES))],
        scratch_shapes=[pltpu.VMEM((1, LANES), F32)],
        compiler_params=pltpu.CompilerParams(dimension_semantics=("arbitrary",), vmem_limit_bytes=VMEM_LIMIT),
        name="post",
    )(*x, mod3, *ypre, *gsig, *y, *weights)


def _slots_kernel(code_ref, meta_ref, slot_ref):
    code = code_ref[...]
    meta = meta_ref[...]
    lane = lax.broadcasted_iota(jnp.int32, code.shape, 1)
    expert = code >> RANK_BITS
    rank = code & ((1 << RANK_BITS) - 1)
    slot = jnp.zeros(code.shape, jnp.int32)
    for k in range(TOP_K):
        ek = jnp.sum(jnp.where(lane == k, expert, 0), axis=-1, keepdims=True)
        off = jnp.sum(jnp.where(lane == ek, meta, 0), axis=-1, keepdims=True)
        slot = jnp.where(lane == k, off + rank, slot)
    slot_ref[...] = slot


def _slots(code, meta):
    tm = 1024
    return pl.pallas_call(
        _slots_kernel,
        out_shape=jax.ShapeDtypeStruct((N_TOK, LANES), jnp.int32),
        grid=(N_TOK // tm,),
        in_specs=[pl.BlockSpec((tm, LANES), lambda i: (i, 0)), pl.BlockSpec((1, LANES), lambda i: (0, 0))],
        out_specs=pl.BlockSpec((tm, LANES), lambda i: (i, 0)),
        compiler_params=pltpu.CompilerParams(dimension_semantics=("arbitrary",), vmem_limit_bytes=VMEM_LIMIT),
        name="slots",
    )(code, meta)


def _load_slots(slot_ref, slot_smem, sem):
    cp = pltpu.make_async_copy(slot_ref.at[pl.program_id(0)], slot_smem, sem)
    cp.start()
    cp.wait()


def _dispatch_kernel(slot_ref, meta_ref, hn2_ref, xs_ref, slot_smem, zero_buf, sem_c, sem_z, sem):
    @pl.when(pl.program_id(0) == 0)
    def _():
        zero_buf[...] = jnp.zeros_like(zero_buf)

        def pad_tile(e):
            end = meta_ref[e + 1] if e + 1 < N_EXP else meta_ref[META_NT] * TM_SLOT
            start = pl.multiple_of(end - TM_SLOT, TM_SLOT)
            return end > meta_ref[e], pltpu.make_async_copy(zero_buf, xs_ref.at[pl.ds(start, TM_SLOT)], sem_z)

        def tail_tile(j):
            return pltpu.make_async_copy(zero_buf, xs_ref.at[pl.ds(pl.multiple_of(j * TM_SLOT, TM_SLOT), TM_SLOT)], sem_z)

        n_used = meta_ref[META_NT]
        for e in range(N_EXP):
            nonempty, cp = pad_tile(e)
            pl.when(nonempty)(cp.start)

        def start_tail(j, c):
            tail_tile(j).start()
            return c

        def wait_tail(j, c):
            tail_tile(j).wait()
            return c

        lax.fori_loop(n_used, N_SLOT_TILES, start_tail, 0)
        for e in range(N_EXP):
            nonempty, cp = pad_tile(e)
            pl.when(nonempty)(cp.wait)
        lax.fori_loop(n_used, N_SLOT_TILES, wait_tail, 0)

    _load_slots(slot_ref, slot_smem, sem_c)

    for t in range(TM_DISP):
        for k in range(TOP_K):
            slot = slot_smem[t * TOP_K + k]
            pltpu.make_async_copy(hn2_ref.at[pl.ds(t, 1)], xs_ref.at[pl.ds(slot, 1)], sem).start(priority=k % 2)
    for _ in range(TOP_K):
        pltpu.make_async_copy(hn2_ref, xs_ref.at[pl.ds(0, TM_DISP)], sem).wait()


def _dispatch(slot2d, meta, hn2):
    n_steps = N_TOK // TM_DISP
    return pl.pallas_call(
        _dispatch_kernel,
        out_shape=jax.ShapeDtypeStruct((N_SLOTS, D), F32),
        grid=(n_steps,),
        in_specs=[
            pl.BlockSpec(slot2d.shape, lambda i: (0, 0)),
            pl.BlockSpec(memory_space=pltpu.SMEM),
            pl.BlockSpec((TM_DISP, D), lambda i: (i, 0)),
        ],
        out_specs=pl.BlockSpec(memory_space=pl.ANY),
        scratch_shapes=[pltpu.SMEM((TM_DISP * TOP_K,), jnp.int32), pltpu.VMEM((TM_SLOT, D), F32),
                        pltpu.SemaphoreType.DMA, pltpu.SemaphoreType.DMA, pltpu.SemaphoreType.DMA],
        compiler_params=pltpu.CompilerParams(dimension_semantics=("arbitrary",), vmem_limit_bytes=VMEM_LIMIT),
        name="dispatch",
    )(slot2d, meta, hn2)


def _ffn_kernel(texp_ref, meta_ref, xs_ref, wgu_ref, bgu_ref, wd_ref, bd_ref, ys_ref, wgu_bf, wd_bf):
    i = pl.program_id(0)
    valid = i < meta_ref[META_NT]
    fresh = (i == 0) | (texp_ref[i] != texp_ref[jnp.maximum(i - 1, 0)])

    @pl.when(valid & fresh)
    def _():
        wgu_bf[...] = wgu_ref[0].astype(BF16)
        wd_bf[...] = wd_ref[0].astype(BF16)

    @pl.when(valid)
    def _():
        gu = _dot(xs_ref[...].astype(BF16), wgu_bf[...]) + bgu_ref[0]
        glu = jnp.minimum(gu[:, 0:D_FF], SWIGLU_LIMIT)
        lin = jnp.clip(gu[:, D_FF:2 * D_FF], -SWIGLU_LIMIT, SWIGLU_LIMIT)
        act = glu * _sigmoid(SWIGLU_ALPHA * glu) * (lin + 1.0)
        ys_ref[...] = _dot(act.astype(BF16), wd_bf[...]) + bd_ref[0]

    @pl.when(jnp.logical_not(valid))
    def _():
        ys_ref[...] = jnp.zeros_like(ys_ref)


def _ffn(texp, meta, xs, wgu, bgu, wd, bd):
    def used_tile(i, te, me):
        return (jnp.minimum(i, me[META_NT] - 1), 0)

    return pl.pallas_call(
        _ffn_kernel,
        out_shape=jax.ShapeDtypeStruct((N_SLOTS, D), F32),
        grid_spec=pltpu.PrefetchScalarGridSpec(
            num_scalar_prefetch=2,
            grid=(N_SLOT_TILES,),
            in_specs=[
                pl.BlockSpec((TM_SLOT, D), used_tile),
                pl.BlockSpec((1, D, 2 * D_FF), lambda i, te, me: (te[i], 0, 0)),
                pl.BlockSpec((1, 1, 2 * D_FF), lambda i, te, me: (te[i], 0, 0)),
                pl.BlockSpec((1, D_FF, D), lambda i, te, me: (te[i], 0, 0)),
                pl.BlockSpec((1, 1, D), lambda i, te, me: (te[i], 0, 0)),
            ],
            out_specs=pl.BlockSpec((TM_SLOT, D), lambda i, te, me: (i, 0)),
            scratch_shapes=[pltpu.VMEM((D, 2 * D_FF), BF16), pltpu.VMEM((D_FF, D), BF16)],
        ),
        compiler_params=pltpu.CompilerParams(dimension_semantics=("arbitrary",), vmem_limit_bytes=VMEM_LIMIT),
        name="ffn",
    )(texp, meta, xs, wgu, bgu, wd, bd)


def _combine_kernel(slot_ref, x1_ref, w4_ref, mod_ref, nfw_ref, ys_ref, oc_ref, ol_ref,
                    slot_smem, buf, sem_c, sem):
    _load_slots(slot_ref, slot_smem, sem_c)

    for t in range(TM_DISP):
        for k in range(TOP_K):
            slot = slot_smem[t * TOP_K + k]
            pltpu.make_async_copy(ys_ref.at[pl.ds(slot, 1)], buf.at[k, pl.ds(t, 1)], sem).start(priority=k % 2)
    for k in range(TOP_K):
        pltpu.make_async_copy(ys_ref.at[pl.ds(0, TM_DISP)], buf.at[k], sem).wait()
    w4 = w4_ref[...]
    moe = w4[:, 0:1] * buf[0]
    for k in range(1, TOP_K):
        moe = moe + w4[:, k:k + 1] * buf[k]
    gate2 = mod_ref[0][:, 5 * D:6 * D]
    x2 = x1_ref[...] + gate2 * moe
    ms = jnp.mean(x2 * x2, axis=-1, keepdims=True)
    out = x2 * lax.rsqrt(ms + RMS_EPS) * nfw_ref[...]
    is_ctx = pl.program_id(0) < N_CTX_TOK // TM_DISP

    @pl.when(is_ctx)
    def _():
        oc_ref[...] = out

    @pl.when(jnp.logical_not(is_ctx))
    def _():
        ol_ref[...] = out


def _combine(slot2d, x1, w4, mod3, nfw, ys):
    n_steps = N_TOK // TM_DISP
    n_ctx = N_CTX_TOK // TM_DISP
    return pl.pallas_call(
        _combine_kernel,
        out_shape=[jax.ShapeDtypeStruct((N_CTX_TOK, D), F32), jax.ShapeDtypeStruct((N_LAT_TOK, D), F32)],
        grid=(n_steps,),
        in_specs=[
            pl.BlockSpec(slot2d.shape, lambda i: (0, 0)),
            pl.BlockSpec((TM_DISP, D), lambda i: (i, 0)),
            pl.BlockSpec((TM_DISP, LANES), lambda i: (i, 0)),
            pl.BlockSpec((1, 1, N_MOD * D), lambda i: (_mod_row(i, TM_DISP), 0, 0)),
            pl.BlockSpec((1, D), lambda i: (0, 0)),
            pl.BlockSpec(memory_space=pl.ANY),
        ],
        out_specs=[pl.BlockSpec((TM_DISP, D), lambda i: (jnp.minimum(i, n_ctx - 1), 0)),
                   pl.BlockSpec((TM_DISP, D), lambda i: (jnp.maximum(i - n_ctx, 0), 0))],
        scratch_shapes=[pltpu.SMEM((TM_DISP * TOP_K,), jnp.int32), pltpu.VMEM((TOP_K, TM_DISP, D), F32),
                        pltpu.SemaphoreType.DMA, pltpu.SemaphoreType.DMA],
        compiler_params=pltpu.CompilerParams(dimension_semantics=("arbitrary",), vmem_limit_bytes=VMEM_LIMIT),
        name="combine",
    )(slot2d, x1, w4, mod3, nfw, ys)


def _pair_blockdiag(s):
    b = s.shape[0]
    s = s.reshape(b, N_PAIR, 2, HEAD, HEAD)
    z = jnp.zeros((b, N_PAIR, HEAD, HEAD), s.dtype)
    top = jnp.concatenate([s[:, :, 0], z], axis=-1)
    bot = jnp.concatenate([z, s[:, :, 1]], axis=-1)
    return jnp.concatenate([top, bot], axis=-2)


def _pair_unblock(s):
    b = s.shape[0]
    a = s[:, :, 0:HEAD, 0:HEAD]
    c = s[:, :, HEAD:, HEAD:]
    return jnp.stack([a, c], axis=2).reshape(b, 1, N_HEAD, HEAD, HEAD)


def kernel(x_prompt, x_sample, state_fwd, state_bwd, c, c_ctx, w_mod, b_mod, norm_mix_w, w_in, b_merge, pool_w, pool_scale, w_pool_out, shift_mu, decay_w0, decay_w2, iclr_a0, iclr_a2, gate_w2, k_k, k_a, r_k, ln_x_w, ln_x_b, w_rwkv_out, w_o, norm_ffn_w, router_w, router_b, expert_w_gu, expert_b_gu, expert_w_down, expert_b_down, norm_final_w):
    l = 0
    x_ctx = x_prompt.reshape(N_CTX_TOK, D)
    x_lat = x_sample.reshape(N_LAT_TOK, D)
    cvec = jnp.concatenate([c_ctx[None, :], c, jnp.zeros((MOD_ROWS - 1 - N_LAT_SEQ, D), F32)], axis=0)
    mod = _modulation(cvec, w_mod[l], b_mod[l][None, :])
    mod3 = mod.reshape(MOD_ROWS, 1, N_MOD * D)

    w_in_l = w_in[l]
    proj = _projection(x_ctx, x_lat, mod3, norm_mix_w[l][None, :], w_in_l[:, :OFF_MERGE].astype(BF16))

    zl = jnp.zeros((LORA, RW), F32)
    dw2 = jnp.stack([jnp.concatenate([decay_w2[l, 0], zl], 0), jnp.concatenate([zl, decay_w2[l, 1]], 0)])
    a2 = jnp.stack([jnp.concatenate([iclr_a2[l, 0], zl], 0), jnp.concatenate([zl, iclr_a2[l, 1]], 0)])
    hid = jnp.arange(RW) // HEAD
    seg = (hid[:, None] == hid[None, :]).astype(BF16)
    front_w = (pool_w[l].astype(BF16), pool_scale[l][None, :], shift_mu[l][None, :],
               decay_w0[l], dw2, iclr_a0[l], a2, k_k[l][None, :], k_a[l][None, :],
               r_k[l].reshape(1, RW), seg)
    f_ctx = _front(proj, False, N_CTX_SEQ, T_CTX, 0, front_w)
    f_lat = _front(proj, True, N_LAT_SEQ, T_LAT, N_CTX_TOK, front_w)

    y_ctx, sf, sb = _scan(f_ctx[2:], N_CTX_SEQ, T_CTX, None)
    init = (_pair_blockdiag(state_fwd[:, l]), _pair_blockdiag(state_bwd[:, l]))
    y_lat, _, _ = _scan(f_lat[2:], N_LAT_SEQ, T_LAT, init)

    rw_pad = jnp.concatenate([router_w[l], jnp.zeros((D, LANES - N_EXP), F32)], axis=1)
    rb_pad = jnp.concatenate([router_b[l], jnp.full((LANES - N_EXP,), -1e30, F32)])[None, :]
    post_w = (norm_mix_w[l][None, :], w_in_l[:, OFF_MERGE:].astype(BF16), b_merge[l][None, :],
              w_pool_out[l].astype(BF16), seg, ln_x_w[l][None, :], ln_x_b[l][None, :],
              gate_w2[l].astype(BF16), w_rwkv_out[l].astype(BF16), w_o[l].astype(BF16),
              norm_ffn_w[l][None, :], rw_pad, rb_pad)
    x1, hn2, code, w4, meta, texp = _post((x_ctx, x_lat), mod3, (f_ctx[0], f_lat[0]), (f_ctx[1], f_lat[1]),
                                          (y_ctx, y_lat), post_w)

    slot2d = _slots(code, meta)[:, :TOP_K].reshape(N_TOK // TM_DISP, TM_DISP * TOP_K)
    meta1 = meta.reshape(LANES)
    xs = _dispatch(slot2d, meta1, hn2)
    ys = _ffn(texp[:, 0], meta1, xs, expert_w_gu[l], expert_b_gu[l][:, None, :],
              expert_w_down[l], expert_b_down[l][:, None, :])
    out_ctx, out_lat = _combine(slot2d, x1, w4, mod3, norm_final_w[None, :], ys)
    y_prompt = out_ctx.reshape(N_CTX_SEQ, T_CTX, D)
    y_sample = out_lat.reshape(N_LAT_SEQ, T_LAT, D)
    return (y_prompt, y_sample, _pair_unblock(sf), _pair_unblock(sb))
```

```python
import functools
import math

import jax
import jax.numpy as jnp
from jax import lax
from jax.experimental import pallas as pl
from jax.experimental.pallas import tpu as pltpu

F32 = jnp.float32
BF16 = jnp.bfloat16

D = 1024
N_CTX_SEQ, T_CTX = 32, 256
N_LAT_SEQ, T_LAT = 4, 1024
GRID_W = 64
N_CTX_TOK = N_CTX_SEQ * T_CTX
N_LAT_TOK = N_LAT_SEQ * T_LAT
N_TOK = N_CTX_TOK + N_LAT_TOK

POOL_W = 512
POOL_G = 4
POOL_GD = POOL_W // POOL_G
POOL_WINDOWS = (2, 4, 8, 16)
RW = 512
HEAD = 64
N_HEAD = RW // HEAD
N_PAIR = N_HEAD // 2
LORA = 64
GATE_LORA = 128
N_EXP = 32
TOP_K = 4
D_FF = 1024
SWIGLU_ALPHA = 1.702
SWIGLU_LIMIT = 7.0
RMS_EPS = 1e-6
GN_EPS = 1e-5 * HEAD
N_MOD = 6

OFF_R = POOL_W
SHIFT_COLS = 3 * RW + 4 * LORA + GATE_LORA
OFF_MERGE = OFF_R + SHIFT_COLS
IN_COLS = OFF_MERGE + 2 * D

LANES = 128
CHUNK = 64
INV_BLOCK = 8
SCAN_TOKENS_PER_STEP = 512
ROWS = 256
HALO = 64
EXT = ROWS + 2 * HALO
TM_PROJ = 512
TM_POST = 512
TM_DISP = 256
TM_SLOT = 512
N_SLOT_TILES = N_TOK * TOP_K // TM_SLOT + N_EXP
N_SLOTS = N_SLOT_TILES * TM_SLOT
RANK_BITS = 14
META_NT = N_EXP
MOD_ROWS = 8
VMEM_LIMIT = 56 * 1024 * 1024


def _sigmoid(x):
    return 1.0 / (1.0 + jnp.exp(-x))


def _split2(a):
    hi = a.astype(BF16)
    lo = (a - hi.astype(F32)).astype(BF16)
    return hi, lo


_NN = (((1,), (0,)), ((), ()))
_NT = (((1,), (1,)), ((), ()))


def _dot(a, b, dims=_NN):
    return lax.dot_general(a, b, dims, preferred_element_type=F32)


def _dot3(a, b, dims=_NN):
    ah, al = _split2(a)
    bh, bl = _split2(b)
    return _dot(ah, bh, dims) + (_dot(ah, bl, dims) + _dot(al, bh, dims))


def _dot_exact_lhs(a_bf16, b, passes):
    acc = None
    rem = b
    for _ in range(passes):
        part = rem.astype(BF16)
        term = _dot(a_bf16, part)
        acc = term if acc is None else acc + term
        rem = rem - part.astype(F32)
    return acc


def _dot_exact_rhs(a, b_bf16, passes):
    acc = None
    rem = a
    for _ in range(passes):
        part = rem.astype(BF16)
        term = _dot(part, b_bf16)
        acc = term if acc is None else acc + term
        rem = rem - part.astype(F32)
    return acc


def _modnorm(x, w, scale, shift):
    ms = jnp.mean(x * x, axis=-1, keepdims=True)
    return x * lax.rsqrt(ms + RMS_EPS) * w * (1.0 + scale) + shift


def _mod_row(i, tm):
    n_ctx = N_CTX_TOK // tm
    per = T_LAT // tm
    return jnp.where(i < n_ctx, 0, 1 + (i - n_ctx) // per)


def _mod_kernel(c_ref, w_ref, b_ref, o_ref):
    c = c_ref[...]
    s = c * _sigmoid(c)
    o_ref[...] = _dot3(s, w_ref[...]) + b_ref[...]


def _modulation(cvec, w_mod, b_mod):
    return pl.pallas_call(
        _mod_kernel,
        out_shape=jax.ShapeDtypeStruct((MOD_ROWS, N_MOD * D), F32),
        grid=(N_MOD,),
        in_specs=[
            pl.BlockSpec((MOD_ROWS, D), lambda j: (0, 0)),
            pl.BlockSpec((D, D), lambda j: (0, j)),
            pl.BlockSpec((1, D), lambda j: (0, j)),
        ],
        out_specs=pl.BlockSpec((MOD_ROWS, D), lambda j: (0, j)),
        compiler_params=pltpu.CompilerParams(dimension_semantics=("arbitrary",), vmem_limit_bytes=VMEM_LIMIT),
        name="mod",
    )(cvec, w_mod, b_mod)


def _path_specs(tm, n_cols=D):
    n_ctx = N_CTX_TOK // tm
    return [pl.BlockSpec((tm, n_cols), lambda i: (jnp.minimum(i, n_ctx - 1), 0)),
            pl.BlockSpec((tm, n_cols), lambda i: (jnp.maximum(i - n_ctx, 0), 0))]


def _path_tile(c_ref, l_ref, tm):
    return jnp.where(pl.program_id(0) < N_CTX_TOK // tm, c_ref[...], l_ref[...])


def _proj_kernel(xc_ref, xl_ref, mod_ref, nw_ref, w_ref, o_ref):
    mod = mod_ref[0]
    hn = _modnorm(_path_tile(xc_ref, xl_ref, TM_PROJ), nw_ref[...], mod[:, D:2 * D], mod[:, 0:D])
    o_ref[...] = _dot(hn.astype(BF16), w_ref[...])


def _projection(x_ctx, x_lat, mod3, norm_w, w_a):
    n_cols = w_a.shape[1]
    return pl.pallas_call(
        _proj_kernel,
        out_shape=jax.ShapeDtypeStruct((N_TOK, n_cols), F32),
        grid=(N_TOK // TM_PROJ,),
        in_specs=_path_specs(TM_PROJ) + [
            pl.BlockSpec((1, 1, N_MOD * D), lambda i: (_mod_row(i, TM_PROJ), 0, 0)),
            pl.BlockSpec((1, D), lambda i: (0, 0)),
            pl.BlockSpec((D, n_cols), lambda i: (0, 0)),
        ],
        out_specs=pl.BlockSpec((TM_PROJ, n_cols), lambda i: (i, 0)),
        compiler_params=pltpu.CompilerParams(dimension_semantics=("arbitrary",), vmem_limit_bytes=VMEM_LIMIT),
        name="proj",
    )(x_ctx, x_lat, mod3, norm_w, w_a)


N_FRONT_IN = 14
N_FRONT_OUT = 12


def _front_kernel(grid_mode, n_chunks, seq_len, *refs):
    (cur_ref, prev_ref, next_ref, poolw_ref, pscale_ref, mu_ref,
     dw0_ref, dw2_ref, a0_ref, a2_ref, kk_ref, ka_ref, rk_ref, seg_ref) = refs[:N_FRONT_IN]
    (ypre_ref, gsig_ref, r_ref, v_ref, kkn_ref, bonus_ref,
     lwf_ref, lwb_ref, kdf_ref, kdb_ref, bf_ref, bb_ref) = refs[-N_FRONT_OUT:]
    c = pl.program_id(1)
    cur = cur_ref[...]
    prev = jnp.where(c > 0, prev_ref[...], 0.0)
    nxt = jnp.where(c < n_chunks - 1, next_ref[...], 0.0)
    ext = jnp.concatenate([prev, cur, nxt], axis=0)

    def down(x, s):
        return pltpu.roll(x, s, 0)

    def up(x, s):
        return pltpu.roll(x, EXT - s, 0)

    def mid(x):
        return x[HALO:HALO + ROWS]

    row = lax.broadcasted_iota(jnp.int32, (ROWS, 1), 0)
    t_seq = c * ROWS + row

    outs = []
    for gi, win in enumerate(POOL_WINDOWS):
        h = win // 2
        u = ext[:, gi * POOL_GD:(gi + 1) * POOL_GD]
        trail, lead, s = u, u, 1
        while s < h:
            trail = trail + down(trail, s)
            lead = lead + up(lead, s)
            s *= 2
        total = mid(down(trail, 1) + lead)
        cnt = (jnp.minimum(t_seq + h, seq_len) - jnp.maximum(t_seq - h, 0)).astype(F32)
        pooled = total / cnt - mid(u)
        outs.append(_dot(pooled.astype(BF16), poolw_ref[gi]))
    mixed = jnp.concatenate(outs, axis=1) * pscale_ref[...]
    ypre_ref[...] = mixed.astype(BF16)

    pe = ext[:, OFF_R:OFF_MERGE]
    p = mid(pe)
    lane = lax.broadcasted_iota(jnp.int32, (1, SHIFT_COLS), 1)
    if grid_mode:
        col = t_seq % GRID_W
        left = jnp.where(col > 0, mid(down(pe, 1)), 0.0)
        right = jnp.where(col < GRID_W - 1, mid(up(pe, 1)), 0.0)
        upn = pe[0:ROWS]
        dnn = pe[2 * HALO:2 * HALO + ROWS]
        q = lane % 4
        sh = jnp.where(q == 0, left, jnp.where(q == 1, right, jnp.where(q == 2, upn, dnn)))
    else:
        sh = jnp.where(lane % 2 == 0, mid(down(pe, 1)), mid(up(pe, 1)))
    pm = p + mu_ref[...] * (sh - p)

    r = pm[:, 0:RW]
    k = pm[:, RW:2 * RW]
    v = pm[:, 2 * RW:3 * RW]
    o = 3 * RW
    w_lo = jnp.tanh(pm[:, o:o + 2 * LORA])
    a_lo = pm[:, o + 2 * LORA:o + 4 * LORA]
    gsig_ref[...] = _sigmoid(pm[:, o + 4 * LORA:o + 4 * LORA + GATE_LORA]).astype(BF16)

    seg = seg_ref[...]
    kkr = k * kk_ref[...]
    ss = _dot_exact_rhs(kkr * kkr, seg, 2)
    kkn = kkr / jnp.maximum(jnp.sqrt(ss), 1e-12)
    r_ref[...] = r
    v_ref[...] = v
    kkn_ref[...] = kkn

    bonus = None
    for d, (lw_ref, kd_ref, b_ref) in enumerate(((lwf_ref, kdf_ref, bf_ref), (lwb_ref, kdb_ref, bb_ref))):
        z = dw0_ref[d:d + 1, :] + _dot3(w_lo, dw2_ref[d])
        lw_ref[...] = -math.exp(-0.5) * _sigmoid(z)
        a = _sigmoid(a0_ref[d:d + 1, :] + _dot3(a_lo, a2_ref[d]))
        kd = k * (1.0 + (a - 1.0) * ka_ref[...])
        kd_ref[...] = kd
        b_ref[...] = kkn * a
        bd = _dot_exact_rhs(r * kd * rk_ref[...], seg, 2) * v
        bonus = bd if bonus is None else bonus + bd
    bonus_ref[...] = bonus


def _front(proj, grid_mode, n_seq, seq_len, tok0, weights):
    n_chunks = seq_len // ROWS
    n_tok = n_seq * seq_len
    blk0 = tok0 // ROWS
    hpb = ROWS // HALO
    last_halo = N_TOK // HALO - 1

    def cur_map(b, c):
        return (blk0 + b * n_chunks + c, 0)

    def prev_map(b, c):
        return (jnp.maximum((blk0 + b * n_chunks + c) * hpb - 1, 0), 0)

    def next_map(b, c):
        return (jnp.minimum((blk0 + b * n_chunks + c + 1) * hpb, last_halo), 0)

    def full(a):
        nd = a.ndim
        return pl.BlockSpec(a.shape, lambda b, c, _nd=nd: (0,) * _nd)

    def out_map(b, c):
        return (b * n_chunks + c, 0)

    f32_out = jax.ShapeDtypeStruct((n_tok, RW), F32)
    out_shape = [jax.ShapeDtypeStruct((n_tok, POOL_W), BF16), jax.ShapeDtypeStruct((n_tok, GATE_LORA), BF16)]
    out_shape += [f32_out] * (N_FRONT_OUT - 2)
    out_specs = [pl.BlockSpec((ROWS, POOL_W), out_map), pl.BlockSpec((ROWS, GATE_LORA), out_map)]
    out_specs += [pl.BlockSpec((ROWS, RW), out_map)] * (N_FRONT_OUT - 2)
    assert len(weights) == N_FRONT_IN - 3
    return pl.pallas_call(
        functools.partial(_front_kernel, grid_mode, n_chunks, seq_len),
        out_shape=out_shape,
        grid=(n_seq, n_chunks),
        in_specs=[
            pl.BlockSpec((ROWS, OFF_MERGE), cur_map),
            pl.BlockSpec((HALO, OFF_MERGE), prev_map),
            pl.BlockSpec((HALO, OFF_MERGE), next_map),
        ] + [full(a) for a in weights],
        out_specs=out_specs,
        compiler_params=pltpu.CompilerParams(dimension_semantics=("arbitrary", "arbitrary"),
                                             vmem_limit_bytes=VMEM_LIMIT),
        name="front_grid" if grid_mode else "front_ctx",
    )(proj, proj, proj, *weights)


def _scan_kernel(n_chunks, seqs_per_step, has_init, *refs):
    (r_ref, v_ref, kk_ref, bonus_ref, lwf_ref, lwb_ref, kdf_ref, kdb_ref, bf_ref, bb_ref) = refs[:10]
    if has_init:
        s0f_ref, s0b_ref = refs[10:12]
    y_ref, sf_ref, sb_ref = refs[-3:]
    C = CHUNK
    P2 = 2 * C
    ri = lax.broadcasted_iota(jnp.int32, (P2, P2), 0)
    ci = lax.broadcasted_iota(jnp.int32, (P2, P2), 1)
    same = (ri // C) == (ci // C)
    rp, cp = ri % C, ci % C
    eye = (ri == ci).astype(F32)
    blk = {}
    s = INV_BLOCK
    while s <= C:
        blk[s] = (ri // s) == (ci // s)
        s *= 2
    ti = lax.broadcasted_iota(jnp.int32, (C, C), 0)
    tj = lax.broadcasted_iota(jnp.int32, (C, C), 1)
    head_a = lax.broadcasted_iota(jnp.int32, (C, LANES), 1) < HEAD

    def stack(x):
        return jnp.concatenate([jnp.where(head_a, x, 0.0), jnp.where(head_a, 0.0, x)], axis=0)

    def d3(a, b, dims=_NN):
        return _dot(a[0], b[0], dims) + (_dot(a[0], b[1], dims) + _dot(a[1], b[0], dims))

    def d2(a_bf16, b):
        return _dot(a_bf16, b[0]) + _dot(a_bf16, b[1])

    def chunks(chains):
        n = range(len(chains))
        masks, sls = [], []
        for (s_ref, q, p, row0, lw_ref, kd_ref, b_ref, reverse) in chains:
            if reverse:
                masks.append((same & (cp > rp), same & (cp >= rp), (tj >= ti), 0))
            else:
                masks.append((same & (cp < rp), same & (cp <= rp), (tj <= ti), C - 1))
            sls.append((pl.ds(row0, C), slice(p * LANES, (p + 1) * LANES)))
        lw = [chains[i][4][sls[i]] for i in n]
        cum = [_dot_exact_lhs(masks[i][2].astype(BF16), lw[i], 3) for i in n]
        e_i = [jnp.exp(cum[i]) for i in n]
        e_n = [jnp.exp(-cum[i]) for i in n]
        QR = [jnp.concatenate([stack(kk_ref[sls[i]] * jnp.exp(cum[i] - lw[i])),
                               stack(r_ref[sls[i]] * e_i[i])], axis=0).astype(BF16) for i in n]
        BK = [jnp.concatenate([stack(chains[i][6][sls[i]] * e_n[i]),
                               stack(chains[i][5][sls[i]] * e_n[i])], axis=0).astype(BF16) for i in n]
        v_st = [stack(v_ref[sls[i]]) for i in n]
        Vs = [v_st[i].astype(BF16) for i in n]
        S = [chains[i][0][chains[i][1], chains[i][2]] for i in n]
        G = [_dot(QR[i], BK[i], _NT) for i in n]
        L = [jnp.where(masks[i][0], G[i][0:P2, 0:P2], 0.0) for i in n]
        Lk = [jnp.where(masks[i][0], G[i][0:P2, P2:2 * P2], 0.0) for i in n]
        RBK = [jnp.concatenate([jnp.where(masks[i][1], G[i][P2:2 * P2, 0:P2], 0.0),
                                jnp.where(masks[i][1], G[i][P2:2 * P2, P2:2 * P2], 0.0)], axis=1) for i in n]
        L0 = [jnp.where(blk[INV_BLOCK], L[i], 0.0) for i in n]
        L0h = [L0[i].astype(BF16) for i in n]
        X = [eye - L0[i] for i in n]
        Pw = [_split2(_dot(L0h[i], L0h[i])) for i in n]
        X = [X[i] + d2(X[i].astype(BF16), Pw[i]) for i in n]
        s = 4
        while s < INV_BLOCK:
            Pw = [_split2(d3(Pw[i], Pw[i])) for i in n]
            X = [X[i] + d3(_split2(X[i]), Pw[i]) for i in n]
            s *= 2
        s = INV_BLOCK
        while s < C:
            Dm = [_split2(X[i]) for i in n]
            Loff = [jnp.where(blk[2 * s] & jnp.logical_not(blk[s]), L[i], 0.0).astype(BF16) for i in n]
            E = [d2(Loff[i], Dm[i]) for i in n]
            X = [X[i] - d3(Dm[i], _split2(E[i])) for i in n]
            s *= 2
        QRA = [_dot(QR[i], S[i].astype(BF16), _NT) for i in n]
        W = [QRA[i][0:P2] + _dot(Lk[i].astype(BF16), Vs[i]) for i in n]
        U = [-_dot(X[i].astype(BF16), W[i].astype(BF16)) for i in n]
        UV = [jnp.concatenate([U[i], v_st[i]], axis=0) for i in n]
        Ys = [QRA[i][P2:2 * P2] + _dot(RBK[i].astype(BF16), UV[i].astype(BF16)) for i in n]
        dS = [_dot(UV[i].T.astype(BF16), BK[i]) for i in n]
        for i in n:
            last = masks[i][3]
            chains[i][0][chains[i][1], chains[i][2]] = (S[i] + dS[i]) * e_i[i][last:last + 1, :]
            y_ref[sls[i]] += Ys[i][0:C] + Ys[i][C:P2]

    y_ref[...] = bonus_ref[...]
    if has_init:
        sf_ref[...] = s0f_ref[...]
        sb_ref[...] = s0b_ref[...]
    else:
        sf_ref[...] = jnp.zeros_like(sf_ref)
        sb_ref[...] = jnp.zeros_like(sb_ref)

    seq_len = n_chunks * C

    def body(c, carry):
        chains = []
        for q in range(seqs_per_step):
            rf = pl.multiple_of(q * seq_len + c * C, C)
            rb = pl.multiple_of(q * seq_len + (n_chunks - 1 - c) * C, C)
            for p in range(N_PAIR):
                chains.append((sf_ref, q, p, rf, lwf_ref, kdf_ref, bf_ref, False))
                chains.append((sb_ref, q, p, rb, lwb_ref, kdb_ref, bb_ref, True))
        chunks(chains)
        return carry

    lax.fori_loop(0, n_chunks, body, 0)


def _scan(arrs, n_seq, seq_len, init):
    n_tok = n_seq * seq_len
    n_chunks = seq_len // CHUNK
    sps = max(1, SCAN_TOKENS_PER_STEP // seq_len)
    tok_spec = pl.BlockSpec((sps * seq_len, RW), lambda b: (b, 0))
    st_spec = pl.BlockSpec((sps, N_PAIR, LANES, LANES), lambda b: (b, 0, 0, 0))
    st_shape = jax.ShapeDtypeStruct((n_seq, N_PAIR, LANES, LANES), F32)
    ins = list(arrs)
    in_specs = [tok_spec] * 10
    if init is not None:
        ins += list(init)
        in_specs += [st_spec, st_spec]
    return pl.pallas_call(
        functools.partial(_scan_kernel, n_chunks, sps, init is not None),
        out_shape=[jax.ShapeDtypeStruct((n_tok, RW), F32), st_shape, st_shape],
        grid=(n_seq // sps,),
        in_specs=in_specs,
        out_specs=[tok_spec, st_spec, st_spec],
        compiler_params=pltpu.CompilerParams(dimension_semantics=("arbitrary",), vmem_limit_bytes=VMEM_LIMIT),
        name="scan_init" if init is not None else "scan_zero",
    )(*ins)


def _post_kernel(xc_ref, xl_ref, mod_ref, yprec_ref, yprel_ref, gsigc_ref, gsigl_ref, yc_ref, yl_ref,
                 nmw_ref, wm_ref, bm_ref, wpo_ref, seg_ref, lnw_ref, lnb_ref, gw2_ref, wro_ref, wo_ref,
                 nfw_ref, rw_ref, rb_ref,
                 x1_ref, hn2_ref, code_ref, w4_ref, meta_ref, texp_ref, carry_ref):
    i = pl.program_id(0)

    @pl.when(i == 0)
    def _():
        carry_ref[...] = jnp.zeros_like(carry_ref)

    x = _path_tile(xc_ref, xl_ref, TM_POST)
    mod = mod_ref[0]
    shift1, scale1, gate1 = mod[:, 0:D], mod[:, D:2 * D], mod[:, 2 * D:3 * D]
    shift2, scale2 = mod[:, 3 * D:4 * D], mod[:, 4 * D:5 * D]
    hn = _modnorm(x, nmw_ref[...], scale1, shift1)
    merge = _sigmoid(_dot(hn.astype(BF16), wm_ref[...]) + bm_ref[...])
    y_pool = _dot(_path_tile(yprec_ref, yprel_ref, TM_POST), wpo_ref[...])

    y = _path_tile(yc_ref, yl_ref, TM_POST)
    seg = seg_ref[...]
    mu = _dot_exact_rhs(y, seg, 2) * (1.0 / HEAD)
    yc = y - mu
    var = _dot_exact_rhs(yc * yc, seg, 2) * (1.0 / HEAD)
    yn = yc * lax.rsqrt(var + GN_EPS) * lnw_ref[...] + lnb_ref[...]
    g = _dot(_path_tile(gsigc_ref, gsigl_ref, TM_POST), gw2_ref[...])
    y_rwkv = _dot((yn * g).astype(BF16), wro_ref[...])

    mixed = merge[:, 0:D] * y_pool + merge[:, D:2 * D] * y_rwkv
    x1 = x + gate1 * _dot(mixed.astype(BF16), wo_ref[...])
    x1_ref[...] = x1
    hn2 = _modnorm(x1, nfw_ref[...], scale2, shift2)
    hn2_ref[...] = hn2

    logits = _dot3(hn2, rw_ref[...]) + rb_ref[...]
    lane = lax.broadcasted_iota(jnp.int32, logits.shape, 1)
    work = logits
    sel = None
    top = None
    for j in range(TOP_K):
        m = jnp.max(work, axis=-1, keepdims=True)
        if j == 0:
            top = m
        idx = jnp.min(jnp.where(work == m, lane, LANES), axis=-1, keepdims=True)
        pick = lane == idx
        sel = pick if sel is None else (sel | pick)
        work = jnp.where(pick, -jnp.inf, work)
    e = jnp.where(sel, jnp.exp(logits - top), 0.0)
    comb = e / jnp.sum(e, axis=-1, keepdims=True)

    tm = logits.shape[0]
    sel_b = jnp.where(sel, 1.0, 0.0).astype(BF16)
    before = (lax.broadcasted_iota(jnp.int32, (tm, tm), 1) < lax.broadcasted_iota(jnp.int32, (tm, tm), 0))
    rank = carry_ref[...] + _dot(before.astype(BF16), sel_b)
    carry = carry_ref[...] + jnp.sum(sel_b.astype(F32), axis=0, keepdims=True)
    carry_ref[...] = carry
    lower_e = (lax.broadcasted_iota(jnp.int32, (LANES, LANES), 0) < lax.broadcasted_iota(jnp.int32, (LANES, LANES), 1))
    rowpos = _dot(sel_b, lower_e.astype(BF16))
    code = jnp.zeros(logits.shape, jnp.int32)
    w4 = jnp.zeros(logits.shape, F32)
    for k in range(TOP_K):
        mk = sel & (rowpos == float(k))
        ek = jnp.sum(jnp.where(mk, lane, 0), axis=-1, keepdims=True)
        rk = jnp.sum(jnp.where(mk, rank, 0.0), axis=-1, keepdims=True).astype(jnp.int32)
        wk = jnp.sum(jnp.where(mk, comb, 0.0), axis=-1, keepdims=True)
        code = jnp.where(lane == k, ek * (1 << RANK_BITS) + rk, code)
        w4 = jnp.where(lane == k, wk, w4)
    code_ref[...] = code
    w4_ref[...] = w4

    tiles = jnp.floor((carry + (TM_SLOT - 1)) * (1.0 / TM_SLOT))
    incl = (lax.broadcasted_iota(jnp.int32, (LANES, LANES), 0) <= lax.broadcasted_iota(jnp.int32, (LANES, LANES), 1))
    tiles8 = jnp.broadcast_to(tiles, (8, LANES)).astype(BF16)
    cum = _dot(tiles8, incl.astype(BF16))[0:1]
    lane1 = lax.broadcasted_iota(jnp.int32, (1, LANES), 1)
    offs = ((cum - tiles) * TM_SLOT).astype(jnp.int32)
    n_used = jnp.sum(jnp.where(lane1 == N_EXP - 1, cum, 0.0), axis=-1, keepdims=True).astype(jnp.int32)
    meta_ref[...] = jnp.where(lane1 == META_NT, n_used, jnp.where(lane1 < N_EXP, offs, 0))
    tile_id = lax.broadcasted_iota(jnp.int32, (N_SLOT_TILES, LANES), 0).astype(F32)
    done = jnp.where((lax.broadcasted_iota(jnp.int32, (N_SLOT_TILES, LANES), 1) < N_EXP) & (cum <= tile_id), 1, 0)
    texp = jnp.minimum(jnp.sum(done, axis=-1, keepdims=True), N_EXP - 1)
    texp_ref[...] = jnp.broadcast_to(texp, (N_SLOT_TILES, LANES))


def _post(x, mod3, ypre, gsig, y, weights):
    def tile(n, dt):
        return pl.BlockSpec((TM_POST, n), lambda i: (i, 0))

    def full(a):
        nd = a.ndim
        return pl.BlockSpec(a.shape, lambda i, _nd=nd: (0,) * _nd)

    def const(shape):
        return pl.BlockSpec(shape, lambda i: (0, 0))

    return pl.pallas_call(
        _post_kernel,
        out_shape=[jax.ShapeDtypeStruct((N_TOK, D), F32), jax.ShapeDtypeStruct((N_TOK, D), F32),
                   jax.ShapeDtypeStruct((N_TOK, LANES), jnp.int32), jax.ShapeDtypeStruct((N_TOK, LANES), F32),
                   jax.ShapeDtypeStruct((1, LANES), jnp.int32),
                   jax.ShapeDtypeStruct((N_SLOT_TILES, LANES), jnp.int32)],
        grid=(N_TOK // TM_POST,),
        in_specs=_path_specs(TM_POST) + [
            pl.BlockSpec((1, 1, N_MOD * D), lambda i: (_mod_row(i, TM_POST), 0, 0)),
        ] + _path_specs(TM_POST, POOL_W) + _path_specs(TM_POST, GATE_LORA) + _path_specs(TM_POST, RW)
        + [full(a) for a in weights],
        out_specs=[tile(D, F32), tile(D, F32), tile(LANES, jnp.int32), tile(LANES, F32),
                   const((1, LANES)), const((N_SLOT_TILES, LANES))],
        scratch_shapes=[pltpu.VMEM((1, LANES), F32)],
        compiler_params=pltpu.CompilerParams(dimension_semantics=("arbitrary",), vmem_limit_bytes=VMEM_LIMIT),
        name="post",
    )(*x, mod3, *ypre, *gsig, *y, *weights)


def _slot_of(code, meta_ref):
    return meta_ref[code >> RANK_BITS] + (code & ((1 << RANK_BITS) - 1))


def _load_codes(code_ref, code_smem, sem):
    cp = pltpu.make_async_copy(code_ref.at[pl.program_id(0)], code_smem, sem)
    cp.start()
    cp.wait()


def _dispatch_kernel(code_ref, meta_ref, hn2_ref, xs_ref, code_smem, zero_buf, sem_c, sem_z, sem):
    @pl.when(pl.program_id(0) == 0)
    def _():
        zero_buf[...] = jnp.zeros_like(zero_buf)

        def pad_tile(e):
            end = meta_ref[e + 1] if e + 1 < N_EXP else meta_ref[META_NT] * TM_SLOT
            start = pl.multiple_of(end - TM_SLOT, TM_SLOT)
            return end > meta_ref[e], pltpu.make_async_copy(zero_buf, xs_ref.at[pl.ds(start, TM_SLOT)], sem_z)

        def tail_tile(j):
            return pltpu.make_async_copy(zero_buf, xs_ref.at[pl.ds(pl.multiple_of(j * TM_SLOT, TM_SLOT), TM_SLOT)], sem_z)

        n_used = meta_ref[META_NT]
        for e in range(N_EXP):
            nonempty, cp = pad_tile(e)
            pl.when(nonempty)(cp.start)

        def start_tail(j, c):
            tail_tile(j).start()
            return c

        def wait_tail(j, c):
            tail_tile(j).wait()
            return c

        lax.fori_loop(n_used, N_SLOT_TILES, start_tail, 0)
        for e in range(N_EXP):
            nonempty, cp = pad_tile(e)
            pl.when(nonempty)(cp.wait)
        lax.fori_loop(n_used, N_SLOT_TILES, wait_tail, 0)

    _load_codes(code_ref, code_smem, sem_c)

    for t in range(TM_DISP):
        for k in range(TOP_K):
            slot = _slot_of(code_smem[t * TOP_K + k], meta_ref)
            pltpu.make_async_copy(hn2_ref.at[pl.ds(t, 1)], xs_ref.at[pl.ds(slot, 1)], sem).start(priority=k % 2)
    for _ in range(TOP_K):
        pltpu.make_async_copy(hn2_ref, xs_ref.at[pl.ds(0, TM_DISP)], sem).wait()


def _dispatch(code2d, meta, hn2):
    n_steps = N_TOK // TM_DISP
    return pl.pallas_call(
        _dispatch_kernel,
        out_shape=jax.ShapeDtypeStruct((N_SLOTS, D), F32),
        grid=(n_steps,),
        in_specs=[
            pl.BlockSpec(code2d.shape, lambda i: (0, 0)),
            pl.BlockSpec(memory_space=pltpu.SMEM),
            pl.BlockSpec((TM_DISP, D), lambda i: (i, 0)),
        ],
        out_specs=pl.BlockSpec(memory_space=pl.ANY),
        scratch_shapes=[pltpu.SMEM((TM_DISP * TOP_K,), jnp.int32), pltpu.VMEM((TM_SLOT, D), F32),
                        pltpu.SemaphoreType.DMA, pltpu.SemaphoreType.DMA, pltpu.SemaphoreType.DMA],
        compiler_params=pltpu.CompilerParams(dimension_semantics=("arbitrary",), vmem_limit_bytes=VMEM_LIMIT),
        name="dispatch",
    )(code2d, meta, hn2)


def _ffn_kernel(texp_ref, meta_ref, xs_ref, wgu_ref, bgu_ref, wd_ref, bd_ref, ys_ref, wgu_bf, wd_bf):
    i = pl.program_id(0)
    valid = i < meta_ref[META_NT]
    fresh = (i == 0) | (texp_ref[i] != texp_ref[jnp.maximum(i - 1, 0)])

    @pl.when(valid & fresh)
    def _():
        wgu_bf[...] = wgu_ref[0].astype(BF16)
        wd_bf[...] = wd_ref[0].astype(BF16)

    @pl.when(valid)
    def _():
        gu = _dot(xs_ref[...].astype(BF16), wgu_bf[...]) + bgu_ref[0]
        glu = jnp.minimum(gu[:, 0:D_FF], SWIGLU_LIMIT)
        lin = jnp.clip(gu[:, D_FF:2 * D_FF], -SWIGLU_LIMIT, SWIGLU_LIMIT)
        act = glu * _sigmoid(SWIGLU_ALPHA * glu) * (lin + 1.0)
        ys_ref[...] = _dot(act.astype(BF16), wd_bf[...]) + bd_ref[0]

    @pl.when(jnp.logical_not(valid))
    def _():
        ys_ref[...] = jnp.zeros_like(ys_ref)


def _ffn(texp, meta, xs, wgu, bgu, wd, bd):
    def used_tile(i, te, me):
        return (jnp.minimum(i, me[META_NT] - 1), 0)

    return pl.pallas_call(
        _ffn_kernel,
        out_shape=jax.ShapeDtypeStruct((N_SLOTS, D), F32),
        grid_spec=pltpu.PrefetchScalarGridSpec(
            num_scalar_prefetch=2,
            grid=(N_SLOT_TILES,),
            in_specs=[
                pl.BlockSpec((TM_SLOT, D), used_tile),
                pl.BlockSpec((1, D, 2 * D_FF), lambda i, te, me: (te[i], 0, 0)),
                pl.BlockSpec((1, 1, 2 * D_FF), lambda i, te, me: (te[i], 0, 0)),
                pl.BlockSpec((1, D_FF, D), lambda i, te, me: (te[i], 0, 0)),
                pl.BlockSpec((1, 1, D), lambda i, te, me: (te[i], 0, 0)),
            ],
            out_specs=pl.BlockSpec((TM_SLOT, D), lambda i, te, me: (i, 0)),
            scratch_shapes=[pltpu.VMEM((D, 2 * D_FF), BF16), pltpu.VMEM((D_FF, D), BF16)],
        ),
        compiler_params=pltpu.CompilerParams(dimension_semantics=("arbitrary",), vmem_limit_bytes=VMEM_LIMIT),
        name="ffn",
    )(texp, meta, xs, wgu, bgu, wd, bd)


def _combine_kernel(code_ref, meta_ref, x1_ref, w4_ref, mod_ref, nfw_ref, ys_ref, oc_ref, ol_ref,
                    code_smem, buf, sem_c, sem):
    _load_codes(code_ref, code_smem, sem_c)

    for t in range(TM_DISP):
        for k in range(TOP_K):
            slot = _slot_of(code_smem[t * TOP_K + k], meta_ref)
            pltpu.make_async_copy(ys_ref.at[pl.ds(slot, 1)], buf.at[k, pl.ds(t, 1)], sem).start(priority=k % 2)
    for k in range(TOP_K):
        pltpu.make_async_copy(ys_ref.at[pl.ds(0, TM_DISP)], buf.at[k], sem).wait()
    w4 = w4_ref[...]
    moe = w4[:, 0:1] * buf[0]
    for k in range(1, TOP_K):
        moe = moe + w4[:, k:k + 1] * buf[k]
    gate2 = mod_ref[0][:, 5 * D:6 * D]
    x2 = x1_ref[...] + gate2 * moe
    ms = jnp.mean(x2 * x2, axis=-1, keepdims=True)
    out = x2 * lax.rsqrt(ms + RMS_EPS) * nfw_ref[...]
    is_ctx = pl.program_id(0) < N_CTX_TOK // TM_DISP

    @pl.when(is_ctx)
    def _():
        oc_ref[...] = out

    @pl.when(jnp.logical_not(is_ctx))
    def _():
        ol_ref[...] = out


def _combine(code2d, meta, x1, w4, mod3, nfw, ys):
    n_steps = N_TOK // TM_DISP
    n_ctx = N_CTX_TOK // TM_DISP
    return pl.pallas_call(
        _combine_kernel,
        out_shape=[jax.ShapeDtypeStruct((N_CTX_TOK, D), F32), jax.ShapeDtypeStruct((N_LAT_TOK, D), F32)],
        grid=(n_steps,),
        in_specs=[
            pl.BlockSpec(code2d.shape, lambda i: (0, 0)),
            pl.BlockSpec(memory_space=pltpu.SMEM),
            pl.BlockSpec((TM_DISP, D), lambda i: (i, 0)),
            pl.BlockSpec((TM_DISP, LANES), lambda i: (i, 0)),
            pl.BlockSpec((1, 1, N_MOD * D), lambda i: (_mod_row(i, TM_DISP), 0, 0)),
            pl.BlockSpec((1, D), lambda i: (0, 0)),
            pl.BlockSpec(memory_space=pl.ANY),
        ],
        out_specs=[pl.BlockSpec((TM_DISP, D), lambda i: (jnp.minimum(i, n_ctx - 1), 0)),
                   pl.BlockSpec((TM_DISP, D), lambda i: (jnp.maximum(i - n_ctx, 0), 0))],
        scratch_shapes=[pltpu.SMEM((TM_DISP * TOP_K,), jnp.int32), pltpu.VMEM((TOP_K, TM_DISP, D), F32),
                        pltpu.SemaphoreType.DMA, pltpu.SemaphoreType.DMA],
        compiler_params=pltpu.CompilerParams(dimension_semantics=("arbitrary",), vmem_limit_bytes=VMEM_LIMIT),
        name="combine",
    )(code2d, meta, x1, w4, mod3, nfw, ys)


def _pair_blockdiag(s):
    b = s.shape[0]
    s = s.reshape(b, N_PAIR, 2, HEAD, HEAD)
    z = jnp.zeros((b, N_PAIR, HEAD, HEAD), s.dtype)
    top = jnp.concatenate([s[:, :, 0], z], axis=-1)
    bot = jnp.concatenate([z, s[:, :, 1]], axis=-1)
    return jnp.concatenate([top, bot], axis=-2)


def _pair_unblock(s):
    b = s.shape[0]
    a = s[:, :, 0:HEAD, 0:HEAD]
    c = s[:, :, HEAD:, HEAD:]
    return jnp.stack([a, c], axis=2).reshape(b, 1, N_HEAD, HEAD, HEAD)


def kernel(x_prompt, x_sample, state_fwd, state_bwd, c, c_ctx, w_mod, b_mod, norm_mix_w, w_in, b_merge, pool_w, pool_scale, w_pool_out, shift_mu, decay_w0, decay_w2, iclr_a0, iclr_a2, gate_w2, k_k, k_a, r_k, ln_x_w, ln_x_b, w_rwkv_out, w_o, norm_ffn_w, router_w, router_b, expert_w_gu, expert_b_gu, expert_w_down, expert_b_down, norm_final_w):
    l = 0
    x_ctx = x_prompt.reshape(N_CTX_TOK, D)
    x_lat = x_sample.reshape(N_LAT_TOK, D)
    cvec = jnp.concatenate([c_ctx[None, :], c, jnp.zeros((MOD_ROWS - 1 - N_LAT_SEQ, D), F32)], axis=0)
    mod = _modulation(cvec, w_mod[l], b_mod[l][None, :])
    mod3 = mod.reshape(MOD_ROWS, 1, N_MOD * D)

    w_in_l = w_in[l]
    proj = _projection(x_ctx, x_lat, mod3, norm_mix_w[l][None, :], w_in_l[:, :OFF_MERGE].astype(BF16))

    zl = jnp.zeros((LORA, RW), F32)
    dw2 = jnp.stack([jnp.concatenate([decay_w2[l, 0], zl], 0), jnp.concatenate([zl, decay_w2[l, 1]], 0)])
    a2 = jnp.stack([jnp.concatenate([iclr_a2[l, 0], zl], 0), jnp.concatenate([zl, iclr_a2[l, 1]], 0)])
    hid = jnp.arange(RW) // HEAD
    seg = (hid[:, None] == hid[None, :]).astype(BF16)
    front_w = (pool_w[l].astype(BF16), pool_scale[l][None, :], shift_mu[l][None, :],
               decay_w0[l], dw2, iclr_a0[l], a2, k_k[l][None, :], k_a[l][None, :],
               r_k[l].reshape(1, RW), seg)
    f_ctx = _front(proj, False, N_CTX_SEQ, T_CTX, 0, front_w)
    f_lat = _front(proj, True, N_LAT_SEQ, T_LAT, N_CTX_TOK, front_w)

    y_ctx, sf, sb = _scan(f_ctx[2:], N_CTX_SEQ, T_CTX, None)
    init = (_pair_blockdiag(state_fwd[:, l]), _pair_blockdiag(state_bwd[:, l]))
    y_lat, _, _ = _scan(f_lat[2:], N_LAT_SEQ, T_LAT, init)

    rw_pad = jnp.concatenate([router_w[l], jnp.zeros((D, LANES - N_EXP), F32)], axis=1)
    rb_pad = jnp.concatenate([router_b[l], jnp.full((LANES - N_EXP,), -1e30, F32)])[None, :]
    post_w = (norm_mix_w[l][None, :], w_in_l[:, OFF_MERGE:].astype(BF16), b_merge[l][None, :],
              w_pool_out[l].astype(BF16), seg, ln_x_w[l][None, :], ln_x_b[l][None, :],
              gate_w2[l].astype(BF16), w_rwkv_out[l].astype(BF16), w_o[l].astype(BF16),
              norm_ffn_w[l][None, :], rw_pad, rb_pad)
    x1, hn2, code, w4, meta, texp = _post((x_ctx, x_lat), mod3, (f_ctx[0], f_lat[0]), (f_ctx[1], f_lat[1]),
                                          (y_ctx, y_lat), post_w)

    code2d = code[:, :TOP_K].reshape(N_TOK // TM_DISP, TM_DISP * TOP_K)
    meta1 = meta.reshape(LANES)
    xs = _dispatch(code2d, meta1, hn2)
    ys = _ffn(texp[:, 0], meta1, xs, expert_w_gu[l], expert_b_gu[l][:, None, :],
              expert_w_down[l], expert_b_down[l][:, None, :])
    out_ctx, out_lat = _combine(code2d, meta1, x1, w4, mod3, norm_final_w[None, :], ys)
    y_prompt = out_ctx.reshape(N_CTX_SEQ, T_CTX, D)
    y_sample = out_lat.reshape(N_LAT_SEQ, T_LAT, D)
    return (y_prompt, y_sample, _pair_unblock(sf), _pair_unblock(sb))
```

```python
import functools
import math

import jax
import jax.numpy as jnp
from jax import lax
from jax.experimental import pallas as pl
from jax.experimental.pallas import tpu as pltpu

F32 = jnp.float32
BF16 = jnp.bfloat16

D = 1024
N_CTX_SEQ, T_CTX = 32, 256
N_LAT_SEQ, T_LAT = 4, 1024
GRID_W = 64
N_CTX_TOK = N_CTX_SEQ * T_CTX
N_LAT_TOK = N_LAT_SEQ * T_LAT
N_TOK = N_CTX_TOK + N_LAT_TOK

POOL_W = 512
POOL_G = 4
POOL_GD = POOL_W // POOL_G
POOL_WINDOWS = (2, 4, 8, 16)
RW = 512
HEAD = 64
N_HEAD = RW // HEAD
N_PAIR = N_HEAD // 2
LORA = 64
GATE_LORA = 128
N_EXP = 32
TOP_K = 4
D_FF = 1024
SWIGLU_ALPHA = 1.702
SWIGLU_LIMIT = 7.0
RMS_EPS = 1e-6
GN_EPS = 1e-5 * HEAD
N_MOD = 6

OFF_R = POOL_W
SHIFT_COLS = 3 * RW + 4 * LORA + GATE_LORA
OFF_MERGE = OFF_R + SHIFT_COLS
IN_COLS = OFF_MERGE + 2 * D

LANES = 128
CHUNK = 64
INV_BLOCK = 8
SCAN_TOKENS_PER_STEP = 512
ROWS = 256
HALO = 64
EXT = ROWS + 2 * HALO
TM_PROJ = 512
TM_POST = 512
TM_DISP = 256
TM_SLOT = 512
N_SLOT_TILES = N_TOK * TOP_K // TM_SLOT + N_EXP
N_SLOTS = N_SLOT_TILES * TM_SLOT
RANK_BITS = 14
META_NT = N_EXP
MOD_ROWS = 8
VMEM_LIMIT = 56 * 1024 * 1024


def _sigmoid(x):
    return 1.0 / (1.0 + jnp.exp(-x))


def _split2(a):
    hi = a.astype(BF16)
    lo = (a - hi.astype(F32)).astype(BF16)
    return hi, lo


_NN = (((1,), (0,)), ((), ()))
_NT = (((1,), (1,)), ((), ()))


def _dot(a, b, dims=_NN):
    return lax.dot_general(a, b, dims, preferred_element_type=F32)


def _dot3(a, b, dims=_NN):
    ah, al = _split2(a)
    bh, bl = _split2(b)
    return _dot(ah, bh, dims) + (_dot(ah, bl, dims) + _dot(al, bh, dims))


def _dot_exact_lhs(a_bf16, b, passes):
    acc = None
    rem = b
    for _ in range(passes):
        part = rem.astype(BF16)
        term = _dot(a_bf16, part)
        acc = term if acc is None else acc + term
        rem = rem - part.astype(F32)
    return acc


def _dot_exact_rhs(a, b_bf16, passes):
    acc = None
    rem = a
    for _ in range(passes):
        part = rem.astype(BF16)
        term = _dot(part, b_bf16)
        acc = term if acc is None else acc + term
        rem = rem - part.astype(F32)
    return acc


def _modnorm(x, w, scale, shift):
    ms = jnp.mean(x * x, axis=-1, keepdims=True)
    return x * lax.rsqrt(ms + RMS_EPS) * w * (1.0 + scale) + shift


def _mod_row(i, tm):
    n_ctx = N_CTX_TOK // tm
    per = T_LAT // tm
    return jnp.where(i < n_ctx, 0, 1 + (i - n_ctx) // per)


def _mod_kernel(c_ref, w_ref, b_ref, o_ref):
    c = c_ref[...]
    s = c * _sigmoid(c)
    o_ref[...] = _dot3(s, w_ref[...]) + b_ref[...]


def _modulation(cvec, w_mod, b_mod):
    return pl.pallas_call(
        _mod_kernel,
        out_shape=jax.ShapeDtypeStruct((MOD_ROWS, N_MOD * D), F32),
        grid=(N_MOD,),
        in_specs=[
            pl.BlockSpec((MOD_ROWS, D), lambda j: (0, 0)),
            pl.BlockSpec((D, D), lambda j: (0, j)),
            pl.BlockSpec((1, D), lambda j: (0, j)),
        ],
        out_specs=pl.BlockSpec((MOD_ROWS, D), lambda j: (0, j)),
        compiler_params=pltpu.CompilerParams(dimension_semantics=("arbitrary",), vmem_limit_bytes=VMEM_LIMIT),
        name="mod",
    )(cvec, w_mod, b_mod)


def _path_specs(tm, n_cols=D):
    n_ctx = N_CTX_TOK // tm
    return [pl.BlockSpec((tm, n_cols), lambda i: (jnp.minimum(i, n_ctx - 1), 0)),
            pl.BlockSpec((tm, n_cols), lambda i: (jnp.maximum(i - n_ctx, 0), 0))]


def _path_tile(c_ref, l_ref, tm):
    return jnp.where(pl.program_id(0) < N_CTX_TOK // tm, c_ref[...], l_ref[...])


def _proj_kernel(xc_ref, xl_ref, mod_ref, nw_ref, w_ref, o_ref):
    mod = mod_ref[0]
    hn = _modnorm(_path_tile(xc_ref, xl_ref, TM_PROJ), nw_ref[...], mod[:, D:2 * D], mod[:, 0:D])
    o_ref[...] = _dot(hn.astype(BF16), w_ref[...])


def _projection(x_ctx, x_lat, mod3, norm_w, w_a):
    n_cols = w_a.shape[1]
    return pl.pallas_call(
        _proj_kernel,
        out_shape=jax.ShapeDtypeStruct((N_TOK, n_cols), F32),
        grid=(N_TOK // TM_PROJ,),
        in_specs=_path_specs(TM_PROJ) + [
            pl.BlockSpec((1, 1, N_MOD * D), lambda i: (_mod_row(i, TM_PROJ), 0, 0)),
            pl.BlockSpec((1, D), lambda i: (0, 0)),
            pl.BlockSpec((D, n_cols), lambda i: (0, 0)),
        ],
        out_specs=pl.BlockSpec((TM_PROJ, n_cols), lambda i: (i, 0)),
        compiler_params=pltpu.CompilerParams(dimension_semantics=("arbitrary",), vmem_limit_bytes=VMEM_LIMIT),
        name="proj",
    )(x_ctx, x_lat, mod3, norm_w, w_a)


N_FRONT_IN = 14
N_FRONT_OUT = 12


def _front_kernel(grid_mode, n_chunks, seq_len, *refs):
    (cur_ref, prev_ref, next_ref, poolw_ref, pscale_ref, mu_ref,
     dw0_ref, dw2_ref, a0_ref, a2_ref, kk_ref, ka_ref, rk_ref, seg_ref) = refs[:N_FRONT_IN]
    (ypre_ref, gsig_ref, r_ref, v_ref, kkn_ref, bonus_ref,
     lwf_ref, lwb_ref, kdf_ref, kdb_ref, bf_ref, bb_ref) = refs[-N_FRONT_OUT:]
    c = pl.program_id(1)
    cur = cur_ref[...]
    prev = jnp.where(c > 0, prev_ref[...], 0.0)
    nxt = jnp.where(c < n_chunks - 1, next_ref[...], 0.0)
    ext = jnp.concatenate([prev, cur, nxt], axis=0)

    def down(x, s):
        return pltpu.roll(x, s, 0)

    def up(x, s):
        return pltpu.roll(x, EXT - s, 0)

    def mid(x):
        return x[HALO:HALO + ROWS]

    row = lax.broadcasted_iota(jnp.int32, (ROWS, 1), 0)
    t_seq = c * ROWS + row

    outs = []
    for gi, win in enumerate(POOL_WINDOWS):
        h = win // 2
        u = ext[:, gi * POOL_GD:(gi + 1) * POOL_GD]
        trail, lead, s = u, u, 1
        while s < h:
            trail = trail + down(trail, s)
            lead = lead + up(lead, s)
            s *= 2
        total = mid(down(trail, 1) + lead)
        cnt = (jnp.minimum(t_seq + h, seq_len) - jnp.maximum(t_seq - h, 0)).astype(F32)
        pooled = total / cnt - mid(u)
        outs.append(_dot(pooled.astype(BF16), poolw_ref[gi]))
    mixed = jnp.concatenate(outs, axis=1) * pscale_ref[...]
    ypre_ref[...] = mixed.astype(BF16)

    pe = ext[:, OFF_R:OFF_MERGE]
    p = mid(pe)
    lane = lax.broadcasted_iota(jnp.int32, (1, SHIFT_COLS), 1)
    if grid_mode:
        col = t_seq % GRID_W
        left = jnp.where(col > 0, mid(down(pe, 1)), 0.0)
        right = jnp.where(col < GRID_W - 1, mid(up(pe, 1)), 0.0)
        upn = pe[0:ROWS]
        dnn = pe[2 * HALO:2 * HALO + ROWS]
        q = lane % 4
        sh = jnp.where(q == 0, left, jnp.where(q == 1, right, jnp.where(q == 2, upn, dnn)))
    else:
        sh = jnp.where(lane % 2 == 0, mid(down(pe, 1)), mid(up(pe, 1)))
    pm = p + mu_ref[...] * (sh - p)

    r = pm[:, 0:RW]
    k = pm[:, RW:2 * RW]
    v = pm[:, 2 * RW:3 * RW]
    o = 3 * RW
    w_lo = jnp.tanh(pm[:, o:o + 2 * LORA])
    a_lo = pm[:, o + 2 * LORA:o + 4 * LORA]
    gsig_ref[...] = _sigmoid(pm[:, o + 4 * LORA:o + 4 * LORA + GATE_LORA]).astype(BF16)

    seg = seg_ref[...]
    kkr = k * kk_ref[...]
    ss = _dot_exact_rhs(kkr * kkr, seg, 2)
    kkn = kkr / jnp.maximum(jnp.sqrt(ss), 1e-12)
    r_ref[...] = r
    v_ref[...] = v
    kkn_ref[...] = kkn

    bonus = None
    for d, (lw_ref, kd_ref, b_ref) in enumerate(((lwf_ref, kdf_ref, bf_ref), (lwb_ref, kdb_ref, bb_ref))):
        z = dw0_ref[d:d + 1, :] + _dot3(w_lo, dw2_ref[d])
        lw_ref[...] = -math.exp(-0.5) * _sigmoid(z)
        a = _sigmoid(a0_ref[d:d + 1, :] + _dot3(a_lo, a2_ref[d]))
        kd = k * (1.0 + (a - 1.0) * ka_ref[...])
        kd_ref[...] = kd
        b_ref[...] = kkn * a
        bd = _dot_exact_rhs(r * kd * rk_ref[...], seg, 2) * v
        bonus = bd if bonus is None else bonus + bd
    bonus_ref[...] = bonus


def _front(proj, grid_mode, n_seq, seq_len, tok0, weights):
    n_chunks = seq_len // ROWS
    n_tok = n_seq * seq_len
    blk0 = tok0 // ROWS
    hpb = ROWS // HALO
    last_halo = N_TOK // HALO - 1

    def cur_map(b, c):
        return (blk0 + b * n_chunks + c, 0)

    def prev_map(b, c):
        return (jnp.maximum((blk0 + b * n_chunks + c) * hpb - 1, 0), 0)

    def next_map(b, c):
        return (jnp.minimum((blk0 + b * n_chunks + c + 1) * hpb, last_halo), 0)

    def full(a):
        nd = a.ndim
        return pl.BlockSpec(a.shape, lambda b, c, _nd=nd: (0,) * _nd)

    def out_map(b, c):
        return (b * n_chunks + c, 0)

    f32_out = jax.ShapeDtypeStruct((n_tok, RW), F32)
    out_shape = [jax.ShapeDtypeStruct((n_tok, POOL_W), BF16), jax.ShapeDtypeStruct((n_tok, GATE_LORA), BF16)]
    out_shape += [f32_out] * (N_FRONT_OUT - 2)
    out_specs = [pl.BlockSpec((ROWS, POOL_W), out_map), pl.BlockSpec((ROWS, GATE_LORA), out_map)]
    out_specs += [pl.BlockSpec((ROWS, RW), out_map)] * (N_FRONT_OUT - 2)
    assert len(weights) == N_FRONT_IN - 3
    return pl.pallas_call(
        functools.partial(_front_kernel, grid_mode, n_chunks, seq_len),
        out_shape=out_shape,
        grid=(n_seq, n_chunks),
        in_specs=[
            pl.BlockSpec((ROWS, OFF_MERGE), cur_map),
            pl.BlockSpec((HALO, OFF_MERGE), prev_map),
            pl.BlockSpec((HALO, OFF_MERGE), next_map),
        ] + [full(a) for a in weights],
        out_specs=out_specs,
        compiler_params=pltpu.CompilerParams(dimension_semantics=("arbitrary", "arbitrary"),
                                             vmem_limit_bytes=VMEM_LIMIT),
        name="front_grid" if grid_mode else "front_ctx",
    )(proj, proj, proj, *weights)


def _scan_kernel(n_chunks, seqs_per_step, has_init, *refs):
    (r_ref, v_ref, kk_ref, bonus_ref, lwf_ref, lwb_ref, kdf_ref, kdb_ref, bf_ref, bb_ref) = refs[:10]
    if has_init:
        s0f_ref, s0b_ref = refs[10:12]
    y_ref, sf_ref, sb_ref = refs[-3:]
    C = CHUNK
    P2 = 2 * C
    ri = lax.broadcasted_iota(jnp.int32, (P2, P2), 0)
    ci = lax.broadcasted_iota(jnp.int32, (P2, P2), 1)
    same = (ri // C) == (ci // C)
    rp, cp = ri % C, ci % C
    eye = (ri == ci).astype(F32)
    blk = {}
    s = INV_BLOCK
    while s <= C:
        blk[s] = (ri // s) == (ci // s)
        s *= 2
    ti = lax.broadcasted_iota(jnp.int32, (C, C), 0)
    tj = lax.broadcasted_iota(jnp.int32, (C, C), 1)
    head_a = lax.broadcasted_iota(jnp.int32, (C, LANES), 1) < HEAD

    def stack(x):
        return jnp.concatenate([jnp.where(head_a, x, 0.0), jnp.where(head_a, 0.0, x)], axis=0)

    def d3(a, b, dims=_NN):
        return _dot(a[0], b[0], dims) + (_dot(a[0], b[1], dims) + _dot(a[1], b[0], dims))

    def d2(a_bf16, b):
        return _dot(a_bf16, b[0]) + _dot(a_bf16, b[1])

    def chunks(chains):
        n = range(len(chains))
        masks, sls = [], []
        for (s_ref, q, p, row0, lw_ref, kd_ref, b_ref, reverse) in chains:
            if reverse:
                masks.append((same & (cp > rp), same & (cp >= rp), (tj >= ti), 0))
            else:
                masks.append((same & (cp < rp), same & (cp <= rp), (tj <= ti), C - 1))
            sls.append((pl.ds(row0, C), slice(p * LANES, (p + 1) * LANES)))
        lw = [chains[i][4][sls[i]] for i in n]
        cum = [_dot_exact_lhs(masks[i][2].astype(BF16), lw[i], 3) for i in n]
        e_i = [jnp.exp(cum[i]) for i in n]
        e_n = [jnp.exp(-cum[i]) for i in n]
        QR = [jnp.concatenate([stack(kk_ref[sls[i]] * jnp.exp(cum[i] - lw[i])),
                               stack(r_ref[sls[i]] * e_i[i])], axis=0).astype(BF16) for i in n]
        BK = [jnp.concatenate([stack(chains[i][6][sls[i]] * e_n[i]),
                               stack(chains[i][5][sls[i]] * e_n[i])], axis=0).astype(BF16) for i in n]
        v_st = [stack(v_ref[sls[i]]) for i in n]
        Vs = [v_st[i].astype(BF16) for i in n]
        S = [chains[i][0][chains[i][1], chains[i][2]] for i in n]
        G = [_dot(QR[i], BK[i], _NT) for i in n]
        L = [jnp.where(masks[i][0], G[i][0:P2, 0:P2], 0.0) for i in n]
        Lk = [jnp.where(masks[i][0], G[i][0:P2, P2:2 * P2], 0.0) for i in n]
        RBK = [jnp.concatenate([jnp.where(masks[i][1], G[i][P2:2 * P2, 0:P2], 0.0),
                                jnp.where(masks[i][1], G[i][P2:2 * P2, P2:2 * P2], 0.0)], axis=1) for i in n]
        L0 = [jnp.where(blk[INV_BLOCK], L[i], 0.0) for i in n]
        L0h = [L0[i].astype(BF16) for i in n]
        X = [eye - L0[i] for i in n]
        Pw = [_split2(_dot(L0h[i], L0h[i])) for i in n]
        X = [X[i] + d2(X[i].astype(BF16), Pw[i]) for i in n]
        s = 4
        while s < INV_BLOCK:
            Pw = [_split2(d3(Pw[i], Pw[i])) for i in n]
            X = [X[i] + d3(_split2(X[i]), Pw[i]) for i in n]
            s *= 2
        s = INV_BLOCK
        while s < C:
            Dm = [_split2(X[i]) for i in n]
            Loff = [jnp.where(blk[2 * s] & jnp.logical_not(blk[s]), L[i], 0.0).astype(BF16) for i in n]
            E = [d2(Loff[i], Dm[i]) for i in n]
            X = [X[i] - d3(Dm[i], _split2(E[i])) for i in n]
            s *= 2
        QRA = [_dot(QR[i], S[i].astype(BF16), _NT) for i in n]
        W = [QRA[i][0:P2] + _dot(Lk[i].astype(BF16), Vs[i]) for i in n]
        U = [-_dot(X[i].astype(BF16), W[i].astype(BF16)) for i in n]
        UV = [jnp.concatenate([U[i], v_st[i]], axis=0) for i in n]
        Ys = [QRA[i][P2:2 * P2] + _dot(RBK[i].astype(BF16), UV[i].astype(BF16)) for i in n]
        dS = [_dot(UV[i].T.astype(BF16), BK[i]) for i in n]
        for i in n:
            last = masks[i][3]
            chains[i][0][chains[i][1], chains[i][2]] = (S[i] + dS[i]) * e_i[i][last:last + 1, :]
            y_ref[sls[i]] += Ys[i][0:C] + Ys[i][C:P2]

    y_ref[...] = bonus_ref[...]
    if has_init:
        sf_ref[...] = s0f_ref[...]
        sb_ref[...] = s0b_ref[...]
    else:
        sf_ref[...] = jnp.zeros_like(sf_ref)
        sb_ref[...] = jnp.zeros_like(sb_ref)

    seq_len = n_chunks * C

    def body(c, carry):
        chains = []
        for q in range(seqs_per_step):
            rf = pl.multiple_of(q * seq_len + c * C, C)
            rb = pl.multiple_of(q * seq_len + (n_chunks - 1 - c) * C, C)
            for p in range(N_PAIR):
                chains.append((sf_ref, q, p, rf, lwf_ref, kdf_ref, bf_ref, False))
                chains.append((sb_ref, q, p, rb, lwb_ref, kdb_ref, bb_ref, True))
        chunks(chains)
        return carry

    lax.fori_loop(0, n_chunks, body, 0)


def _scan(arrs, n_seq, seq_len, init):
    n_tok = n_seq * seq_len
    n_chunks = seq_len // CHUNK
    sps = max(1, SCAN_TOKENS_PER_STEP // seq_len)
    tok_spec = pl.BlockSpec((sps * seq_len, RW), lambda b: (b, 0))
    st_spec = pl.BlockSpec((sps, N_PAIR, LANES, LANES), lambda b: (b, 0, 0, 0))
    st_shape = jax.ShapeDtypeStruct((n_seq, N_PAIR, LANES, LANES), F32)
    ins = list(arrs)
    in_specs = [tok_spec] * 10
    if init is not None:
        ins += list(init)
        in_specs += [st_spec, st_spec]
    return pl.pallas_call(
        functools.partial(_scan_kernel, n_chunks, sps, init is not None),
        out_shape=[jax.ShapeDtypeStruct((n_tok, RW), F32), st_shape, st_shape],
        grid=(n_seq // sps,),
        in_specs=in_specs,
        out_specs=[tok_spec, st_spec, st_spec],
        compiler_params=pltpu.CompilerParams(dimension_semantics=("arbitrary",), vmem_limit_bytes=VMEM_LIMIT),
        name="scan_init" if init is not None else "scan_zero",
    )(*ins)


def _post_kernel(xc_ref, xl_ref, mod_ref, yprec_ref, yprel_ref, gsigc_ref, gsigl_ref, yc_ref, yl_ref,
                 nmw_ref, wm_ref, bm_ref, wpo_ref, seg_ref, lnw_ref, lnb_ref, gw2_ref, wro_ref, wo_ref,
                 nfw_ref, rw_ref, rb_ref,
                 x1_ref, hn2_ref, code_ref, w4_ref, meta_ref, texp_ref, carry_ref):
    i = pl.program_id(0)

    @pl.when(i == 0)
    def _():
        carry_ref[...] = jnp.zeros_like(carry_ref)

    x = _path_tile(xc_ref, xl_ref, TM_POST)
    mod = mod_ref[0]
    shift1, scale1, gate1 = mod[:, 0:D], mod[:, D:2 * D], mod[:, 2 * D:3 * D]
    shift2, scale2 = mod[:, 3 * D:4 * D], mod[:, 4 * D:5 * D]
    hn = _modnorm(x, nmw_ref[...], scale1, shift1)
    merge = _sigmoid(_dot(hn.astype(BF16), wm_ref[...]) + bm_ref[...])
    y_pool = _dot(_path_tile(yprec_ref, yprel_ref, TM_POST), wpo_ref[...])

    y = _path_tile(yc_ref, yl_ref, TM_POST)
    seg = seg_ref[...]
    mu = _dot_exact_rhs(y, seg, 2) * (1.0 / HEAD)
    yc = y - mu
    var = _dot_exact_rhs(yc * yc, seg, 2) * (1.0 / HEAD)
    yn = yc * lax.rsqrt(var + GN_EPS) * lnw_ref[...] + lnb_ref[...]
    g = _dot(_path_tile(gsigc_ref, gsigl_ref, TM_POST), gw2_ref[...])
    y_rwkv = _dot((yn * g).astype(BF16), wro_ref[...])

    mixed = merge[:, 0:D] * y_pool + merge[:, D:2 * D] * y_rwkv
    x1 = x + gate1 * _dot(mixed.astype(BF16), wo_ref[...])
    x1_ref[...] = x1
    hn2 = _modnorm(x1, nfw_ref[...], scale2, shift2)
    hn2_ref[...] = hn2

    logits = _dot3(hn2, rw_ref[...]) + rb_ref[...]
    lane = lax.broadcasted_iota(jnp.int32, logits.shape, 1)
    work = logits
    sel = None
    top = None
    for j in range(TOP_K):
        m = jnp.max(work, axis=-1, keepdims=True)
        if j == 0:
            top = m
        idx = jnp.min(jnp.where(work == m, lane, LANES), axis=-1, keepdims=True)
        pick = lane == idx
        sel = pick if sel is None else (sel | pick)
        work = jnp.where(pick, -jnp.inf, work)
    e = jnp.where(sel, jnp.exp(logits - top), 0.0)
    comb = e / jnp.sum(e, axis=-1, keepdims=True)

    tm = logits.shape[0]
    sel_b = jnp.where(sel, 1.0, 0.0).astype(BF16)
    before = (lax.broadcasted_iota(jnp.int32, (tm, tm), 1) < lax.broadcasted_iota(jnp.int32, (tm, tm), 0))
    rank = carry_ref[...] + _dot(before.astype(BF16), sel_b)
    carry = carry_ref[...] + jnp.sum(sel_b.astype(F32), axis=0, keepdims=True)
    carry_ref[...] = carry
    lower_e = (lax.broadcasted_iota(jnp.int32, (LANES, LANES), 0) < lax.broadcasted_iota(jnp.int32, (LANES, LANES), 1))
    rowpos = _dot(sel_b, lower_e.astype(BF16))
    code = jnp.zeros(logits.shape, jnp.int32)
    w4 = jnp.zeros(logits.shape, F32)
    for k in range(TOP_K):
        mk = sel & (rowpos == float(k))
        ek = jnp.sum(jnp.where(mk, lane, 0), axis=-1, keepdims=True)
        rk = jnp.sum(jnp.where(mk, rank, 0.0), axis=-1, keepdims=True).astype(jnp.int32)
        wk = jnp.sum(jnp.where(mk, comb, 0.0), axis=-1, keepdims=True)
        code = jnp.where(lane == k, ek * (1 << RANK_BITS) + rk, code)
        w4 = jnp.where(lane == k, wk, w4)
    code_ref[...] = code
    w4_ref[...] = w4

    tiles = jnp.floor((carry + (TM_SLOT - 1)) * (1.0 / TM_SLOT))
    incl = (lax.broadcasted_iota(jnp.int32, (LANES, LANES), 0) <= lax.broadcasted_iota(jnp.int32, (LANES, LANES), 1))
    tiles8 = jnp.broadcast_to(tiles, (8, LANES)).astype(BF16)
    cum = _dot(tiles8, incl.astype(BF16))[0:1]
    lane1 = lax.broadcasted_iota(jnp.int32, (1, LANES), 1)
    offs = ((cum - tiles) * TM_SLOT).astype(jnp.int32)
    n_used = jnp.sum(jnp.where(lane1 == N_EXP - 1, cum, 0.0), axis=-1, keepdims=True).astype(jnp.int32)
    meta_ref[...] = jnp.where(lane1 == META_NT, n_used, jnp.where(lane1 < N_EXP, offs, 0))
    tile_id = lax.broadcasted_iota(jnp.int32, (N_SLOT_TILES, LANES), 0).astype(F32)
    done = jnp.where((lax.broadcasted_iota(jnp.int32, (N_SLOT_TILES, LANES), 1) < N_EXP) & (cum <= tile_id), 1, 0)
    texp = jnp.minimum(jnp.sum(done, axis=-1, keepdims=True), N_EXP - 1)
    texp_ref[...] = jnp.broadcast_to(texp, (N_SLOT_TILES, LANES))


def _post(x, mod3, ypre, gsig, y, weights):
    def tile(n, dt):
        return pl.BlockSpec((TM_POST, n), lambda i: (i, 0))

    def full(a):
        nd = a.ndim
        return pl.BlockSpec(a.shape, lambda i, _nd=nd: (0,) * _nd)

    def const(shape):
        return pl.BlockSpec(shape, lambda i: (0, 0))

    return pl.pallas_call(
        _post_kernel,
        out_shape=[jax.ShapeDtypeStruct((N_TOK, D), F32), jax.ShapeDtypeStruct((N_TOK, D), F32),
                   jax.ShapeDtypeStruct((N_TOK, LANES), jnp.int32), jax.ShapeDtypeStruct((N_TOK, LANES), F32),
                   jax.ShapeDtypeStruct((1, LANES), jnp.int32),
                   jax.ShapeDtypeStruct((N_SLOT_TILES, LANES), jnp.int32)],
        grid=(N_TOK // TM_POST,),
        in_specs=_path_specs(TM_POST) + [
            pl.BlockSpec((1, 1, N_MOD * D), lambda i: (_mod_row(i, TM_POST), 0, 0)),
        ] + _path_specs(TM_POST, POOL_W) + _path_specs(TM_POST, GATE_LORA) + _path_specs(TM_POST, RW)
        + [full(a) for a in weights],
        out_specs=[tile(D, F32), tile(D, F32), tile(LANES, jnp.int32), tile(LANES, F32),
                   const((1, LANES)), const((N_SLOT_TILES, LANES))],
        scratch_shapes=[pltpu.VMEM((1, LANES), F32)],
        compiler_params=pltpu.CompilerParams(dimension_semantics=("arbitrary",), vmem_limit_bytes=VMEM_LIMIT),
        name="post",
    )(*x, mod3, *ypre, *gsig, *y, *weights)


def _slots_kernel(code_ref, meta_ref, slot_ref):
    code = code_ref[...]
    meta = meta_ref[...]
    lane = lax.broadcasted_iota(jnp.int32, code.shape, 1)
    expert = code >> RANK_BITS
    rank = code & ((1 << RANK_BITS) - 1)
    slot = jnp.zeros(code.shape, jnp.int32)
    for k in range(TOP_K):
        ek = jnp.sum(jnp.where(lane == k, expert, 0), axis=-1, keepdims=True)
        off = jnp.sum(jnp.where(lane == ek, meta, 0), axis=-1, keepdims=True)
        slot = jnp.where(lane == k, off + rank, slot)
    slot_ref[...] = slot


def _slots(code, meta):
    tm = 1024
    return pl.pallas_call(
        _slots_kernel,
        out_shape=jax.ShapeDtypeStruct((N_TOK, LANES), jnp.int32),
        grid=(N_TOK // tm,),
        in_specs=[pl.BlockSpec((tm, LANES), lambda i: (i, 0)), pl.BlockSpec((1, LANES), lambda i: (0, 0))],
        out_specs=pl.BlockSpec((tm, LANES), lambda i: (i, 0)),
        compiler_params=pltpu.CompilerParams(dimension_semantics=("arbitrary",), vmem_limit_bytes=VMEM_LIMIT),
        name="slots",
    )(code, meta)


def _load_slots(slot_ref, slot_smem, sem):
    cp = pltpu.make_async_copy(slot_ref.at[pl.program_id(0)], slot_smem, sem)
    cp.start()
    cp.wait()


def _dispatch_kernel(slot_ref, meta_ref, hn2_ref, xs_ref, slot_smem, zero_buf, sem_c, sem_z, sem):
    @pl.when(pl.program_id(0) == 0)
    def _():
        zero_buf[...] = jnp.zeros_like(zero_buf)

        def pad_tile(e):
            end = meta_ref[e + 1] if e + 1 < N_EXP else meta_ref[META_NT] * TM_SLOT
            start = pl.multiple_of(end - TM_SLOT, TM_SLOT)
            return end > meta_ref[e], pltpu.make_async_copy(zero_buf, xs_ref.at[pl.ds(start, TM_SLOT)], sem_z)

        def tail_tile(j):
            return pltpu.make_async_copy(zero_buf, xs_ref.at[pl.ds(pl.multiple_of(j * TM_SLOT, TM_SLOT), TM_SLOT)], sem_z)

        n_used = meta_ref[META_NT]
        for e in range(N_EXP):
            nonempty, cp = pad_tile(e)
            pl.when(nonempty)(cp.start)

        def start_tail(j, c):
            tail_tile(j).start()
            return c

        def wait_tail(j, c):
            tail_tile(j).wait()
            return c

        lax.fori_loop(n_used, N_SLOT_TILES, start_tail, 0)
        for e in range(N_EXP):
            nonempty, cp = pad_tile(e)
            pl.when(nonempty)(cp.wait)
        lax.fori_loop(n_used, N_SLOT_TILES, wait_tail, 0)

    _load_slots(slot_ref, slot_smem, sem_c)

    for t in range(TM_DISP):
        for k in range(TOP_K):
            slot = slot_smem[t * TOP_K + k]
            pltpu.make_async_copy(hn2_ref.at[pl.ds(t, 1)], xs_ref.at[pl.ds(slot, 1)], sem).start(priority=k % 2)
    for _ in range(TOP_K):
        pltpu.make_async_copy(hn2_ref, xs_ref.at[pl.ds(0, TM_DISP)], sem).wait()


def _dispatch(slot2d, meta, hn2):
    n_steps = N_TOK // TM_DISP
    return pl.pallas_call(
        _dispatch_kernel,
        out_shape=jax.ShapeDtypeStruct((N_SLOTS, D), F32),
        grid=(n_steps,),
        in_specs=[
            pl.BlockSpec(slot2d.shape, lambda i: (0, 0)),
            pl.BlockSpec(memory_space=pltpu.SMEM),
            pl.BlockSpec((TM_DISP, D), lambda i: (i, 0)),
        ],
        out_specs=pl.BlockSpec(memory_space=pl.ANY),
        scratch_shapes=[pltpu.SMEM((TM_DISP * TOP_K,), jnp.int32), pltpu.VMEM((TM_SLOT, D), F32),
                        pltpu.SemaphoreType.DMA, pltpu.SemaphoreType.DMA, pltpu.SemaphoreType.DMA],
        compiler_params=pltpu.CompilerParams(dimension_semantics=("arbitrary",), vmem_limit_bytes=VMEM_LIMIT),
        name="dispatch",
    )(slot2d, meta, hn2)


def _ffn_kernel(texp_ref, meta_ref, xs_ref, wgu_ref, bgu_ref, wd_ref, bd_ref, ys_ref, wgu_bf, wd_bf):
    i = pl.program_id(0)
    valid = i < meta_ref[META_NT]
    fresh = (i == 0) | (texp_ref[i] != texp_ref[jnp.maximum(i - 1, 0)])

    @pl.when(valid & fresh)
    def _():
        wgu_bf[...] = wgu_ref[0].astype(BF16)
        wd_bf[...] = wd_ref[0].astype(BF16)

    @pl.when(valid)
    def _():
        gu = _dot(xs_ref[...].astype(BF16), wgu_bf[...]) + bgu_ref[0]
        glu = jnp.minimum(gu[:, 0:D_FF], SWIGLU_LIMIT)
        lin = jnp.clip(gu[:, D_FF:2 * D_FF], -SWIGLU_LIMIT, SWIGLU_LIMIT)
        act = glu * _sigmoid(SWIGLU_ALPHA * glu) * (lin + 1.0)
        ys_ref[...] = _dot(act.astype(BF16), wd_bf[...]) + bd_ref[0]

    @pl.when(jnp.logical_not(valid))
    def _():
        ys_ref[...] = jnp.zeros_like(ys_ref)


def _ffn(texp, meta, xs, wgu, bgu, wd, bd):
    def used_tile(i, te, me):
        return (jnp.minimum(i, me[META_NT] - 1), 0)

    return pl.pallas_call(
        _ffn_kernel,
        out_shape=jax.ShapeDtypeStruct((N_SLOTS, D), F32),
        grid_spec=pltpu.PrefetchScalarGridSpec(
            num_scalar_prefetch=2,
            grid=(N_SLOT_TILES,),
            in_specs=[
                pl.BlockSpec((TM_SLOT, D), used_tile),
                pl.BlockSpec((1, D, 2 * D_FF), lambda i, te, me: (te[i], 0, 0)),
                pl.BlockSpec((1, 1, 2 * D_FF), lambda i, te, me: (te[i], 0, 0)),
                pl.BlockSpec((1, D_FF, D), lambda i, te, me: (te[i], 0, 0)),
                pl.BlockSpec((1, 1, D), lambda i, te, me: (te[i], 0, 0)),
            ],
            out_specs=pl.BlockSpec((TM_SLOT, D), lambda i, te, me: (i, 0)),
            scratch_shapes=[pltpu.VMEM((D, 2 * D_FF), BF16), pltpu.VMEM((D_FF, D), BF16)],
        ),
        compiler_params=pltpu.CompilerParams(dimension_semantics=("arbitrary",), vmem_limit_bytes=VMEM_LIMIT),
        name="ffn",
    )(texp, meta, xs, wgu, bgu, wd, bd)


def _combine_kernel(slot_ref, x1_ref, w4_ref, mod_ref, nfw_ref, ys_ref, oc_ref, ol_ref,
                    slot_smem, buf, sem_c, sem):
    _load_slots(slot_ref, slot_smem, sem_c)

    for t in range(TM_DISP):
        for k in range(TOP_K):
            slot = slot_smem[t * TOP_K + k]
            pltpu.make_async_copy(ys_ref.at[pl.ds(slot, 1)], buf.at[k, pl.ds(t, 1)], sem).start(priority=k % 2)
    for k in range(TOP_K):
        pltpu.make_async_copy(ys_ref.at[pl.ds(0, TM_DISP)], buf.at[k], sem).wait()
    w4 = w4_ref[...]
    moe = w4[:, 0:1] * buf[0]
    for k in range(1, TOP_K):
        moe = moe + w4[:, k:k + 1] * buf[k]
    gate2 = mod_ref[0][:, 5 * D:6 * D]
    x2 = x1_ref[...] + gate2 * moe
    ms = jnp.mean(x2 * x2, axis=-1, keepdims=True)
    out = x2 * lax.rsqrt(ms + RMS_EPS) * nfw_ref[...]
    is_ctx = pl.program_id(0) < N_CTX_TOK // TM_DISP

    @pl.when(is_ctx)
    def _():
        oc_ref[...] = out

    @pl.when(jnp.logical_not(is_ctx))
    def _():
        ol_ref[...] = out


def _combine(slot2d, x1, w4, mod3, nfw, ys):
    n_steps = N_TOK // TM_DISP
    n_ctx = N_CTX_TOK // TM_DISP
    return pl.pallas_call(
        _combine_kernel,
        out_shape=[jax.ShapeDtypeStruct((N_CTX_TOK, D), F32), jax.ShapeDtypeStruct((N_LAT_TOK, D), F32)],
        grid=(n_steps,),
        in_specs=[
            pl.BlockSpec(slot2d.shape, lambda i: (0, 0)),
            pl.BlockSpec((TM_DISP, D), lambda i: (i, 0)),
            pl.BlockSpec((TM_DISP, LANES), lambda i: (i, 0)),
            pl.BlockSpec((1, 1, N_MOD * D), lambda i: (_mod_row(i, TM_DISP), 0, 0)),
            pl.BlockSpec((1, D), lambda i: (0, 0)),
            pl.BlockSpec(memory_space=pl.ANY),
        ],
        out_specs=[pl.BlockSpec((TM_DISP, D), lambda i: (jnp.minimum(i, n_ctx - 1), 0)),
                   pl.BlockSpec((TM_DISP, D), lambda i: (jnp.maximum(i - n_ctx, 0), 0))],
        scratch_shapes=[pltpu.SMEM((TM_DISP * TOP_K,), jnp.int32), pltpu.VMEM((TOP_K, TM_DISP, D), F32),
                        pltpu.SemaphoreType.DMA, pltpu.SemaphoreType.DMA],
        compiler_params=pltpu.CompilerParams(dimension_semantics=("arbitrary",), vmem_limit_bytes=VMEM_LIMIT),
        name="combine",
    )(slot2d, x1, w4, mod3, nfw, ys)


def _pair_blockdiag(s):
    b = s.shape[0]
    s = s.reshape(b, N_PAIR, 2, HEAD, HEAD)
    z = jnp.zeros((b, N_PAIR, HEAD, HEAD), s.dtype)
    top = jnp.concatenate([s[:, :, 0], z], axis=-1)
    bot = jnp.concatenate([z, s[:, :, 1]], axis=-1)
    return jnp.concatenate([top, bot], axis=-2)


def _pair_unblock(s):
    b = s.shape[0]
    a = s[:, :, 0:HEAD, 0:HEAD]
    c = s[:, :, HEAD:, HEAD:]
    return jnp.stack([a, c], axis=2).reshape(b, 1, N_HEAD, HEAD, HEAD)


def kernel(x_prompt, x_sample, state_fwd, state_bwd, c, c_ctx, w_mod, b_mod, norm_mix_w, w_in, b_merge, pool_w, pool_scale, w_pool_out, shift_mu, decay_w0, decay_w2, iclr_a0, iclr_a2, gate_w2, k_k, k_a, r_k, ln_x_w, ln_x_b, w_rwkv_out, w_o, norm_ffn_w, router_w, router_b, expert_w_gu, expert_b_gu, expert_w_down, expert_b_down, norm_final_w):
    l = 0
    x_ctx = x_prompt.reshape(N_CTX_TOK, D)
    x_lat = x_sample.reshape(N_LAT_TOK, D)
    cvec = jnp.concatenate([c_ctx[None, :], c, jnp.zeros((MOD_ROWS - 1 - N_LAT_SEQ, D), F32)], axis=0)
    mod = _modulation(cvec, w_mod[l], b_mod[l][None, :])
    mod3 = mod.reshape(MOD_ROWS, 1, N_MOD * D)

    w_in_l = w_in[l]
    proj = _projection(x_ctx, x_lat, mod3, norm_mix_w[l][None, :], w_in_l[:, :OFF_MERGE].astype(BF16))

    zl = jnp.zeros((LORA, RW), F32)
    dw2 = jnp.stack([jnp.concatenate([decay_w2[l, 0], zl], 0), jnp.concatenate([zl, decay_w2[l, 1]], 0)])
    a2 = jnp.stack([jnp.concatenate([iclr_a2[l, 0], zl], 0), jnp.concatenate([zl, iclr_a2[l, 1]], 0)])
    hid = jnp.arange(RW) // HEAD
    seg = (hid[:, None] == hid[None, :]).astype(BF16)
    front_w = (pool_w[l].astype(BF16), pool_scale[l][None, :], shift_mu[l][None, :],
               decay_w0[l], dw2, iclr_a0[l], a2, k_k[l][None, :], k_a[l][None, :],
               r_k[l].reshape(1, RW), seg)
    f_ctx = _front(proj, False, N_CTX_SEQ, T_CTX, 0, front_w)
    f_lat = _front(proj, True, N_LAT_SEQ, T_LAT, N_CTX_TOK, front_w)

    y_ctx, sf, sb = _scan(f_ctx[2:], N_CTX_SEQ, T_CTX, None)
    init = (_pair_blockdiag(state_fwd[:, l]), _pair_blockdiag(state_bwd[:, l]))
    y_lat, _, _ = _scan(f_lat[2:], N_LAT_SEQ, T_LAT, init)

    rw_pad = jnp.concatenate([router_w[l], jnp.zeros((D, LANES - N_EXP), F32)], axis=1)
    rb_pad = jnp.concatenate([router_b[l], jnp.full((LANES - N_EXP,), -1e30, F32)])[None, :]
    post_w = (norm_mix_w[l][None, :], w_in_l[:, OFF_MERGE:].astype(BF16), b_merge[l][None, :],
              w_pool_out[l].astype(BF16), seg, ln_x_w[l][None, :], ln_x_b[l][None, :],
              gate_w2[l].astype(BF16), w_rwkv_out[l].astype(BF16), w_o[l].astype(BF16),
              norm_ffn_w[l][None, :], rw_pad, rb_pad)
    x1, hn2, code, w4, meta, texp = _post((x_ctx, x_lat), mod3, (f_ctx[0], f_lat[0]), (f_ctx[1], f_lat[1]),
                                          (y_ctx, y_lat), post_w)

    slot2d = _slots(code, meta)[:, :TOP_K].reshape(N_TOK // TM_DISP, TM_DISP * TOP_K)
    meta1 = meta.reshape(LANES)
    xs = _dispatch(slot2d, meta1, hn2)
    ys = _ffn(texp[:, 0], meta1, xs, expert_w_gu[l], expert_b_gu[l][:, None, :],
              expert_w_down[l], expert_b_down[l][:, None, :])
    out_ctx, out_lat = _combine(slot2d, x1, w4, mod3, norm_final_w[None, :], ys)
    y_prompt = out_ctx.reshape(N_CTX_SEQ, T_CTX, D)
    y_sample = out_lat.reshape(N_LAT_SEQ, T_LAT, D)
    return (y_prompt, y_sample, _pair_unblock(sf), _pair_unblock(sb))
```

```python
import functools
import math

import jax
import jax.numpy as jnp
from jax import lax
from jax.experimental import pallas as pl
from jax.experimental.pallas import tpu as pltpu

F32 = jnp.float32
BF16 = jnp.bfloat16

D = 1024
N_CTX_SEQ, T_CTX = 32, 256
N_LAT_SEQ, T_LAT = 4, 1024
GRID_W = 64
N_CTX_TOK = N_CTX_SEQ * T_CTX
N_LAT_TOK = N_LAT_SEQ * T_LAT
N_TOK = N_CTX_TOK + N_LAT_TOK

POOL_W = 512
POOL_G = 4
POOL_GD = POOL_W // POOL_G
POOL_WINDOWS = (2, 4, 8, 16)
RW = 512
HEAD = 64
N_HEAD = RW // HEAD
N_PAIR = N_HEAD // 2
LORA = 64
GATE_LORA = 128
N_EXP = 32
TOP_K = 4
D_FF = 1024
SWIGLU_ALPHA = 1.702
SWIGLU_LIMIT = 7.0
RMS_EPS = 1e-6
GN_EPS = 1e-5 * HEAD
N_MOD = 6

OFF_R = POOL_W
SHIFT_COLS = 3 * RW + 4 * LORA + GATE_LORA
OFF_MERGE = OFF_R + SHIFT_COLS
IN_COLS = OFF_MERGE + 2 * D

LANES = 128
CHUNK = 64
INV_BLOCK = 8
SCAN_TOKENS_PER_STEP = 512
ROWS = 256
HALO = 64
EXT = ROWS + 2 * HALO
TM_PROJ = 512
TM_POST = 512
TM_DISP = 1024
TM_SLOT = 512
N_SLOT_TILES = N_TOK * TOP_K // TM_SLOT + N_EXP
N_SLOTS = N_SLOT_TILES * TM_SLOT
RANK_BITS = 14
META_NT = N_EXP
MOD_ROWS = 8
VMEM_LIMIT = 56 * 1024 * 1024


def _sigmoid(x):
    return 1.0 / (1.0 + jnp.exp(-x))


def _split2(a):
    hi = a.astype(BF16)
    lo = (a - hi.astype(F32)).astype(BF16)
    return hi, lo


_NN = (((1,), (0,)), ((), ()))
_NT = (((1,), (1,)), ((), ()))


def _dot(a, b, dims=_NN):
    return lax.dot_general(a, b, dims, preferred_element_type=F32)


def _dot3(a, b, dims=_NN):
    ah, al = _split2(a)
    bh, bl = _split2(b)
    return _dot(ah, bh, dims) + (_dot(ah, bl, dims) + _dot(al, bh, dims))


def _dot_exact_lhs(a_bf16, b, passes):
    acc = None
    rem = b
    for _ in range(passes):
        part = rem.astype(BF16)
        term = _dot(a_bf16, part)
        acc = term if acc is None else acc + term
        rem = rem - part.astype(F32)
    return acc


def _dot_exact_rhs(a, b_bf16, passes):
    acc = None
    rem = a
    for _ in range(passes):
        part = rem.astype(BF16)
        term = _dot(part, b_bf16)
        acc = term if acc is None else acc + term
        rem = rem - part.astype(F32)
    return acc


def _modnorm(x, w, scale, shift):
    ms = jnp.mean(x * x, axis=-1, keepdims=True)
    return x * lax.rsqrt(ms + RMS_EPS) * w * (1.0 + scale) + shift


def _mod_row(i, tm):
    n_ctx = N_CTX_TOK // tm
    per = T_LAT // tm
    return jnp.where(i < n_ctx, 0, 1 + (i - n_ctx) // per)


def _mod_kernel(c_ref, w_ref, b_ref, o_ref):
    c = c_ref[...]
    s = c * _sigmoid(c)
    o_ref[...] = _dot3(s, w_ref[...]) + b_ref[...]


def _modulation(cvec, w_mod, b_mod):
    return pl.pallas_call(
        _mod_kernel,
        out_shape=jax.ShapeDtypeStruct((MOD_ROWS, N_MOD * D), F32),
        grid=(N_MOD,),
        in_specs=[
            pl.BlockSpec((MOD_ROWS, D), lambda j: (0, 0)),
            pl.BlockSpec((D, D), lambda j: (0, j)),
            pl.BlockSpec((1, D), lambda j: (0, j)),
        ],
        out_specs=pl.BlockSpec((MOD_ROWS, D), lambda j: (0, j)),
        compiler_params=pltpu.CompilerParams(dimension_semantics=("arbitrary",), vmem_limit_bytes=VMEM_LIMIT),
        name="mod",
    )(cvec, w_mod, b_mod)


def _path_specs(tm, n_cols=D):
    n_ctx = N_CTX_TOK // tm
    return [pl.BlockSpec((tm, n_cols), lambda i: (jnp.minimum(i, n_ctx - 1), 0)),
            pl.BlockSpec((tm, n_cols), lambda i: (jnp.maximum(i - n_ctx, 0), 0))]


def _path_tile(c_ref, l_ref, tm):
    return jnp.where(pl.program_id(0) < N_CTX_TOK // tm, c_ref[...], l_ref[...])


def _proj_kernel(xc_ref, xl_ref, mod_ref, nw_ref, w_ref, o_ref):
    mod = mod_ref[0]
    hn = _modnorm(_path_tile(xc_ref, xl_ref, TM_PROJ), nw_ref[...], mod[:, D:2 * D], mod[:, 0:D])
    o_ref[...] = _dot(hn.astype(BF16), w_ref[...])


def _projection(x_ctx, x_lat, mod3, norm_w, w_a):
    n_cols = w_a.shape[1]
    return pl.pallas_call(
        _proj_kernel,
        out_shape=jax.ShapeDtypeStruct((N_TOK, n_cols), F32),
        grid=(N_TOK // TM_PROJ,),
        in_specs=_path_specs(TM_PROJ) + [
            pl.BlockSpec((1, 1, N_MOD * D), lambda i: (_mod_row(i, TM_PROJ), 0, 0)),
            pl.BlockSpec((1, D), lambda i: (0, 0)),
            pl.BlockSpec((D, n_cols), lambda i: (0, 0)),
        ],
        out_specs=pl.BlockSpec((TM_PROJ, n_cols), lambda i: (i, 0)),
        compiler_params=pltpu.CompilerParams(dimension_semantics=("arbitrary",), vmem_limit_bytes=VMEM_LIMIT),
        name="proj",
    )(x_ctx, x_lat, mod3, norm_w, w_a)


N_FRONT_IN = 14
N_FRONT_OUT = 12


def _front_kernel(grid_mode, n_chunks, seq_len, *refs):
    (cur_ref, prev_ref, next_ref, poolw_ref, pscale_ref, mu_ref,
     dw0_ref, dw2_ref, a0_ref, a2_ref, kk_ref, ka_ref, rk_ref, seg_ref) = refs[:N_FRONT_IN]
    (ypre_ref, gsig_ref, r_ref, v_ref, kkn_ref, bonus_ref,
     lwf_ref, lwb_ref, kdf_ref, kdb_ref, bf_ref, bb_ref) = refs[-N_FRONT_OUT:]
    c = pl.program_id(1)
    cur = cur_ref[...]
    prev = jnp.where(c > 0, prev_ref[...], 0.0)
    nxt = jnp.where(c < n_chunks - 1, next_ref[...], 0.0)
    ext = jnp.concatenate([prev, cur, nxt], axis=0)

    def down(x, s):
        return pltpu.roll(x, s, 0)

    def up(x, s):
        return pltpu.roll(x, EXT - s, 0)

    def mid(x):
        return x[HALO:HALO + ROWS]

    row = lax.broadcasted_iota(jnp.int32, (ROWS, 1), 0)
    t_seq = c * ROWS + row

    outs = []
    for gi, win in enumerate(POOL_WINDOWS):
        h = win // 2
        u = ext[:, gi * POOL_GD:(gi + 1) * POOL_GD]
        trail, lead, s = u, u, 1
        while s < h:
            trail = trail + down(trail, s)
            lead = lead + up(lead, s)
            s *= 2
        total = mid(down(trail, 1) + lead)
        cnt = (jnp.minimum(t_seq + h, seq_len) - jnp.maximum(t_seq - h, 0)).astype(F32)
        pooled = total / cnt - mid(u)
        outs.append(_dot(pooled.astype(BF16), poolw_ref[gi]))
    mixed = jnp.concatenate(outs, axis=1) * pscale_ref[...]
    ypre_ref[...] = mixed.astype(BF16)

    pe = ext[:, OFF_R:OFF_MERGE]
    p = mid(pe)
    lane = lax.broadcasted_iota(jnp.int32, (1, SHIFT_COLS), 1)
    if grid_mode:
        col = t_seq % GRID_W
        left = jnp.where(col > 0, mid(down(pe, 1)), 0.0)
        right = jnp.where(col < GRID_W - 1, mid(up(pe, 1)), 0.0)
        upn = pe[0:ROWS]
        dnn = pe[2 * HALO:2 * HALO + ROWS]
        q = lane % 4
        sh = jnp.where(q == 0, left, jnp.where(q == 1, right, jnp.where(q == 2, upn, dnn)))
    else:
        sh = jnp.where(lane % 2 == 0, mid(down(pe, 1)), mid(up(pe, 1)))
    pm = p + mu_ref[...] * (sh - p)

    r = pm[:, 0:RW]
    k = pm[:, RW:2 * RW]
    v = pm[:, 2 * RW:3 * RW]
    o = 3 * RW
    w_lo = jnp.tanh(pm[:, o:o + 2 * LORA])
    a_lo = pm[:, o + 2 * LORA:o + 4 * LORA]
    gsig_ref[...] = _sigmoid(pm[:, o + 4 * LORA:o + 4 * LORA + GATE_LORA]).astype(BF16)

    seg = seg_ref[...]
    kkr = k * kk_ref[...]
    ss = _dot_exact_rhs(kkr * kkr, seg, 2)
    kkn = kkr / jnp.maximum(jnp.sqrt(ss), 1e-12)
    r_ref[...] = r
    v_ref[...] = v
    kkn_ref[...] = kkn

    bonus = None
    for d, (lw_ref, kd_ref, b_ref) in enumerate(((lwf_ref, kdf_ref, bf_ref), (lwb_ref, kdb_ref, bb_ref))):
        z = dw0_ref[d:d + 1, :] + _dot3(w_lo, dw2_ref[d])
        lw_ref[...] = -math.exp(-0.5) * _sigmoid(z)
        a = _sigmoid(a0_ref[d:d + 1, :] + _dot3(a_lo, a2_ref[d]))
        kd = k * (1.0 + (a - 1.0) * ka_ref[...])
        kd_ref[...] = kd
        b_ref[...] = kkn * a
        bd = _dot_exact_rhs(r * kd * rk_ref[...], seg, 2) * v
        bonus = bd if bonus is None else bonus + bd
    bonus_ref[...] = bonus


def _front(proj, grid_mode, n_seq, seq_len, tok0, weights):
    n_chunks = seq_len // ROWS
    n_tok = n_seq * seq_len
    blk0 = tok0 // ROWS
    hpb = ROWS // HALO
    last_halo = N_TOK // HALO - 1

    def cur_map(b, c):
        return (blk0 + b * n_chunks + c, 0)

    def prev_map(b, c):
        return (jnp.maximum((blk0 + b * n_chunks + c) * hpb - 1, 0), 0)

    def next_map(b, c):
        return (jnp.minimum((blk0 + b * n_chunks + c + 1) * hpb, last_halo), 0)

    def full(a):
        nd = a.ndim
        return pl.BlockSpec(a.shape, lambda b, c, _nd=nd: (0,) * _nd)

    def out_map(b, c):
        return (b * n_chunks + c, 0)

    f32_out = jax.ShapeDtypeStruct((n_tok, RW), F32)
    out_shape = [jax.ShapeDtypeStruct((n_tok, POOL_W), BF16), jax.ShapeDtypeStruct((n_tok, GATE_LORA), BF16)]
    out_shape += [f32_out] * (N_FRONT_OUT - 2)
    out_specs = [pl.BlockSpec((ROWS, POOL_W), out_map), pl.BlockSpec((ROWS, GATE_LORA), out_map)]
    out_specs += [pl.BlockSpec((ROWS, RW), out_map)] * (N_FRONT_OUT - 2)
    assert len(weights) == N_FRONT_IN - 3
    return pl.pallas_call(
        functools.partial(_front_kernel, grid_mode, n_chunks, seq_len),
        out_shape=out_shape,
        grid=(n_seq, n_chunks),
        in_specs=[
            pl.BlockSpec((ROWS, OFF_MERGE), cur_map),
            pl.BlockSpec((HALO, OFF_MERGE), prev_map),
            pl.BlockSpec((HALO, OFF_MERGE), next_map),
        ] + [full(a) for a in weights],
        out_specs=out_specs,
        compiler_params=pltpu.CompilerParams(dimension_semantics=("arbitrary", "arbitrary"),
                                             vmem_limit_bytes=VMEM_LIMIT),
        name="front_grid" if grid_mode else "front_ctx",
    )(proj, proj, proj, *weights)


def _scan_kernel(n_chunks, seqs_per_step, has_init, *refs):
    (r_ref, v_ref, kk_ref, bonus_ref, lwf_ref, lwb_ref, kdf_ref, kdb_ref, bf_ref, bb_ref) = refs[:10]
    if has_init:
        s0f_ref, s0b_ref = refs[10:12]
    y_ref, sf_ref, sb_ref = refs[-3:]
    C = CHUNK
    P2 = 2 * C
    ri = lax.broadcasted_iota(jnp.int32, (P2, P2), 0)
    ci = lax.broadcasted_iota(jnp.int32, (P2, P2), 1)
    same = (ri // C) == (ci // C)
    rp, cp = ri % C, ci % C
    eye = (ri == ci).astype(F32)
    blk = {}
    s = INV_BLOCK
    while s <= C:
        blk[s] = (ri // s) == (ci // s)
        s *= 2
    ti = lax.broadcasted_iota(jnp.int32, (C, C), 0)
    tj = lax.broadcasted_iota(jnp.int32, (C, C), 1)
    head_a = lax.broadcasted_iota(jnp.int32, (C, LANES), 1) < HEAD

    def stack(x):
        return jnp.concatenate([jnp.where(head_a, x, 0.0), jnp.where(head_a, 0.0, x)], axis=0)

    def d3(a, b, dims=_NN):
        return _dot(a[0], b[0], dims) + (_dot(a[0], b[1], dims) + _dot(a[1], b[0], dims))

    def d2(a_bf16, b):
        return _dot(a_bf16, b[0]) + _dot(a_bf16, b[1])

    def chunks(chains):
        n = range(len(chains))
        masks, sls = [], []
        for (s_ref, q, p, row0, lw_ref, kd_ref, b_ref, reverse) in chains:
            if reverse:
                masks.append((same & (cp > rp), same & (cp >= rp), (tj >= ti), 0))
            else:
                masks.append((same & (cp < rp), same & (cp <= rp), (tj <= ti), C - 1))
            sls.append((pl.ds(row0, C), slice(p * LANES, (p + 1) * LANES)))
        lw = [chains[i][4][sls[i]] for i in n]
        cum = [_dot_exact_lhs(masks[i][2].astype(BF16), lw[i], 3) for i in n]
        e_i = [jnp.exp(cum[i]) for i in n]
        e_n = [jnp.exp(-cum[i]) for i in n]
        QR = [jnp.concatenate([stack(kk_ref[sls[i]] * jnp.exp(cum[i] - lw[i])),
                               stack(r_ref[sls[i]] * e_i[i])], axis=0).astype(BF16) for i in n]
        BK = [jnp.concatenate([stack(chains[i][6][sls[i]] * e_n[i]),
                               stack(chains[i][5][sls[i]] * e_n[i])], axis=0).astype(BF16) for i in n]
        v_st = [stack(v_ref[sls[i]]) for i in n]
        Vs = [v_st[i].astype(BF16) for i in n]
        S = [chains[i][0][chains[i][1], chains[i][2]] for i in n]
        G = [_dot(QR[i], BK[i], _NT) for i in n]
        L = [jnp.where(masks[i][0], G[i][0:P2, 0:P2], 0.0) for i in n]
        Lk = [jnp.where(masks[i][0], G[i][0:P2, P2:2 * P2], 0.0) for i in n]
        RBK = [jnp.concatenate([jnp.where(masks[i][1], G[i][P2:2 * P2, 0:P2], 0.0),
                                jnp.where(masks[i][1], G[i][P2:2 * P2, P2:2 * P2], 0.0)], axis=1) for i in n]
        L0 = [jnp.where(blk[INV_BLOCK], L[i], 0.0) for i in n]
        L0h = [L0[i].astype(BF16) for i in n]
        X = [eye - L0[i] for i in n]
        Pw = [_split2(_dot(L0h[i], L0h[i])) for i in n]
        X = [X[i] + d2(X[i].astype(BF16), Pw[i]) for i in n]
        s = 4
        while s < INV_BLOCK:
            Pw = [_split2(d3(Pw[i], Pw[i])) for i in n]
            X = [X[i] + d3(_split2(X[i]), Pw[i]) for i in n]
            s *= 2
        s = INV_BLOCK
        while s < C:
            Dm = [_split2(X[i]) for i in n]
            Loff = [jnp.where(blk[2 * s] & jnp.logical_not(blk[s]), L[i], 0.0).astype(BF16) for i in n]
            E = [d2(Loff[i], Dm[i]) for i in n]
            X = [X[i] - d3(Dm[i], _split2(E[i])) for i in n]
            s *= 2
        QRA = [_dot(QR[i], S[i].astype(BF16), _NT) for i in n]
        W = [QRA[i][0:P2] + _dot(Lk[i].astype(BF16), Vs[i]) for i in n]
        U = [-_dot(X[i].astype(BF16), W[i].astype(BF16)) for i in n]
        UV = [jnp.concatenate([U[i], v_st[i]], axis=0) for i in n]
        Ys = [QRA[i][P2:2 * P2] + _dot(RBK[i].astype(BF16), UV[i].astype(BF16)) for i in n]
        dS = [_dot(UV[i].T.astype(BF16), BK[i]) for i in n]
        for i in n:
            last = masks[i][3]
            chains[i][0][chains[i][1], chains[i][2]] = (S[i] + dS[i]) * e_i[i][last:last + 1, :]
            y_ref[sls[i]] += Ys[i][0:C] + Ys[i][C:P2]

    y_ref[...] = bonus_ref[...]
    if has_init:
        sf_ref[...] = s0f_ref[...]
        sb_ref[...] = s0b_ref[...]
    else:
        sf_ref[...] = jnp.zeros_like(sf_ref)
        sb_ref[...] = jnp.zeros_like(sb_ref)

    seq_len = n_chunks * C

    def body(c, carry):
        chains = []
        for q in range(seqs_per_step):
            rf = pl.multiple_of(q * seq_len + c * C, C)
            rb = pl.multiple_of(q * seq_len + (n_chunks - 1 - c) * C, C)
            for p in range(N_PAIR):
                chains.append((sf_ref, q, p, rf, lwf_ref, kdf_ref, bf_ref, False))
                chains.append((sb_ref, q, p, rb, lwb_ref, kdb_ref, bb_ref, True))
        chunks(chains)
        return carry

    lax.fori_loop(0, n_chunks, body, 0)


def _scan(arrs, n_seq, seq_len, init):
    n_tok = n_seq * seq_len
    n_chunks = seq_len // CHUNK
    sps = max(1, SCAN_TOKENS_PER_STEP // seq_len)
    tok_spec = pl.BlockSpec((sps * seq_len, RW), lambda b: (b, 0))
    st_spec = pl.BlockSpec((sps, N_PAIR, LANES, LANES), lambda b: (b, 0, 0, 0))
    st_shape = jax.ShapeDtypeStruct((n_seq, N_PAIR, LANES, LANES), F32)
    ins = list(arrs)
    in_specs = [tok_spec] * 10
    if init is not None:
        ins += list(init)
        in_specs += [st_spec, st_spec]
    return pl.pallas_call(
        functools.partial(_scan_kernel, n_chunks, sps, init is not None),
        out_shape=[jax.ShapeDtypeStruct((n_tok, RW), F32), st_shape, st_shape],
        grid=(n_seq // sps,),
        in_specs=in_specs,
        out_specs=[tok_spec, st_spec, st_spec],
        compiler_params=pltpu.CompilerParams(dimension_semantics=("arbitrary",), vmem_limit_bytes=VMEM_LIMIT),
        name="scan_init" if init is not None else "scan_zero",
    )(*ins)


def _post_kernel(xc_ref, xl_ref, mod_ref, yprec_ref, yprel_ref, gsigc_ref, gsigl_ref, yc_ref, yl_ref,
                 nmw_ref, wm_ref, bm_ref, wpo_ref, seg_ref, lnw_ref, lnb_ref, gw2_ref, wro_ref, wo_ref,
                 nfw_ref, rw_ref, rb_ref,
                 x1_ref, hn2_ref, code_ref, w4_ref, meta_ref, texp_ref, carry_ref):
    i = pl.program_id(0)

    @pl.when(i == 0)
    def _():
        carry_ref[...] = jnp.zeros_like(carry_ref)

    x = _path_tile(xc_ref, xl_ref, TM_POST)
    mod = mod_ref[0]
    shift1, scale1, gate1 = mod[:, 0:D], mod[:, D:2 * D], mod[:, 2 * D:3 * D]
    shift2, scale2 = mod[:, 3 * D:4 * D], mod[:, 4 * D:5 * D]
    hn = _modnorm(x, nmw_ref[...], scale1, shift1)
    merge = _sigmoid(_dot(hn.astype(BF16), wm_ref[...]) + bm_ref[...])
    y_pool = _dot(_path_tile(yprec_ref, yprel_ref, TM_POST), wpo_ref[...])

    y = _path_tile(yc_ref, yl_ref, TM_POST)
    seg = seg_ref[...]
    mu = _dot_exact_rhs(y, seg, 2) * (1.0 / HEAD)
    yc = y - mu
    var = _dot_exact_rhs(yc * yc, seg, 2) * (1.0 / HEAD)
    yn = yc * lax.rsqrt(var + GN_EPS) * lnw_ref[...] + lnb_ref[...]
    g = _dot(_path_tile(gsigc_ref, gsigl_ref, TM_POST), gw2_ref[...])
    y_rwkv = _dot((yn * g).astype(BF16), wro_ref[...])

    mixed = merge[:, 0:D] * y_pool + merge[:, D:2 * D] * y_rwkv
    x1 = x + gate1 * _dot(mixed.astype(BF16), wo_ref[...])
    x1_ref[...] = x1
    hn2 = _modnorm(x1, nfw_ref[...], scale2, shift2)
    hn2_ref[...] = hn2

    logits = _dot3(hn2, rw_ref[...]) + rb_ref[...]
    lane = lax.broadcasted_iota(jnp.int32, logits.shape, 1)
    work = logits
    sel = None
    top = None
    for j in range(TOP_K):
        m = jnp.max(work, axis=-1, keepdims=True)
        if j == 0:
            top = m
        idx = jnp.min(jnp.where(work == m, lane, LANES), axis=-1, keepdims=True)
        pick = lane == idx
        sel = pick if sel is None else (sel | pick)
        work = jnp.where(pick, -jnp.inf, work)
    e = jnp.where(sel, jnp.exp(logits - top), 0.0)
    comb = e / jnp.sum(e, axis=-1, keepdims=True)

    tm = logits.shape[0]
    sel_b = jnp.where(sel, 1.0, 0.0).astype(BF16)
    before = (lax.broadcasted_iota(jnp.int32, (tm, tm), 1) < lax.broadcasted_iota(jnp.int32, (tm, tm), 0))
    rank = carry_ref[...] + _dot(before.astype(BF16), sel_b)
    carry = carry_ref[...] + jnp.sum(sel_b.astype(F32), axis=0, keepdims=True)
    carry_ref[...] = carry
    lower_e = (lax.broadcasted_iota(jnp.int32, (LANES, LANES), 0) < lax.broadcasted_iota(jnp.int32, (LANES, LANES), 1))
    rowpos = _dot(sel_b, lower_e.astype(BF16))
    code = jnp.zeros(logits.shape, jnp.int32)
    w4 = jnp.zeros(logits.shape, F32)
    for k in range(TOP_K):
        mk = sel & (rowpos == float(k))
        ek = jnp.sum(jnp.where(mk, lane, 0), axis=-1, keepdims=True)
        rk = jnp.sum(jnp.where(mk, rank, 0.0), axis=-1, keepdims=True).astype(jnp.int32)
        wk = jnp.sum(jnp.where(mk, comb, 0.0), axis=-1, keepdims=True)
        code = jnp.where(lane == k, ek * (1 << RANK_BITS) + rk, code)
        w4 = jnp.where(lane == k, wk, w4)
    code_ref[...] = code
    w4_ref[...] = w4

    tiles = jnp.floor((carry + (TM_SLOT - 1)) * (1.0 / TM_SLOT))
    incl = (lax.broadcasted_iota(jnp.int32, (LANES, LANES), 0) <= lax.broadcasted_iota(jnp.int32, (LANES, LANES), 1))
    tiles8 = jnp.broadcast_to(tiles, (8, LANES)).astype(BF16)
    cum = _dot(tiles8, incl.astype(BF16))[0:1]
    lane1 = lax.broadcasted_iota(jnp.int32, (1, LANES), 1)
    offs = ((cum - tiles) * TM_SLOT).astype(jnp.int32)
    n_used = jnp.sum(jnp.where(lane1 == N_EXP - 1, cum, 0.0), axis=-1, keepdims=True).astype(jnp.int32)
    meta_ref[...] = jnp.where(lane1 == META_NT, n_used, jnp.where(lane1 < N_EXP, offs, 0))
    tile_id = lax.broadcasted_iota(jnp.int32, (N_SLOT_TILES, LANES), 0).astype(F32)
    done = jnp.where((lax.broadcasted_iota(jnp.int32, (N_SLOT_TILES, LANES), 1) < N_EXP) & (cum <= tile_id), 1, 0)
    texp = jnp.minimum(jnp.sum(done, axis=-1, keepdims=True), N_EXP - 1)
    texp_ref[...] = jnp.broadcast_to(texp, (N_SLOT_TILES, LANES))


def _post(x, mod3, ypre, gsig, y, weights):
    def tile(n, dt):
        return pl.BlockSpec((TM_POST, n), lambda i: (i, 0))

    def full(a):
        nd = a.ndim
        return pl.BlockSpec(a.shape, lambda i, _nd=nd: (0,) * _nd)

    def const(shape):
        return pl.BlockSpec(shape, lambda i: (0, 0))

    return pl.pallas_call(
        _post_kernel,
        out_shape=[jax.ShapeDtypeStruct((N_TOK, D), F32), jax.ShapeDtypeStruct((N_TOK, D), F32),
                   jax.ShapeDtypeStruct((N_TOK, LANES), jnp.int32), jax.ShapeDtypeStruct((N_TOK, LANES), F32),
                   jax.ShapeDtypeStruct((1, LANES), jnp.int32),
                   jax.ShapeDtypeStruct((N_SLOT_TILES, LANES), jnp.int32)],
        grid=(N_TOK // TM_POST,),
        in_specs=_path_specs(TM_POST) + [
            pl.BlockSpec((1, 1, N_MOD * D), lambda i: (_mod_row(i, TM_POST), 0, 0)),
        ] + _path_specs(TM_POST, POOL_W) + _path_specs(TM_POST, GATE_LORA) + _path_specs(TM_POST, RW)
        + [full(a) for a in weights],
        out_specs=[tile(D, F32), tile(D, F32), tile(LANES, jnp.int32), tile(LANES, F32),
                   const((1, LANES)), const((N_SLOT_TILES, LANES))],
        scratch_shapes=[pltpu.VMEM((1, LANES), F32)],
        compiler_params=pltpu.CompilerParams(dimension_semantics=("arbitrary",), vmem_limit_bytes=VMEM_LIMIT),
        name="post",
    )(*x, mod3, *ypre, *gsig, *y, *weights)


def _slots_kernel(code_ref, meta_ref, slot_ref):
    code = code_ref[...]
    meta = meta_ref[...]
    lane = lax.broadcasted_iota(jnp.int32, code.shape, 1)
    expert = code >> RANK_BITS
    rank = code & ((1 << RANK_BITS) - 1)
    slot = jnp.zeros(code.shape, jnp.int32)
    for k in range(TOP_K):
        ek = jnp.sum(jnp.where(lane == k, expert, 0), axis=-1, keepdims=True)
        off = jnp.sum(jnp.where(lane == ek, meta, 0), axis=-1, keepdims=True)
        slot = jnp.where(lane == k, off + rank, slot)
    slot_ref[...] = slot


def _slots(code, meta):
    tm = 1024
    return pl.pallas_call(
        _slots_kernel,
        out_shape=jax.ShapeDtypeStruct((N_TOK, LANES), jnp.int32),
        grid=(N_TOK // tm,),
        in_specs=[pl.BlockSpec((tm, LANES), lambda i: (i, 0)), pl.BlockSpec((1, LANES), lambda i: (0, 0))],
        out_specs=pl.BlockSpec((tm, LANES), lambda i: (i, 0)),
        compiler_params=pltpu.CompilerParams(dimension_semantics=("arbitrary",), vmem_limit_bytes=VMEM_LIMIT),
        name="slots",
    )(code, meta)


def _load_slots(slot_ref, slot_smem, sem):
    cp = pltpu.make_async_copy(slot_ref.at[pl.program_id(0)], slot_smem, sem)
    cp.start()
    cp.wait()


def _dispatch_kernel(slot_ref, meta_ref, hn2_ref, xs_ref, slot_smem, zero_buf, sem_c, sem_z, sem):
    @pl.when(pl.program_id(0) == 0)
    def _():
        zero_buf[...] = jnp.zeros_like(zero_buf)

        def pad_tile(e):
            end = meta_ref[e + 1] if e + 1 < N_EXP else meta_ref[META_NT] * TM_SLOT
            start = pl.multiple_of(end - TM_SLOT, TM_SLOT)
            return end > meta_ref[e], pltpu.make_async_copy(zero_buf, xs_ref.at[pl.ds(start, TM_SLOT)], sem_z)

        def tail_tile(j):
            return pltpu.make_async_copy(zero_buf, xs_ref.at[pl.ds(pl.multiple_of(j * TM_SLOT, TM_SLOT), TM_SLOT)], sem_z)

        n_used = meta_ref[META_NT]
        for e in range(N_EXP):
            nonempty, cp = pad_tile(e)
            pl.when(nonempty)(cp.start)

        def start_tail(j, c):
            tail_tile(j).start()
            return c

        def wait_tail(j, c):
            tail_tile(j).wait()
            return c

        lax.fori_loop(n_used, N_SLOT_TILES, start_tail, 0)
        for e in range(N_EXP):
            nonempty, cp = pad_tile(e)
            pl.when(nonempty)(cp.wait)
        lax.fori_loop(n_used, N_SLOT_TILES, wait_tail, 0)

    _load_slots(slot_ref, slot_smem, sem_c)

    for t in range(TM_DISP):
        for k in range(TOP_K):
            slot = slot_smem[t * TOP_K + k]
            pltpu.make_async_copy(hn2_ref.at[pl.ds(t, 1)], xs_ref.at[pl.ds(slot, 1)], sem).start(priority=k % 2)
    for _ in range(TOP_K):
        pltpu.make_async_copy(hn2_ref, xs_ref.at[pl.ds(0, TM_DISP)], sem).wait()


def _dispatch(slot2d, meta, hn2):
    n_steps = N_TOK // TM_DISP
    return pl.pallas_call(
        _dispatch_kernel,
        out_shape=jax.ShapeDtypeStruct((N_SLOTS, D), F32),
        grid=(n_steps,),
        in_specs=[
            pl.BlockSpec(slot2d.shape, lambda i: (0, 0)),
            pl.BlockSpec(memory_space=pltpu.SMEM),
            pl.BlockSpec((TM_DISP, D), lambda i: (i, 0)),
        ],
        out_specs=pl.BlockSpec(memory_space=pl.ANY),
        scratch_shapes=[pltpu.SMEM((TM_DISP * TOP_K,), jnp.int32), pltpu.VMEM((TM_SLOT, D), F32),
                        pltpu.SemaphoreType.DMA, pltpu.SemaphoreType.DMA, pltpu.SemaphoreType.DMA],
        compiler_params=pltpu.CompilerParams(dimension_semantics=("arbitrary",), vmem_limit_bytes=VMEM_LIMIT),
        name="dispatch",
    )(slot2d, meta, hn2)


def _ffn_kernel(texp_ref, meta_ref, xs_ref, wgu_ref, bgu_ref, wd_ref, bd_ref, ys_ref, wgu_bf, wd_bf):
    i = pl.program_id(0)
    valid = i < meta_ref[META_NT]
    fresh = (i == 0) | (texp_ref[i] != texp_ref[jnp.maximum(i - 1, 0)])

    @pl.when(valid & fresh)
    def _():
        wgu_bf[...] = wgu_ref[0].astype(BF16)
        wd_bf[...] = wd_ref[0].astype(BF16)

    @pl.when(valid)
    def _():
        gu = _dot(xs_ref[...].astype(BF16), wgu_bf[...]) + bgu_ref[0]
        glu = jnp.minimum(gu[:, 0:D_FF], SWIGLU_LIMIT)
        lin = jnp.clip(gu[:, D_FF:2 * D_FF], -SWIGLU_LIMIT, SWIGLU_LIMIT)
        act = glu * _sigmoid(SWIGLU_ALPHA * glu) * (lin + 1.0)
        ys_ref[...] = _dot(act.astype(BF16), wd_bf[...]) + bd_ref[0]

    @pl.when(jnp.logical_not(valid))
    def _():
        ys_ref[...] = jnp.zeros_like(ys_ref)


def _ffn(texp, meta, xs, wgu, bgu, wd, bd):
    def used_tile(i, te, me):
        return (jnp.minimum(i, me[META_NT] - 1), 0)

    return pl.pallas_call(
        _ffn_kernel,
        out_shape=jax.ShapeDtypeStruct((N_SLOTS, D), F32),
        grid_spec=pltpu.PrefetchScalarGridSpec(
            num_scalar_prefetch=2,
            grid=(N_SLOT_TILES,),
            in_specs=[
                pl.BlockSpec((TM_SLOT, D), used_tile),
                pl.BlockSpec((1, D, 2 * D_FF), lambda i, te, me: (te[i], 0, 0)),
                pl.BlockSpec((1, 1, 2 * D_FF), lambda i, te, me: (te[i], 0, 0)),
                pl.BlockSpec((1, D_FF, D), lambda i, te, me: (te[i], 0, 0)),
                pl.BlockSpec((1, 1, D), lambda i, te, me: (te[i], 0, 0)),
            ],
            out_specs=pl.BlockSpec((TM_SLOT, D), lambda i, te, me: (i, 0)),
            scratch_shapes=[pltpu.VMEM((D, 2 * D_FF), BF16), pltpu.VMEM((D_FF, D), BF16)],
        ),
        compiler_params=pltpu.CompilerParams(dimension_semantics=("arbitrary",), vmem_limit_bytes=VMEM_LIMIT),
        name="ffn",
    )(texp, meta, xs, wgu, bgu, wd, bd)


def _combine_kernel(slot_ref, x1_ref, w4_ref, mod_ref, nfw_ref, ys_ref, oc_ref, ol_ref,
                    slot_smem, buf, sem_c, sem):
    _load_slots(slot_ref, slot_smem, sem_c)

    for t in range(TM_DISP):
        for k in range(TOP_K):
            slot = slot_smem[t * TOP_K + k]
            pltpu.make_async_copy(ys_ref.at[pl.ds(slot, 1)], buf.at[k, pl.ds(t, 1)], sem).start(priority=k % 2)
    for k in range(TOP_K):
        pltpu.make_async_copy(ys_ref.at[pl.ds(0, TM_DISP)], buf.at[k], sem).wait()
    w4 = w4_ref[...]
    moe = w4[:, 0:1] * buf[0]
    for k in range(1, TOP_K):
        moe = moe + w4[:, k:k + 1] * buf[k]
    gate2 = mod_ref[0][:, 5 * D:6 * D]
    x2 = x1_ref[...] + gate2 * moe
    ms = jnp.mean(x2 * x2, axis=-1, keepdims=True)
    out = x2 * lax.rsqrt(ms + RMS_EPS) * nfw_ref[...]
    is_ctx = pl.program_id(0) < N_CTX_TOK // TM_DISP

    @pl.when(is_ctx)
    def _():
        oc_ref[...] = out

    @pl.when(jnp.logical_not(is_ctx))
    def _():
        ol_ref[...] = out


def _combine(slot2d, x1, w4, mod3, nfw, ys):
    n_steps = N_TOK // TM_DISP
    n_ctx = N_CTX_TOK // TM_DISP
    return pl.pallas_call(
        _combine_kernel,
        out_shape=[jax.ShapeDtypeStruct((N_CTX_TOK, D), F32), jax.ShapeDtypeStruct((N_LAT_TOK, D), F32)],
        grid=(n_steps,),
        in_specs=[
            pl.BlockSpec(slot2d.shape, lambda i: (0, 0)),
            pl.BlockSpec((TM_DISP, D), lambda i: (i, 0)),
            pl.BlockSpec((TM_DISP, LANES), lambda i: (i, 0)),
            pl.BlockSpec((1, 1, N_MOD * D), lambda i: (_mod_row(i, TM_DISP), 0, 0)),
            pl.BlockSpec((1, D), lambda i: (0, 0)),
            pl.BlockSpec(memory_space=pl.ANY),
        ],
        out_specs=[pl.BlockSpec((TM_DISP, D), lambda i: (jnp.minimum(i, n_ctx - 1), 0)),
                   pl.BlockSpec((TM_DISP, D), lambda i: (jnp.maximum(i - n_ctx, 0), 0))],
        scratch_shapes=[pltpu.SMEM((TM_DISP * TOP_K,), jnp.int32), pltpu.VMEM((TOP_K, TM_DISP, D), F32),
                        pltpu.SemaphoreType.DMA, pltpu.SemaphoreType.DMA],
        compiler_params=pltpu.CompilerParams(dimension_semantics=("arbitrary",), vmem_limit_bytes=VMEM_LIMIT),
        name="combine",
    )(slot2d, x1, w4, mod3, nfw, ys)


def _pair_blockdiag(s):
    b = s.shape[0]
    s = s.reshape(b, N_PAIR, 2, HEAD, HEAD)
    z = jnp.zeros((b, N_PAIR, HEAD, HEAD), s.dtype)
    top = jnp.concatenate([s[:, :, 0], z], axis=-1)
    bot = jnp.concatenate([z, s[:, :, 1]], axis=-1)
    return jnp.concatenate([top, bot], axis=-2)


def _pair_unblock(s):
    b = s.shape[0]
    a = s[:, :, 0:HEAD, 0:HEAD]
    c = s[:, :, HEAD:, HEAD:]
    return jnp.stack([a, c], axis=2).reshape(b, 1, N_HEAD, HEAD, HEAD)


def kernel(x_prompt, x_sample, state_fwd, state_bwd, c, c_ctx, w_mod, b_mod, norm_mix_w, w_in, b_merge, pool_w, pool_scale, w_pool_out, shift_mu, decay_w0, decay_w2, iclr_a0, iclr_a2, gate_w2, k_k, k_a, r_k, ln_x_w, ln_x_b, w_rwkv_out, w_o, norm_ffn_w, router_w, router_b, expert_w_gu, expert_b_gu, expert_w_down, expert_b_down, norm_final_w):
    l = 0
    x_ctx = x_prompt.reshape(N_CTX_TOK, D)
    x_lat = x_sample.reshape(N_LAT_TOK, D)
    cvec = jnp.concatenate([c_ctx[None, :], c, jnp.zeros((MOD_ROWS - 1 - N_LAT_SEQ, D), F32)], axis=0)
    mod = _modulation(cvec, w_mod[l], b_mod[l][None, :])
    mod3 = mod.reshape(MOD_ROWS, 1, N_MOD * D)

    w_in_l = w_in[l]
    proj = _projection(x_ctx, x_lat, mod3, norm_mix_w[l][None, :], w_in_l[:, :OFF_MERGE].astype(BF16))

    zl = jnp.zeros((LORA, RW), F32)
    dw2 = jnp.stack([jnp.concatenate([decay_w2[l, 0], zl], 0), jnp.concatenate([zl, decay_w2[l, 1]], 0)])
    a2 = jnp.stack([jnp.concatenate([iclr_a2[l, 0], zl], 0), jnp.concatenate([zl, iclr_a2[l, 1]], 0)])
    hid = jnp.arange(RW) // HEAD
    seg = (hid[:, None] == hid[None, :]).astype(BF16)
    front_w = (pool_w[l].astype(BF16), pool_scale[l][None, :], shift_mu[l][None, :],
               decay_w0[l], dw2, iclr_a0[l], a2, k_k[l][None, :], k_a[l][None, :],
               r_k[l].reshape(1, RW), seg)
    f_ctx = _front(proj, False, N_CTX_SEQ, T_CTX, 0, front_w)
    f_lat = _front(proj, True, N_LAT_SEQ, T_LAT, N_CTX_TOK, front_w)

    y_ctx, sf, sb = _scan(f_ctx[2:], N_CTX_SEQ, T_CTX, None)
    init = (_pair_blockdiag(state_fwd[:, l]), _pair_blockdiag(state_bwd[:, l]))
    y_lat, _, _ = _scan(f_lat[2:], N_LAT_SEQ, T_LAT, init)

    rw_pad = jnp.concatenate([router_w[l], jnp.zeros((D, LANES - N_EXP), F32)], axis=1)
    rb_pad = jnp.concatenate([router_b[l], jnp.full((LANES - N_EXP,), -1e30, F32)])[None, :]
    post_w = (norm_mix_w[l][None, :], w_in_l[:, OFF_MERGE:].astype(BF16), b_merge[l][None, :],
              w_pool_out[l].astype(BF16), seg, ln_x_w[l][None, :], ln_x_b[l][None, :],
              gate_w2[l].astype(BF16), w_rwkv_out[l].astype(BF16), w_o[l].astype(BF16),
              norm_ffn_w[l][None, :], rw_pad, rb_pad)
    x1, hn2, code, w4, meta, texp = _post((x_ctx, x_lat), mod3, (f_ctx[0], f_lat[0]), (f_ctx[1], f_lat[1]),
                                          (y_ctx, y_lat), post_w)

    slot2d = _slots(code, meta)[:, :TOP_K].reshape(N_TOK // TM_DISP, TM_DISP * TOP_K)
    meta1 = meta.reshape(LANES)
    xs = _dispatch(slot2d, meta1, hn2)
    ys = _ffn(texp[:, 0], meta1, xs, expert_w_gu[l], expert_b_gu[l][:, None, :],
              expert_w_down[l], expert_b_down[l][:, None, :])
    out_ctx, out_lat = _combine(slot2d, x1, w4, mod3, norm_final_w[None, :], ys)
    y_prompt = out_ctx.reshape(N_CTX_SEQ, T_CTX, D)
    y_sample = out_lat.reshape(N_LAT_SEQ, T_LAT, D)
    return (y_prompt, y_sample, _pair_unblock(sf), _pair_unblock(sb))
```

```python
import functools
import math

import jax
import jax.numpy as jnp
from jax import lax
from jax.experimental import pallas as pl
from jax.experimental.pallas import tpu as pltpu

F32 = jnp.float32
BF16 = jnp.bfloat16

D = 1024
N_CTX_SEQ, T_CTX = 32, 256
N_LAT_SEQ, T_LAT = 4, 1024
GRID_W = 64
N_CTX_TOK = N_CTX_SEQ * T_CTX
N_LAT_TOK = N_LAT_SEQ * T_LAT
N_TOK = N_CTX_TOK + N_LAT_TOK

POOL_W = 512
POOL_G = 4
POOL_GD = POOL_W // POOL_G
POOL_WINDOWS = (2, 4, 8, 16)
RW = 512
HEAD = 64
N_HEAD = RW // HEAD
N_PAIR = N_HEAD // 2
LORA = 64
GATE_LORA = 128
N_EXP = 32
TOP_K = 4
D_FF = 1024
SWIGLU_ALPHA = 1.702
SWIGLU_LIMIT = 7.0
RMS_EPS = 1e-6
GN_EPS = 1e-5 * HEAD
N_MOD = 6

OFF_R = POOL_W
SHIFT_COLS = 3 * RW + 4 * LORA + GATE_LORA
OFF_MERGE = OFF_R + SHIFT_COLS
IN_COLS = OFF_MERGE + 2 * D

LANES = 128
CHUNK = 64
INV_BLOCK = 8
SCAN_TOKENS_PER_STEP = 512
ROWS = 256
HALO = 64
EXT = ROWS + 2 * HALO
TM_PROJ = 512
TM_POST = 512
TM_DISP = 1024
TM_SLOT = 512
N_SLOT_TILES = N_TOK * TOP_K // TM_SLOT + N_EXP
N_SLOTS = N_SLOT_TILES * TM_SLOT
RANK_BITS = 14
META_NT = N_EXP
MOD_ROWS = 8
VMEM_LIMIT = 56 * 1024 * 1024


def _sigmoid(x):
    return 1.0 / (1.0 + jnp.exp(-x))


def _split2(a):
    hi = a.astype(BF16)
    lo = (a - hi.astype(F32)).astype(BF16)
    return hi, lo


_NN = (((1,), (0,)), ((), ()))
_NT = (((1,), (1,)), ((), ()))


def _dot(a, b, dims=_NN):
    return lax.dot_general(a, b, dims, preferred_element_type=F32)


def _dot3(a, b, dims=_NN):
    ah, al = _split2(a)
    bh, bl = _split2(b)
    return _dot(ah, bh, dims) + (_dot(ah, bl, dims) + _dot(al, bh, dims))


def _dot_exact_lhs(a_bf16, b, passes):
    acc = None
    rem = b
    for _ in range(passes):
        part = rem.astype(BF16)
        term = _dot(a_bf16, part)
        acc = term if acc is None else acc + term
        rem = rem - part.astype(F32)
    return acc


def _dot_exact_rhs(a, b_bf16, passes):
    acc = None
    rem = a
    for _ in range(passes):
        part = rem.astype(BF16)
        term = _dot(part, b_bf16)
        acc = term if acc is None else acc + term
        rem = rem - part.astype(F32)
    return acc


def _modnorm(x, w, scale, shift):
    ms = jnp.mean(x * x, axis=-1, keepdims=True)
    return x * lax.rsqrt(ms + RMS_EPS) * w * (1.0 + scale) + shift


def _mod_row(i, tm):
    n_ctx = N_CTX_TOK // tm
    per = T_LAT // tm
    return jnp.where(i < n_ctx, 0, 1 + (i - n_ctx) // per)


def _mod_kernel(c_ref, w_ref, b_ref, o_ref):
    c = c_ref[...]
    s = c * _sigmoid(c)
    o_ref[...] = _dot3(s, w_ref[...]) + b_ref[...]


def _modulation(cvec, w_mod, b_mod):
    return pl.pallas_call(
        _mod_kernel,
        out_shape=jax.ShapeDtypeStruct((MOD_ROWS, N_MOD * D), F32),
        grid=(N_MOD,),
        in_specs=[
            pl.BlockSpec((MOD_ROWS, D), lambda j: (0, 0)),
            pl.BlockSpec((D, D), lambda j: (0, j)),
            pl.BlockSpec((1, D), lambda j: (0, j)),
        ],
        out_specs=pl.BlockSpec((MOD_ROWS, D), lambda j: (0, j)),
        compiler_params=pltpu.CompilerParams(dimension_semantics=("arbitrary",), vmem_limit_bytes=VMEM_LIMIT),
        name="mod",
    )(cvec, w_mod, b_mod)


def _path_specs(tm, n_cols=D):
    n_ctx = N_CTX_TOK // tm
    return [pl.BlockSpec((tm, n_cols), lambda i: (jnp.minimum(i, n_ctx - 1), 0)),
            pl.BlockSpec((tm, n_cols), lambda i: (jnp.maximum(i - n_ctx, 0), 0))]


def _path_tile(c_ref, l_ref, tm):
    return jnp.where(pl.program_id(0) < N_CTX_TOK // tm, c_ref[...], l_ref[...])


def _proj_kernel(xc_ref, xl_ref, mod_ref, nw_ref, w_ref, o_ref):
    mod = mod_ref[0]
    hn = _modnorm(_path_tile(xc_ref, xl_ref, TM_PROJ), nw_ref[...], mod[:, D:2 * D], mod[:, 0:D])
    o_ref[...] = _dot(hn.astype(BF16), w_ref[...])


def _projection(x_ctx, x_lat, mod3, norm_w, w_a):
    n_cols = w_a.shape[1]
    return pl.pallas_call(
        _proj_kernel,
        out_shape=jax.ShapeDtypeStruct((N_TOK, n_cols), F32),
        grid=(N_TOK // TM_PROJ,),
        in_specs=_path_specs(TM_PROJ) + [
            pl.BlockSpec((1, 1, N_MOD * D), lambda i: (_mod_row(i, TM_PROJ), 0, 0)),
            pl.BlockSpec((1, D), lambda i: (0, 0)),
            pl.BlockSpec((D, n_cols), lambda i: (0, 0)),
        ],
        out_specs=pl.BlockSpec((TM_PROJ, n_cols), lambda i: (i, 0)),
        compiler_params=pltpu.CompilerParams(dimension_semantics=("arbitrary",), vmem_limit_bytes=VMEM_LIMIT),
        name="proj",
    )(x_ctx, x_lat, mod3, norm_w, w_a)


N_FRONT_IN = 14
N_FRONT_OUT = 12


def _front_kernel(grid_mode, n_chunks, seq_len, *refs):
    (cur_ref, prev_ref, next_ref, poolw_ref, pscale_ref, mu_ref,
     dw0_ref, dw2_ref, a0_ref, a2_ref, kk_ref, ka_ref, rk_ref, seg_ref) = refs[:N_FRONT_IN]
    (ypre_ref, gsig_ref, r_ref, v_ref, kkn_ref, bonus_ref,
     lwf_ref, lwb_ref, kdf_ref, kdb_ref, bf_ref, bb_ref) = refs[-N_FRONT_OUT:]
    c = pl.program_id(1)
    cur = cur_ref[...]
    prev = jnp.where(c > 0, prev_ref[...], 0.0)
    nxt = jnp.where(c < n_chunks - 1, next_ref[...], 0.0)
    ext = jnp.concatenate([prev, cur, nxt], axis=0)

    def down(x, s):
        return pltpu.roll(x, s, 0)

    def up(x, s):
        return pltpu.roll(x, EXT - s, 0)

    def mid(x):
        return x[HALO:HALO + ROWS]

    row = lax.broadcasted_iota(jnp.int32, (ROWS, 1), 0)
    t_seq = c * ROWS + row

    outs = []
    for gi, win in enumerate(POOL_WINDOWS):
        h = win // 2
        u = ext[:, gi * POOL_GD:(gi + 1) * POOL_GD]
        trail, lead, s = u, u, 1
        while s < h:
            trail = trail + down(trail, s)
            lead = lead + up(lead, s)
            s *= 2
        total = mid(down(trail, 1) + lead)
        cnt = (jnp.minimum(t_seq + h, seq_len) - jnp.maximum(t_seq - h, 0)).astype(F32)
        pooled = total / cnt - mid(u)
        outs.append(_dot(pooled.astype(BF16), poolw_ref[gi]))
    mixed = jnp.concatenate(outs, axis=1) * pscale_ref[...]
    ypre_ref[...] = mixed.astype(BF16)

    pe = ext[:, OFF_R:OFF_MERGE]
    p = mid(pe)
    lane = lax.broadcasted_iota(jnp.int32, (1, SHIFT_COLS), 1)
    if grid_mode:
        col = t_seq % GRID_W
        left = jnp.where(col > 0, mid(down(pe, 1)), 0.0)
        right = jnp.where(col < GRID_W - 1, mid(up(pe, 1)), 0.0)
        upn = pe[0:ROWS]
        dnn = pe[2 * HALO:2 * HALO + ROWS]
        q = lane % 4
        sh = jnp.where(q == 0, left, jnp.where(q == 1, right, jnp.where(q == 2, upn, dnn)))
    else:
        sh = jnp.where(lane % 2 == 0, mid(down(pe, 1)), mid(up(pe, 1)))
    pm = p + mu_ref[...] * (sh - p)

    r = pm[:, 0:RW]
    k = pm[:, RW:2 * RW]
    v = pm[:, 2 * RW:3 * RW]
    o = 3 * RW
    w_lo = jnp.tanh(pm[:, o:o + 2 * LORA])
    a_lo = pm[:, o + 2 * LORA:o + 4 * LORA]
    gsig_ref[...] = _sigmoid(pm[:, o + 4 * LORA:o + 4 * LORA + GATE_LORA]).astype(BF16)

    seg = seg_ref[...]
    kkr = k * kk_ref[...]
    ss = _dot_exact_rhs(kkr * kkr, seg, 2)
    kkn = kkr / jnp.maximum(jnp.sqrt(ss), 1e-12)
    r_ref[...] = r
    v_ref[...] = v
    kkn_ref[...] = kkn

    bonus = None
    for d, (lw_ref, kd_ref, b_ref) in enumerate(((lwf_ref, kdf_ref, bf_ref), (lwb_ref, kdb_ref, bb_ref))):
        z = dw0_ref[d:d + 1, :] + _dot3(w_lo, dw2_ref[d])
        lw_ref[...] = -math.exp(-0.5) * _sigmoid(z)
        a = _sigmoid(a0_ref[d:d + 1, :] + _dot3(a_lo, a2_ref[d]))
        kd = k * (1.0 + (a - 1.0) * ka_ref[...])
        kd_ref[...] = kd
        b_ref[...] = kkn * a
        bd = _dot_exact_rhs(r * kd * rk_ref[...], seg, 2) * v
        bonus = bd if bonus is None else bonus + bd
    bonus_ref[...] = bonus


def _front(proj, grid_mode, n_seq, seq_len, tok0, weights):
    n_chunks = seq_len // ROWS
    n_tok = n_seq * seq_len
    blk0 = tok0 // ROWS
    hpb = ROWS // HALO
    last_halo = N_TOK // HALO - 1

    def cur_map(b, c):
        return (blk0 + b * n_chunks + c, 0)

    def prev_map(b, c):
        return (jnp.maximum((blk0 + b * n_chunks + c) * hpb - 1, 0), 0)

    def next_map(b, c):
        return (jnp.minimum((blk0 + b * n_chunks + c + 1) * hpb, last_halo), 0)

    def full(a):
        nd = a.ndim
        return pl.BlockSpec(a.shape, lambda b, c, _nd=nd: (0,) * _nd)

    def out_map(b, c):
        return (b * n_chunks + c, 0)

    f32_out = jax.ShapeDtypeStruct((n_tok, RW), F32)
    out_shape = [jax.ShapeDtypeStruct((n_tok, POOL_W), BF16), jax.ShapeDtypeStruct((n_tok, GATE_LORA), BF16)]
    out_shape += [f32_out] * (N_FRONT_OUT - 2)
    out_specs = [pl.BlockSpec((ROWS, POOL_W), out_map), pl.BlockSpec((ROWS, GATE_LORA), out_map)]
    out_specs += [pl.BlockSpec((ROWS, RW), out_map)] * (N_FRONT_OUT - 2)
    assert len(weights) == N_FRONT_IN - 3
    return pl.pallas_call(
        functools.partial(_front_kernel, grid_mode, n_chunks, seq_len),
        out_shape=out_shape,
        grid=(n_seq, n_chunks),
        in_specs=[
            pl.BlockSpec((ROWS, OFF_MERGE), cur_map),
            pl.BlockSpec((HALO, OFF_MERGE), prev_map),
            pl.BlockSpec((HALO, OFF_MERGE), next_map),
        ] + [full(a) for a in weights],
        out_specs=out_specs,
        compiler_params=pltpu.CompilerParams(dimension_semantics=("arbitrary", "arbitrary"),
                                             vmem_limit_bytes=VMEM_LIMIT),
        name="front_grid" if grid_mode else "front_ctx",
    )(proj, proj, proj, *weights)


def _scan_kernel(n_chunks, seqs_per_step, has_init, *refs):
    (r_ref, v_ref, kk_ref, bonus_ref, lwf_ref, lwb_ref, kdf_ref, kdb_ref, bf_ref, bb_ref) = refs[:10]
    if has_init:
        s0f_ref, s0b_ref = refs[10:12]
    y_ref, sf_ref, sb_ref = refs[-3:]
    C = CHUNK
    P2 = 2 * C
    ri = lax.broadcasted_iota(jnp.int32, (P2, P2), 0)
    ci = lax.broadcasted_iota(jnp.int32, (P2, P2), 1)
    same = (ri // C) == (ci // C)
    rp, cp = ri % C, ci % C
    eye = (ri == ci).astype(F32)
    blk = {}
    s = INV_BLOCK
    while s <= C:
        blk[s] = (ri // s) == (ci // s)
        s *= 2
    ti = lax.broadcasted_iota(jnp.int32, (C, C), 0)
    tj = lax.broadcasted_iota(jnp.int32, (C, C), 1)
    head_a = lax.broadcasted_iota(jnp.int32, (C, LANES), 1) < HEAD

    def stack(x):
        return jnp.concatenate([jnp.where(head_a, x, 0.0), jnp.where(head_a, 0.0, x)], axis=0)

    def d3(a, b, dims=_NN):
        return _dot(a[0], b[0], dims) + (_dot(a[0], b[1], dims) + _dot(a[1], b[0], dims))

    def d2(a_bf16, b):
        return _dot(a_bf16, b[0]) + _dot(a_bf16, b[1])

    def chunks(chains):
        n = range(len(chains))
        masks, sls = [], []
        for (s_ref, q, p, row0, lw_ref, kd_ref, b_ref, reverse) in chains:
            if reverse:
                masks.append((same & (cp > rp), same & (cp >= rp), (tj >= ti), 0))
            else:
                masks.append((same & (cp < rp), same & (cp <= rp), (tj <= ti), C - 1))
            sls.append((pl.ds(row0, C), slice(p * LANES, (p + 1) * LANES)))
        lw = [chains[i][4][sls[i]] for i in n]
        cum = [_dot_exact_lhs(masks[i][2].astype(BF16), lw[i], 3) for i in n]
        e_i = [jnp.exp(cum[i]) for i in n]
        e_n = [jnp.exp(-cum[i]) for i in n]
        QR = [jnp.concatenate([stack(kk_ref[sls[i]] * jnp.exp(cum[i] - lw[i])),
                               stack(r_ref[sls[i]] * e_i[i])], axis=0).astype(BF16) for i in n]
        BK = [jnp.concatenate([stack(chains[i][6][sls[i]] * e_n[i]),
                               stack(chains[i][5][sls[i]] * e_n[i])], axis=0).astype(BF16) for i in n]
        v_st = [stack(v_ref[sls[i]]) for i in n]
        Vs = [v_st[i].astype(BF16) for i in n]
        S = [chains[i][0][chains[i][1], chains[i][2]] for i in n]
        G = [_dot(QR[i], BK[i], _NT) for i in n]
        L = [jnp.where(masks[i][0], G[i][0:P2, 0:P2], 0.0) for i in n]
        Lk = [jnp.where(masks[i][0], G[i][0:P2, P2:2 * P2], 0.0) for i in n]
        RBK = [jnp.concatenate([jnp.where(masks[i][1], G[i][P2:2 * P2, 0:P2], 0.0),
                                jnp.where(masks[i][1], G[i][P2:2 * P2, P2:2 * P2], 0.0)], axis=1) for i in n]
        L0 = [jnp.where(blk[INV_BLOCK], L[i], 0.0) for i in n]
        L0h = [L0[i].astype(BF16) for i in n]
        X = [eye - L0[i] for i in n]
        Pw = [_split2(_dot(L0h[i], L0h[i])) for i in n]
        X = [X[i] + d2(X[i].astype(BF16), Pw[i]) for i in n]
        s = 4
        while s < INV_BLOCK:
            Pw = [_split2(d3(Pw[i], Pw[i])) for i in n]
            X = [X[i] + d3(_split2(X[i]), Pw[i]) for i in n]
            s *= 2
        s = INV_BLOCK
        while s < C:
            Dm = [_split2(X[i]) for i in n]
            Loff = [jnp.where(blk[2 * s] & jnp.logical_not(blk[s]), L[i], 0.0).astype(BF16) for i in n]
            E = [d2(Loff[i], Dm[i]) for i in n]
            X = [X[i] - d3(Dm[i], _split2(E[i])) for i in n]
            s *= 2
        QRA = [_dot(QR[i], S[i].astype(BF16), _NT) for i in n]
        W = [QRA[i][0:P2] + _dot(Lk[i].astype(BF16), Vs[i]) for i in n]
        U = [-_dot(X[i].astype(BF16), W[i].astype(BF16)) for i in n]
        UV = [jnp.concatenate([U[i], v_st[i]], axis=0) for i in n]
        Ys = [QRA[i][P2:2 * P2] + _dot(RBK[i].astype(BF16), UV[i].astype(BF16)) for i in n]
        dS = [_dot(UV[i].T.astype(BF16), BK[i]) for i in n]
        for i in n:
            last = masks[i][3]
            chains[i][0][chains[i][1], chains[i][2]] = (S[i] + dS[i]) * e_i[i][last:last + 1, :]
            y_ref[sls[i]] += Ys[i][0:C] + Ys[i][C:P2]

    y_ref[...] = bonus_ref[...]
    if has_init:
        sf_ref[...] = s0f_ref[...]
        sb_ref[...] = s0b_ref[...]
    else:
        sf_ref[...] = jnp.zeros_like(sf_ref)
        sb_ref[...] = jnp.zeros_like(sb_ref)

    seq_len = n_chunks * C

    def body(c, carry):
        chains = []
        for q in range(seqs_per_step):
            rf = pl.multiple_of(q * seq_len + c * C, C)
            rb = pl.multiple_of(q * seq_len + (n_chunks - 1 - c) * C, C)
            for p in range(N_PAIR):
                chains.append((sf_ref, q, p, rf, lwf_ref, kdf_ref, bf_ref, False))
                chains.append((sb_ref, q, p, rb, lwb_ref, kdb_ref, bb_ref, True))
        chunks(chains)
        return carry

    lax.fori_loop(0, n_chunks, body, 0)


def _scan(arrs, n_seq, seq_len, init):
    n_tok = n_seq * seq_len
    n_chunks = seq_len // CHUNK
    sps = max(1, SCAN_TOKENS_PER_STEP // seq_len)
    tok_spec = pl.BlockSpec((sps * seq_len, RW), lambda b: (b, 0))
    st_spec = pl.BlockSpec((sps, N_PAIR, LANES, LANES), lambda b: (b, 0, 0, 0))
    st_shape = jax.ShapeDtypeStruct((n_seq, N_PAIR, LANES, LANES), F32)
    ins = list(arrs)
    in_specs = [tok_spec] * 10
    if init is not None:
        ins += list(init)
        in_specs += [st_spec, st_spec]
    return pl.pallas_call(
        functools.partial(_scan_kernel, n_chunks, sps, init is not None),
        out_shape=[jax.ShapeDtypeStruct((n_tok, RW), F32), st_shape, st_shape],
        grid=(n_seq // sps,),
        in_specs=in_specs,
        out_specs=[tok_spec, st_spec, st_spec],
        compiler_params=pltpu.CompilerParams(dimension_semantics=("arbitrary",), vmem_limit_bytes=VMEM_LIMIT),
        name="scan_init" if init is not None else "scan_zero",
    )(*ins)


def _post_kernel(xc_ref, xl_ref, mod_ref, yprec_ref, yprel_ref, gsigc_ref, gsigl_ref, yc_ref, yl_ref,
                 nmw_ref, wm_ref, bm_ref, wpo_ref, seg_ref, lnw_ref, lnb_ref, gw2_ref, wro_ref, wo_ref,
                 nfw_ref, rw_ref, rb_ref,
                 x1_ref, hn2_ref, code_ref, w4_ref, meta_ref, texp_ref, carry_ref):
    i = pl.program_id(0)

    @pl.when(i == 0)
    def _():
        carry_ref[...] = jnp.zeros_like(carry_ref)

    x = _path_tile(xc_ref, xl_ref, TM_POST)
    mod = mod_ref[0]
    shift1, scale1, gate1 = mod[:, 0:D], mod[:, D:2 * D], mod[:, 2 * D:3 * D]
    shift2, scale2 = mod[:, 3 * D:4 * D], mod[:, 4 * D:5 * D]
    hn = _modnorm(x, nmw_ref[...], scale1, shift1)
    merge = _sigmoid(_dot(hn.astype(BF16), wm_ref[...]) + bm_ref[...])
    y_pool = _dot(_path_tile(yprec_ref, yprel_ref, TM_POST), wpo_ref[...])

    y = _path_tile(yc_ref, yl_ref, TM_POST)
    seg = seg_ref[...]
    mu = _dot_exact_rhs(y, seg, 2) * (1.0 / HEAD)
    yc = y - mu
    var = _dot_exact_rhs(yc * yc, seg, 2) * (1.0 / HEAD)
    yn = yc * lax.rsqrt(var + GN_EPS) * lnw_ref[...] + lnb_ref[...]
    g = _dot(_path_tile(gsigc_ref, gsigl_ref, TM_POST), gw2_ref[...])
    y_rwkv = _dot((yn * g).astype(BF16), wro_ref[...])

    mixed = merge[:, 0:D] * y_pool + merge[:, D:2 * D] * y_rwkv
    x1 = x + gate1 * _dot(mixed.astype(BF16), wo_ref[...])
    x1_ref[...] = x1
    hn2 = _modnorm(x1, nfw_ref[...], scale2, shift2)
    hn2_ref[...] = hn2

    logits = _dot3(hn2, rw_ref[...]) + rb_ref[...]
    lane = lax.broadcasted_iota(jnp.int32, logits.shape, 1)
    work = logits
    sel = None
    top = None
    for j in range(TOP_K):
        m = jnp.max(work, axis=-1, keepdims=True)
        if j == 0:
            top = m
        idx = jnp.min(jnp.where(work == m, lane, LANES), axis=-1, keepdims=True)
        pick = lane == idx
        sel = pick if sel is None else (sel | pick)
        work = jnp.where(pick, -jnp.inf, work)
    e = jnp.where(sel, jnp.exp(logits - top), 0.0)
    comb = e / jnp.sum(e, axis=-1, keepdims=True)

    tm = logits.shape[0]
    sel_b = jnp.where(sel, 1.0, 0.0).astype(BF16)
    before = (lax.broadcasted_iota(jnp.int32, (tm, tm), 1) < lax.broadcasted_iota(jnp.int32, (tm, tm), 0))
    rank = carry_ref[...] + _dot(before.astype(BF16), sel_b)
    carry = carry_ref[...] + jnp.sum(sel_b.astype(F32), axis=0, keepdims=True)
    carry_ref[...] = carry
    lower_e = (lax.broadcasted_iota(jnp.int32, (LANES, LANES), 0) < lax.broadcasted_iota(jnp.int32, (LANES, LANES), 1))
    rowpos = _dot(sel_b, lower_e.astype(BF16))
    code = jnp.zeros(logits.shape, jnp.int32)
    w4 = jnp.zeros(logits.shape, F32)
    for k in range(TOP_K):
        mk = sel & (rowpos == float(k))
        ek = jnp.sum(jnp.where(mk, lane, 0), axis=-1, keepdims=True)
        rk = jnp.sum(jnp.where(mk, rank, 0.0), axis=-1, keepdims=True).astype(jnp.int32)
        wk = jnp.sum(jnp.where(mk, comb, 0.0), axis=-1, keepdims=True)
        code = jnp.where(lane == k, ek * (1 << RANK_BITS) + rk, code)
        w4 = jnp.where(lane == k, wk, w4)
    code_ref[...] = code
    w4_ref[...] = w4

    tiles = jnp.floor((carry + (TM_SLOT - 1)) * (1.0 / TM_SLOT))
    incl = (lax.broadcasted_iota(jnp.int32, (LANES, LANES), 0) <= lax.broadcasted_iota(jnp.int32, (LANES, LANES), 1))
    tiles8 = jnp.broadcast_to(tiles, (8, LANES)).astype(BF16)
    cum = _dot(tiles8, incl.astype(BF16))[0:1]
    lane1 = lax.broadcasted_iota(jnp.int32, (1, LANES), 1)
    offs = ((cum - tiles) * TM_SLOT).astype(jnp.int32)
    n_used = jnp.sum(jnp.where(lane1 == N_EXP - 1, cum, 0.0), axis=-1, keepdims=True).astype(jnp.int32)
    meta_ref[...] = jnp.where(lane1 == META_NT, n_used, jnp.where(lane1 < N_EXP, offs, 0))
    tile_id = lax.broadcasted_iota(jnp.int32, (N_SLOT_TILES, LANES), 0).astype(F32)
    done = jnp.where((lax.broadcasted_iota(jnp.int32, (N_SLOT_TILES, LANES), 1) < N_EXP) & (cum <= tile_id), 1, 0)
    texp = jnp.minimum(jnp.sum(done, axis=-1, keepdims=True), N_EXP - 1)
    texp_ref[...] = jnp.broadcast_to(texp, (N_SLOT_TILES, LANES))


def _post(x, mod3, ypre, gsig, y, weights):
    def tile(n, dt):
        return pl.BlockSpec((TM_POST, n), lambda i: (i, 0))

    def full(a):
        nd = a.ndim
        return pl.BlockSpec(a.shape, lambda i, _nd=nd: (0,) * _nd)

    def const(shape):
        return pl.BlockSpec(shape, lambda i: (0, 0))

    return pl.pallas_call(
        _post_kernel,
        out_shape=[jax.ShapeDtypeStruct((N_TOK, D), F32), jax.ShapeDtypeStruct((N_TOK, D), F32),
                   jax.ShapeDtypeStruct((N_TOK, LANES), jnp.int32), jax.ShapeDtypeStruct((N_TOK, LANES), F32),
                   jax.ShapeDtypeStruct((1, LANES), jnp.int32),
                   jax.ShapeDtypeStruct((N_SLOT_TILES, LANES), jnp.int32)],
        grid=(N_TOK // TM_POST,),
        in_specs=_path_specs(TM_POST) + [
            pl.BlockSpec((1, 1, N_MOD * D), lambda i: (_mod_row(i, TM_POST), 0, 0)),
        ] + _path_specs(TM_POST, POOL_W) + _path_specs(TM_POST, GATE_LORA) + _path_specs(TM_POST, RW)
        + [full(a) for a in weights],
        out_specs=[tile(D, F32), tile(D, F32), tile(LANES, jnp.int32), tile(LANES, F32),
                   const((1, LANES)), const((N_SLOT_TILES, LANES))],
        scratch_shapes=[pltpu.VMEM((1, LANES), F32)],
        compiler_params=pltpu.CompilerParams(dimension_semantics=("arbitrary",), vmem_limit_bytes=VMEM_LIMIT),
        name="post",
    )(*x, mod3, *ypre, *gsig, *y, *weights)


def _slots_kernel(code_ref, meta_ref, slot_ref):
    code = code_ref[...]
    meta = meta_ref[...]
    expert = code >> RANK_BITS
    off = jnp.zeros(code.shape, jnp.int32)
    for e in range(N_EXP):
        off = jnp.where(expert == e, meta[:, e:e + 1], off)
    slot_ref[...] = off + (code & ((1 << RANK_BITS) - 1))


def _slots(code_dense, meta):
    return pl.pallas_call(
        _slots_kernel,
        out_shape=jax.ShapeDtypeStruct(code_dense.shape, jnp.int32),
        grid=(1,),
        in_specs=[pl.BlockSpec(code_dense.shape, lambda i: (0, 0)), pl.BlockSpec((1, LANES), lambda i: (0, 0))],
        out_specs=pl.BlockSpec(code_dense.shape, lambda i: (0, 0)),
        compiler_params=pltpu.CompilerParams(dimension_semantics=("arbitrary",), vmem_limit_bytes=VMEM_LIMIT),
        name="slots",
    )(code_dense, meta)


def _load_slots(slot_ref, slot_smem, sem):
    cp = pltpu.make_async_copy(slot_ref.at[pl.program_id(0)], slot_smem, sem)
    cp.start()
    cp.wait()


def _dispatch_kernel(slot_ref, meta_ref, hn2_ref, xs_ref, slot_smem, zero_buf, sem_c, sem_z, sem):
    @pl.when(pl.program_id(0) == 0)
    def _():
        zero_buf[...] = jnp.zeros_like(zero_buf)

        def pad_tile(e):
            end = meta_ref[e + 1] if e + 1 < N_EXP else meta_ref[META_NT] * TM_SLOT
            start = pl.multiple_of(end - TM_SLOT, TM_SLOT)
            return end > meta_ref[e], pltpu.make_async_copy(zero_buf, xs_ref.at[pl.ds(start, TM_SLOT)], sem_z)

        def tail_tile(j):
            return pltpu.make_async_copy(zero_buf, xs_ref.at[pl.ds(pl.multiple_of(j * TM_SLOT, TM_SLOT), TM_SLOT)], sem_z)

        n_used = meta_ref[META_NT]
        for e in range(N_EXP):
            nonempty, cp = pad_tile(e)
            pl.when(nonempty)(cp.start)

        def start_tail(j, c):
            tail_tile(j).start()
            return c

        def wait_tail(j, c):
            tail_tile(j).wait()
            return c

        lax.fori_loop(n_used, N_SLOT_TILES, start_tail, 0)
        for e in range(N_EXP):
            nonempty, cp = pad_tile(e)
            pl.when(nonempty)(cp.wait)
        lax.fori_loop(n_used, N_SLOT_TILES, wait_tail, 0)

    _load_slots(slot_ref, slot_smem, sem_c)

    for t in range(TM_DISP):
        for k in range(TOP_K):
            slot = slot_smem[t * TOP_K + k]
            pltpu.make_async_copy(hn2_ref.at[pl.ds(t, 1)], xs_ref.at[pl.ds(slot, 1)], sem).start(priority=k % 2)
    for _ in range(TOP_K):
        pltpu.make_async_copy(hn2_ref, xs_ref.at[pl.ds(0, TM_DISP)], sem).wait()


def _dispatch(slot2d, meta, hn2):
    n_steps = N_TOK // TM_DISP
    return pl.pallas_call(
        _dispatch_kernel,
        out_shape=jax.ShapeDtypeStruct((N_SLOTS, D), F32),
        grid=(n_steps,),
        in_specs=[
            pl.BlockSpec(slot2d.shape, lambda i: (0, 0)),
            pl.BlockSpec(memory_space=pltpu.SMEM),
            pl.BlockSpec((TM_DISP, D), lambda i: (i, 0)),
        ],
        out_specs=pl.BlockSpec(memory_space=pl.ANY),
        scratch_shapes=[pltpu.SMEM((TM_DISP * TOP_K,), jnp.int32), pltpu.VMEM((TM_SLOT, D), F32),
                        pltpu.SemaphoreType.DMA, pltpu.SemaphoreType.DMA, pltpu.SemaphoreType.DMA],
        compiler_params=pltpu.CompilerParams(dimension_semantics=("arbitrary",), vmem_limit_bytes=VMEM_LIMIT),
        name="dispatch",
    )(slot2d, meta, hn2)


def _ffn_kernel(texp_ref, meta_ref, xs_ref, wgu_ref, bgu_ref, wd_ref, bd_ref, ys_ref, wgu_bf, wd_bf):
    i = pl.program_id(0)
    valid = i < meta_ref[META_NT]
    fresh = (i == 0) | (texp_ref[i] != texp_ref[jnp.maximum(i - 1, 0)])

    @pl.when(valid & fresh)
    def _():
        wgu_bf[...] = wgu_ref[0].astype(BF16)
        wd_bf[...] = wd_ref[0].astype(BF16)

    @pl.when(valid)
    def _():
        gu = _dot(xs_ref[...].astype(BF16), wgu_bf[...]) + bgu_ref[0]
        glu = jnp.minimum(gu[:, 0:D_FF], SWIGLU_LIMIT)
        lin = jnp.clip(gu[:, D_FF:2 * D_FF], -SWIGLU_LIMIT, SWIGLU_LIMIT)
        act = glu * _sigmoid(SWIGLU_ALPHA * glu) * (lin + 1.0)
        ys_ref[...] = _dot(act.astype(BF16), wd_bf[...]) + bd_ref[0]

    @pl.when(jnp.logical_not(valid))
    def _():
        ys_ref[...] = jnp.zeros_like(ys_ref)


def _ffn(texp, meta, xs, wgu, bgu, wd, bd):
    def used_tile(i, te, me):
        return (jnp.minimum(i, me[META_NT] - 1), 0)

    return pl.pallas_call(
        _ffn_kernel,
        out_shape=jax.ShapeDtypeStruct((N_SLOTS, D), F32),
        grid_spec=pltpu.PrefetchScalarGridSpec(
            num_scalar_prefetch=2,
            grid=(N_SLOT_TILES,),
            in_specs=[
                pl.BlockSpec((TM_SLOT, D), used_tile),
                pl.BlockSpec((1, D, 2 * D_FF), lambda i, te, me: (te[i], 0, 0)),
                pl.BlockSpec((1, 1, 2 * D_FF), lambda i, te, me: (te[i], 0, 0)),
                pl.BlockSpec((1, D_FF, D), lambda i, te, me: (te[i], 0, 0)),
                pl.BlockSpec((1, 1, D), lambda i, te, me: (te[i], 0, 0)),
            ],
            out_specs=pl.BlockSpec((TM_SLOT, D), lambda i, te, me: (i, 0)),
            scratch_shapes=[pltpu.VMEM((D, 2 * D_FF), BF16), pltpu.VMEM((D_FF, D), BF16)],
        ),
        compiler_params=pltpu.CompilerParams(dimension_semantics=("arbitrary",), vmem_limit_bytes=VMEM_LIMIT),
        name="ffn",
    )(texp, meta, xs, wgu, bgu, wd, bd)


def _combine_kernel(slot_ref, x1_ref, w4_ref, mod_ref, nfw_ref, ys_ref, oc_ref, ol_ref,
                    slot_smem, buf, sem_c, sem):
    _load_slots(slot_ref, slot_smem, sem_c)

    for t in range(TM_DISP):
        for k in range(TOP_K):
            slot = slot_smem[t * TOP_K + k]
            pltpu.make_async_copy(ys_ref.at[pl.ds(slot, 1)], buf.at[k, pl.ds(t, 1)], sem).start(priority=k % 2)
    for k in range(TOP_K):
        pltpu.make_async_copy(ys_ref.at[pl.ds(0, TM_DISP)], buf.at[k], sem).wait()
    w4 = w4_ref[...]
    moe = w4[:, 0:1] * buf[0]
    for k in range(1, TOP_K):
        moe = moe + w4[:, k:k + 1] * buf[k]
    gate2 = mod_ref[0][:, 5 * D:6 * D]
    x2 = x1_ref[...] + gate2 * moe
    ms = jnp.mean(x2 * x2, axis=-1, keepdims=True)
    out = x2 * lax.rsqrt(ms + RMS_EPS) * nfw_ref[...]
    is_ctx = pl.program_id(0) < N_CTX_TOK // TM_DISP

    @pl.when(is_ctx)
    def _():
        oc_ref[...] = out

    @pl.when(jnp.logical_not(is_ctx))
    def _():
        ol_ref[...] = out


def _combine(slot2d, x1, w4, mod3, nfw, ys):
    n_steps = N_TOK // TM_DISP
    n_ctx = N_CTX_TOK // TM_DISP
    return pl.pallas_call(
        _combine_kernel,
        out_shape=[jax.ShapeDtypeStruct((N_CTX_TOK, D), F32), jax.ShapeDtypeStruct((N_LAT_TOK, D), F32)],
        grid=(n_steps,),
        in_specs=[
            pl.BlockSpec(slot2d.shape, lambda i: (0, 0)),
            pl.BlockSpec((TM_DISP, D), lambda i: (i, 0)),
            pl.BlockSpec((TM_DISP, LANES), lambda i: (i, 0)),
            pl.BlockSpec((1, 1, N_MOD * D), lambda i: (_mod_row(i, TM_DISP), 0, 0)),
            pl.BlockSpec((1, D), lambda i: (0, 0)),
            pl.BlockSpec(memory_space=pl.ANY),
        ],
        out_specs=[pl.BlockSpec((TM_DISP, D), lambda i: (jnp.minimum(i, n_ctx - 1), 0)),
                   pl.BlockSpec((TM_DISP, D), lambda i: (jnp.maximum(i - n_ctx, 0), 0))],
        scratch_shapes=[pltpu.SMEM((TM_DISP * TOP_K,), jnp.int32), pltpu.VMEM((TOP_K, TM_DISP, D), F32),
                        pltpu.SemaphoreType.DMA, pltpu.SemaphoreType.DMA],
        compiler_params=pltpu.CompilerParams(dimension_semantics=("arbitrary",), vmem_limit_bytes=VMEM_LIMIT),
        name="combine",
    )(slot2d, x1, w4, mod3, nfw, ys)


def _pair_blockdiag(s):
    b = s.shape[0]
    s = s.reshape(b, N_PAIR, 2, HEAD, HEAD)
    z = jnp.zeros((b, N_PAIR, HEAD, HEAD), s.dtype)
    top = jnp.concatenate([s[:, :, 0], z], axis=-1)
    bot = jnp.concatenate([z, s[:, :, 1]], axis=-1)
    return jnp.concatenate([top, bot], axis=-2)


def _pair_unblock(s):
    b = s.shape[0]
    a = s[:, :, 0:HEAD, 0:HEAD]
    c = s[:, :, HEAD:, HEAD:]
    return jnp.stack([a, c], axis=2).reshape(b, 1, N_HEAD, HEAD, HEAD)


def kernel(x_prompt, x_sample, state_fwd, state_bwd, c, c_ctx, w_mod, b_mod, norm_mix_w, w_in, b_merge, pool_w, pool_scale, w_pool_out, shift_mu, decay_w0, decay_w2, iclr_a0, iclr_a2, gate_w2, k_k, k_a, r_k, ln_x_w, ln_x_b, w_rwkv_out, w_o, norm_ffn_w, router_w, router_b, expert_w_gu, expert_b_gu, expert_w_down, expert_b_down, norm_final_w):
    l = 0
    x_ctx = x_prompt.reshape(N_CTX_TOK, D)
    x_lat = x_sample.reshape(N_LAT_TOK, D)
    cvec = jnp.concatenate([c_ctx[None, :], c, jnp.zeros((MOD_ROWS - 1 - N_LAT_SEQ, D), F32)], axis=0)
    mod = _modulation(cvec, w_mod[l], b_mod[l][None, :])
    mod3 = mod.reshape(MOD_ROWS, 1, N_MOD * D)

    w_in_l = w_in[l]
    proj = _projection(x_ctx, x_lat, mod3, norm_mix_w[l][None, :], w_in_l[:, :OFF_MERGE].astype(BF16))

    zl = jnp.zeros((LORA, RW), F32)
    dw2 = jnp.stack([jnp.concatenate([decay_w2[l, 0], zl], 0), jnp.concatenate([zl, decay_w2[l, 1]], 0)])
    a2 = jnp.stack([jnp.concatenate([iclr_a2[l, 0], zl], 0), jnp.concatenate([zl, iclr_a2[l, 1]], 0)])
    hid = jnp.arange(RW) // HEAD
    seg = (hid[:, None] == hid[None, :]).astype(BF16)
    front_w = (pool_w[l].astype(BF16), pool_scale[l][None, :], shift_mu[l][None, :],
               decay_w0[l], dw2, iclr_a0[l], a2, k_k[l][None, :], k_a[l][None, :],
               r_k[l].reshape(1, RW), seg)
    f_ctx = _front(proj, False, N_CTX_SEQ, T_CTX, 0, front_w)
    f_lat = _front(proj, True, N_LAT_SEQ, T_LAT, N_CTX_TOK, front_w)

    y_ctx, sf, sb = _scan(f_ctx[2:], N_CTX_SEQ, T_CTX, None)
    init = (_pair_blockdiag(state_fwd[:, l]), _pair_blockdiag(state_bwd[:, l]))
    y_lat, _, _ = _scan(f_lat[2:], N_LAT_SEQ, T_LAT, init)

    rw_pad = jnp.concatenate([router_w[l], jnp.zeros((D, LANES - N_EXP), F32)], axis=1)
    rb_pad = jnp.concatenate([router_b[l], jnp.full((LANES - N_EXP,), -1e30, F32)])[None, :]
    post_w = (norm_mix_w[l][None, :], w_in_l[:, OFF_MERGE:].astype(BF16), b_merge[l][None, :],
              w_pool_out[l].astype(BF16), seg, ln_x_w[l][None, :], ln_x_b[l][None, :],
              gate_w2[l].astype(BF16), w_rwkv_out[l].astype(BF16), w_o[l].astype(BF16),
              norm_ffn_w[l][None, :], rw_pad, rb_pad)
    x1, hn2, code, w4, meta, texp = _post((x_ctx, x_lat), mod3, (f_ctx[0], f_lat[0]), (f_ctx[1], f_lat[1]),
                                          (y_ctx, y_lat), post_w)

    code_dense = code[:, :TOP_K].reshape(N_TOK * TOP_K // LANES, LANES)
    slot2d = _slots(code_dense, meta).reshape(N_TOK // TM_DISP, TM_DISP * TOP_K)
    meta1 = meta.reshape(LANES)
    xs = _dispatch(slot2d, meta1, hn2)
    ys = _ffn(texp[:, 0], meta1, xs, expert_w_gu[l], expert_b_gu[l][:, None, :],
              expert_w_down[l], expert_b_down[l][:, None, :])
    out_ctx, out_lat = _combine(slot2d, x1, w4, mod3, norm_final_w[None, :], ys)
    y_prompt = out_ctx.reshape(N_CTX_SEQ, T_CTX, D)
    y_sample = out_lat.reshape(N_LAT_SEQ, T_LAT, D)
    return (y_prompt, y_sample, _pair_unblock(sf), _pair_unblock(sb))
```

```python
import functools
import math

import jax
import jax.numpy as jnp
from jax import lax
from jax.experimental import pallas as pl
from jax.experimental.pallas import tpu as pltpu

F32 = jnp.float32
BF16 = jnp.bfloat16

D = 1024
N_CTX_SEQ, T_CTX = 32, 256
N_LAT_SEQ, T_LAT = 4, 1024
GRID_W = 64
N_CTX_TOK = N_CTX_SEQ * T_CTX
N_LAT_TOK = N_LAT_SEQ * T_LAT
N_TOK = N_CTX_TOK + N_LAT_TOK

POOL_W = 512
POOL_G = 4
POOL_GD = POOL_W // POOL_G
POOL_WINDOWS = (2, 4, 8, 16)
RW = 512
HEAD = 64
N_HEAD = RW // HEAD
N_PAIR = N_HEAD // 2
LORA = 64
GATE_LORA = 128
N_EXP = 32
TOP_K = 4
D_FF = 1024
SWIGLU_ALPHA = 1.702
SWIGLU_LIMIT = 7.0
RMS_EPS = 1e-6
GN_EPS = 1e-5 * HEAD
N_MOD = 6

OFF_R = POOL_W
SHIFT_COLS = 3 * RW + 4 * LORA + GATE_LORA
OFF_MERGE = OFF_R + SHIFT_COLS
IN_COLS = OFF_MERGE + 2 * D

LANES = 128
CHUNK = 64
INV_BLOCK = 8
SCAN_TOKENS_PER_STEP = 512
ROWS = 256
HALO = 64
EXT = ROWS + 2 * HALO
TM_PROJ = 512
TM_POST = 512
TM_DISP = 1024
TM_SLOT = 512
FFN_HIDDEN_CHUNK = 512
N_SLOT_TILES = N_TOK * TOP_K // TM_SLOT + N_EXP
N_SLOTS = N_SLOT_TILES * TM_SLOT
RANK_BITS = 14
META_NT = N_EXP
MOD_ROWS = 8
VMEM_LIMIT = 56 * 1024 * 1024


def _sigmoid(x):
    return 1.0 / (1.0 + jnp.exp(-x))


def _split2(a):
    hi = a.astype(BF16)
    lo = (a - hi.astype(F32)).astype(BF16)
    return hi, lo


_NN = (((1,), (0,)), ((), ()))
_NT = (((1,), (1,)), ((), ()))


def _dot(a, b, dims=_NN):
    return lax.dot_general(a, b, dims, preferred_element_type=F32)


def _dot3(a, b, dims=_NN):
    ah, al = _split2(a)
    bh, bl = _split2(b)
    return _dot(ah, bh, dims) + (_dot(ah, bl, dims) + _dot(al, bh, dims))


def _dot_exact_lhs(a_bf16, b, passes):
    acc = None
    rem = b
    for _ in range(passes):
        part = rem.astype(BF16)
        term = _dot(a_bf16, part)
        acc = term if acc is None else acc + term
        rem = rem - part.astype(F32)
    return acc


def _dot_exact_rhs(a, b_bf16, passes):
    acc = None
    rem = a
    for _ in range(passes):
        part = rem.astype(BF16)
        term = _dot(part, b_bf16)
        acc = term if acc is None else acc + term
        rem = rem - part.astype(F32)
    return acc


def _modnorm(x, w, scale, shift):
    ms = jnp.mean(x * x, axis=-1, keepdims=True)
    return x * lax.rsqrt(ms + RMS_EPS) * w * (1.0 + scale) + shift


def _mod_row(i, tm):
    n_ctx = N_CTX_TOK // tm
    per = T_LAT // tm
    return jnp.where(i < n_ctx, 0, 1 + (i - n_ctx) // per)


def _mod_kernel(c_ref, w_ref, b_ref, o_ref):
    c = c_ref[...]
    s = c * _sigmoid(c)
    o_ref[...] = _dot3(s, w_ref[...]) + b_ref[...]


def _modulation(cvec, w_mod, b_mod):
    return pl.pallas_call(
        _mod_kernel,
        out_shape=jax.ShapeDtypeStruct((MOD_ROWS, N_MOD * D), F32),
        grid=(N_MOD,),
        in_specs=[
            pl.BlockSpec((MOD_ROWS, D), lambda j: (0, 0)),
            pl.BlockSpec((D, D), lambda j: (0, j)),
            pl.BlockSpec((1, D), lambda j: (0, j)),
        ],
        out_specs=pl.BlockSpec((MOD_ROWS, D), lambda j: (0, j)),
        compiler_params=pltpu.CompilerParams(dimension_semantics=("arbitrary",), vmem_limit_bytes=VMEM_LIMIT),
        name="mod",
    )(cvec, w_mod, b_mod)


def _path_specs(tm, n_cols=D):
    n_ctx = N_CTX_TOK // tm
    return [pl.BlockSpec((tm, n_cols), lambda i: (jnp.minimum(i, n_ctx - 1), 0)),
            pl.BlockSpec((tm, n_cols), lambda i: (jnp.maximum(i - n_ctx, 0), 0))]


def _path_tile(c_ref, l_ref, tm):
    return jnp.where(pl.program_id(0) < N_CTX_TOK // tm, c_ref[...], l_ref[...])


def _proj_kernel(xc_ref, xl_ref, mod_ref, nw_ref, w_ref, o_ref):
    mod = mod_ref[0]
    hn = _modnorm(_path_tile(xc_ref, xl_ref, TM_PROJ), nw_ref[...], mod[:, D:2 * D], mod[:, 0:D])
    o_ref[...] = _dot(hn.astype(BF16), w_ref[...])


def _projection(x_ctx, x_lat, mod3, norm_w, w_a):
    n_cols = w_a.shape[1]
    return pl.pallas_call(
        _proj_kernel,
        out_shape=jax.ShapeDtypeStruct((N_TOK, n_cols), F32),
        grid=(N_TOK // TM_PROJ,),
        in_specs=_path_specs(TM_PROJ) + [
            pl.BlockSpec((1, 1, N_MOD * D), lambda i: (_mod_row(i, TM_PROJ), 0, 0)),
            pl.BlockSpec((1, D), lambda i: (0, 0)),
            pl.BlockSpec((D, n_cols), lambda i: (0, 0)),
        ],
        out_specs=pl.BlockSpec((TM_PROJ, n_cols), lambda i: (i, 0)),
        compiler_params=pltpu.CompilerParams(dimension_semantics=("arbitrary",), vmem_limit_bytes=VMEM_LIMIT),
        name="proj",
    )(x_ctx, x_lat, mod3, norm_w, w_a)


N_FRONT_IN = 14
N_FRONT_OUT = 12


def _front_kernel(grid_mode, n_chunks, seq_len, *refs):
    (cur_ref, prev_ref, next_ref, poolw_ref, pscale_ref, mu_ref,
     dw0_ref, dw2_ref, a0_ref, a2_ref, kk_ref, ka_ref, rk_ref, seg_ref) = refs[:N_FRONT_IN]
    (ypre_ref, gsig_ref, r_ref, v_ref, kkn_ref, bonus_ref,
     lwf_ref, lwb_ref, kdf_ref, kdb_ref, bf_ref, bb_ref) = refs[-N_FRONT_OUT:]
    c = pl.program_id(1)
    cur = cur_ref[...]
    prev = jnp.where(c > 0, prev_ref[...], 0.0)
    nxt = jnp.where(c < n_chunks - 1, next_ref[...], 0.0)
    ext = jnp.concatenate([prev, cur, nxt], axis=0)

    def down(x, s):
        return pltpu.roll(x, s, 0)

    def up(x, s):
        return pltpu.roll(x, EXT - s, 0)

    def mid(x):
        return x[HALO:HALO + ROWS]

    row = lax.broadcasted_iota(jnp.int32, (ROWS, 1), 0)
    t_seq = c * ROWS + row

    outs = []
    for gi, win in enumerate(POOL_WINDOWS):
        h = win // 2
        u = ext[:, gi * POOL_GD:(gi + 1) * POOL_GD]
        trail, lead, s = u, u, 1
        while s < h:
            trail = trail + down(trail, s)
            lead = lead + up(lead, s)
            s *= 2
        total = mid(down(trail, 1) + lead)
        cnt = (jnp.minimum(t_seq + h, seq_len) - jnp.maximum(t_seq - h, 0)).astype(F32)
        pooled = total / cnt - mid(u)
        outs.append(_dot(pooled.astype(BF16), poolw_ref[gi]))
    mixed = jnp.concatenate(outs, axis=1) * pscale_ref[...]
    ypre_ref[...] = mixed.astype(BF16)

    pe = ext[:, OFF_R:OFF_MERGE]
    p = mid(pe)
    lane = lax.broadcasted_iota(jnp.int32, (1, SHIFT_COLS), 1)
    if grid_mode:
        col = t_seq % GRID_W
        left = jnp.where(col > 0, mid(down(pe, 1)), 0.0)
        right = jnp.where(col < GRID_W - 1, mid(up(pe, 1)), 0.0)
        upn = pe[0:ROWS]
        dnn = pe[2 * HALO:2 * HALO + ROWS]
        q = lane % 4
        sh = jnp.where(q == 0, left, jnp.where(q == 1, right, jnp.where(q == 2, upn, dnn)))
    else:
        sh = jnp.where(lane % 2 == 0, mid(down(pe, 1)), mid(up(pe, 1)))
    pm = p + mu_ref[...] * (sh - p)

    r = pm[:, 0:RW]
    k = pm[:, RW:2 * RW]
    v = pm[:, 2 * RW:3 * RW]
    o = 3 * RW
    w_lo = jnp.tanh(pm[:, o:o + 2 * LORA])
    a_lo = pm[:, o + 2 * LORA:o + 4 * LORA]
    gsig_ref[...] = _sigmoid(pm[:, o + 4 * LORA:o + 4 * LORA + GATE_LORA]).astype(BF16)

    seg = seg_ref[...]
    kkr = k * kk_ref[...]
    ss = _dot_exact_rhs(kkr * kkr, seg, 2)
    kkn = kkr / jnp.maximum(jnp.sqrt(ss), 1e-12)
    r_ref[...] = r
    v_ref[...] = v
    kkn_ref[...] = kkn

    bonus = None
    for d, (lw_ref, kd_ref, b_ref) in enumerate(((lwf_ref, kdf_ref, bf_ref), (lwb_ref, kdb_ref, bb_ref))):
        z = dw0_ref[d:d + 1, :] + _dot3(w_lo, dw2_ref[d])
        lw_ref[...] = -math.exp(-0.5) * _sigmoid(z)
        a = _sigmoid(a0_ref[d:d + 1, :] + _dot3(a_lo, a2_ref[d]))
        kd = k * (1.0 + (a - 1.0) * ka_ref[...])
        kd_ref[...] = kd
        b_ref[...] = kkn * a
        bd = _dot_exact_rhs(r * kd * rk_ref[...], seg, 2) * v
        bonus = bd if bonus is None else bonus + bd
    bonus_ref[...] = bonus


def _front(proj, grid_mode, n_seq, seq_len, tok0, weights):
    n_chunks = seq_len // ROWS
    n_tok = n_seq * seq_len
    blk0 = tok0 // ROWS
    hpb = ROWS // HALO
    last_halo = N_TOK // HALO - 1

    def cur_map(b, c):
        return (blk0 + b * n_chunks + c, 0)

    def prev_map(b, c):
        return (jnp.maximum((blk0 + b * n_chunks + c) * hpb - 1, 0), 0)

    def next_map(b, c):
        return (jnp.minimum((blk0 + b * n_chunks + c + 1) * hpb, last_halo), 0)

    def full(a):
        nd = a.ndim
        return pl.BlockSpec(a.shape, lambda b, c, _nd=nd: (0,) * _nd)

    def out_map(b, c):
        return (b * n_chunks + c, 0)

    f32_out = jax.ShapeDtypeStruct((n_tok, RW), F32)
    out_shape = [jax.ShapeDtypeStruct((n_tok, POOL_W), BF16), jax.ShapeDtypeStruct((n_tok, GATE_LORA), BF16)]
    out_shape += [f32_out] * (N_FRONT_OUT - 2)
    out_specs = [pl.BlockSpec((ROWS, POOL_W), out_map), pl.BlockSpec((ROWS, GATE_LORA), out_map)]
    out_specs += [pl.BlockSpec((ROWS, RW), out_map)] * (N_FRONT_OUT - 2)
    assert len(weights) == N_FRONT_IN - 3
    return pl.pallas_call(
        functools.partial(_front_kernel, grid_mode, n_chunks, seq_len),
        out_shape=out_shape,
        grid=(n_seq, n_chunks),
        in_specs=[
            pl.BlockSpec((ROWS, OFF_MERGE), cur_map),
            pl.BlockSpec((HALO, OFF_MERGE), prev_map),
            pl.BlockSpec((HALO, OFF_MERGE), next_map),
        ] + [full(a) for a in weights],
        out_specs=out_specs,
        compiler_params=pltpu.CompilerParams(dimension_semantics=("arbitrary", "arbitrary"),
                                             vmem_limit_bytes=VMEM_LIMIT),
        name="front_grid" if grid_mode else "front_ctx",
    )(proj, proj, proj, *weights)


def _scan_kernel(n_chunks, seqs_per_step, has_init, *refs):
    (r_ref, v_ref, kk_ref, bonus_ref, lwf_ref, lwb_ref, kdf_ref, kdb_ref, bf_ref, bb_ref) = refs[:10]
    if has_init:
        s0f_ref, s0b_ref = refs[10:12]
    y_ref, sf_ref, sb_ref = refs[-3:]
    C = CHUNK
    P2 = 2 * C
    ri = lax.broadcasted_iota(jnp.int32, (P2, P2), 0)
    ci = lax.broadcasted_iota(jnp.int32, (P2, P2), 1)
    same = (ri // C) == (ci // C)
    rp, cp = ri % C, ci % C
    eye = (ri == ci).astype(F32)
    blk = {}
    s = INV_BLOCK
    while s <= C:
        blk[s] = (ri // s) == (ci // s)
        s *= 2
    ti = lax.broadcasted_iota(jnp.int32, (C, C), 0)
    tj = lax.broadcasted_iota(jnp.int32, (C, C), 1)
    head_a = lax.broadcasted_iota(jnp.int32, (C, LANES), 1) < HEAD

    def stack(x):
        return jnp.concatenate([jnp.where(head_a, x, 0.0), jnp.where(head_a, 0.0, x)], axis=0)

    def d3(a, b, dims=_NN):
        return _dot(a[0], b[0], dims) + (_dot(a[0], b[1], dims) + _dot(a[1], b[0], dims))

    def d2(a_bf16, b):
        return _dot(a_bf16, b[0]) + _dot(a_bf16, b[1])

    def chunks(chains):
        n = range(len(chains))
        masks, sls = [], []
        for (s_ref, q, p, row0, lw_ref, kd_ref, b_ref, reverse) in chains:
            if reverse:
                masks.append((same & (cp > rp), same & (cp >= rp), (tj >= ti), 0))
            else:
                masks.append((same & (cp < rp), same & (cp <= rp), (tj <= ti), C - 1))
            sls.append((pl.ds(row0, C), slice(p * LANES, (p + 1) * LANES)))
        lw = [chains[i][4][sls[i]] for i in n]
        cum = [_dot_exact_lhs(masks[i][2].astype(BF16), lw[i], 3) for i in n]
        e_i = [jnp.exp(cum[i]) for i in n]
        e_n = [jnp.exp(-cum[i]) for i in n]
        QR = [jnp.concatenate([stack(kk_ref[sls[i]] * jnp.exp(cum[i] - lw[i])),
                               stack(r_ref[sls[i]] * e_i[i])], axis=0).astype(BF16) for i in n]
        BK = [jnp.concatenate([stack(chains[i][6][sls[i]] * e_n[i]),
                               stack(chains[i][5][sls[i]] * e_n[i])], axis=0).astype(BF16) for i in n]
        v_st = [stack(v_ref[sls[i]]) for i in n]
        Vs = [v_st[i].astype(BF16) for i in n]
        S = [chains[i][0][chains[i][1], chains[i][2]] for i in n]
        G = [_dot(QR[i], BK[i], _NT) for i in n]
        L = [jnp.where(masks[i][0], G[i][0:P2, 0:P2], 0.0) for i in n]
        Lk = [jnp.where(masks[i][0], G[i][0:P2, P2:2 * P2], 0.0) for i in n]
        RBK = [jnp.concatenate([jnp.where(masks[i][1], G[i][P2:2 * P2, 0:P2], 0.0),
                                jnp.where(masks[i][1], G[i][P2:2 * P2, P2:2 * P2], 0.0)], axis=1) for i in n]
        L0 = [jnp.where(blk[INV_BLOCK], L[i], 0.0) for i in n]
        L0h = [L0[i].astype(BF16) for i in n]
        X = [eye - L0[i] for i in n]
        Pw = [_split2(_dot(L0h[i], L0h[i])) for i in n]
        X = [X[i] + d2(X[i].astype(BF16), Pw[i]) for i in n]
        s = 4
        while s < INV_BLOCK:
            Pw = [_split2(d3(Pw[i], Pw[i])) for i in n]
            X = [X[i] + d3(_split2(X[i]), Pw[i]) for i in n]
            s *= 2
        s = INV_BLOCK
        while s < C:
            Dm = [_split2(X[i]) for i in n]
            Loff = [jnp.where(blk[2 * s] & jnp.logical_not(blk[s]), L[i], 0.0).astype(BF16) for i in n]
            E = [d2(Loff[i], Dm[i]) for i in n]
            X = [X[i] - d3(Dm[i], _split2(E[i])) for i in n]
            s *= 2
        QRA = [_dot(QR[i], S[i].astype(BF16), _NT) for i in n]
        W = [QRA[i][0:P2] + _dot(Lk[i].astype(BF16), Vs[i]) for i in n]
        U = [-_dot(X[i].astype(BF16), W[i].astype(BF16)) for i in n]
        UV = [jnp.concatenate([U[i], v_st[i]], axis=0) for i in n]
        Ys = [QRA[i][P2:2 * P2] + _dot(RBK[i].astype(BF16), UV[i].astype(BF16)) for i in n]
        dS = [_dot(UV[i].T.astype(BF16), BK[i]) for i in n]
        for i in n:
            last = masks[i][3]
            chains[i][0][chains[i][1], chains[i][2]] = (S[i] + dS[i]) * e_i[i][last:last + 1, :]
            y_ref[sls[i]] += Ys[i][0:C] + Ys[i][C:P2]

    y_ref[...] = bonus_ref[...]
    if has_init:
        sf_ref[...] = s0f_ref[...]
        sb_ref[...] = s0b_ref[...]
    else:
        sf_ref[...] = jnp.zeros_like(sf_ref)
        sb_ref[...] = jnp.zeros_like(sb_ref)

    seq_len = n_chunks * C

    def body(c, carry):
        chains = []
        for q in range(seqs_per_step):
            rf = pl.multiple_of(q * seq_len + c * C, C)
            rb = pl.multiple_of(q * seq_len + (n_chunks - 1 - c) * C, C)
            for p in range(N_PAIR):
                chains.append((sf_ref, q, p, rf, lwf_ref, kdf_ref, bf_ref, False))
                chains.append((sb_ref, q, p, rb, lwb_ref, kdb_ref, bb_ref, True))
        chunks(chains)
        return carry

    lax.fori_loop(0, n_chunks, body, 0)


def _scan(arrs, n_seq, seq_len, init):
    n_tok = n_seq * seq_len
    n_chunks = seq_len // CHUNK
    sps = max(1, SCAN_TOKENS_PER_STEP // seq_len)
    tok_spec = pl.BlockSpec((sps * seq_len, RW), lambda b: (b, 0))
    st_spec = pl.BlockSpec((sps, N_PAIR, LANES, LANES), lambda b: (b, 0, 0, 0))
    st_shape = jax.ShapeDtypeStruct((n_seq, N_PAIR, LANES, LANES), F32)
    ins = list(arrs)
    in_specs = [tok_spec] * 10
    if init is not None:
        ins += list(init)
        in_specs += [st_spec, st_spec]
    return pl.pallas_call(
        functools.partial(_scan_kernel, n_chunks, sps, init is not None),
        out_shape=[jax.ShapeDtypeStruct((n_tok, RW), F32), st_shape, st_shape],
        grid=(n_seq // sps,),
        in_specs=in_specs,
        out_specs=[tok_spec, st_spec, st_spec],
        compiler_params=pltpu.CompilerParams(dimension_semantics=("arbitrary",), vmem_limit_bytes=VMEM_LIMIT),
        name="scan_init" if init is not None else "scan_zero",
    )(*ins)


def _post_kernel(xc_ref, xl_ref, mod_ref, yprec_ref, yprel_ref, gsigc_ref, gsigl_ref, yc_ref, yl_ref,
                 nmw_ref, wm_ref, bm_ref, wpo_ref, seg_ref, lnw_ref, lnb_ref, gw2_ref, wro_ref, wo_ref,
                 nfw_ref, rw_ref, rb_ref,
                 x1_ref, hn2_ref, code_ref, w4_ref, meta_ref, texp_ref, carry_ref):
    i = pl.program_id(0)

    @pl.when(i == 0)
    def _():
        carry_ref[...] = jnp.zeros_like(carry_ref)

    x = _path_tile(xc_ref, xl_ref, TM_POST)
    mod = mod_ref[0]
    shift1, scale1, gate1 = mod[:, 0:D], mod[:, D:2 * D], mod[:, 2 * D:3 * D]
    shift2, scale2 = mod[:, 3 * D:4 * D], mod[:, 4 * D:5 * D]
    hn = _modnorm(x, nmw_ref[...], scale1, shift1)
    merge = _sigmoid(_dot(hn.astype(BF16), wm_ref[...]) + bm_ref[...])
    y_pool = _dot(_path_tile(yprec_ref, yprel_ref, TM_POST), wpo_ref[...])

    y = _path_tile(yc_ref, yl_ref, TM_POST)
    seg = seg_ref[...]
    mu = _dot_exact_rhs(y, seg, 2) * (1.0 / HEAD)
    yc = y - mu
    var = _dot_exact_rhs(yc * yc, seg, 2) * (1.0 / HEAD)
    yn = yc * lax.rsqrt(var + GN_EPS) * lnw_ref[...] + lnb_ref[...]
    g = _dot(_path_tile(gsigc_ref, gsigl_ref, TM_POST), gw2_ref[...])
    y_rwkv = _dot((yn * g).astype(BF16), wro_ref[...])

    mixed = merge[:, 0:D] * y_pool + merge[:, D:2 * D] * y_rwkv
    x1 = x + gate1 * _dot(mixed.astype(BF16), wo_ref[...])
    x1_ref[...] = x1
    hn2 = _modnorm(x1, nfw_ref[...], scale2, shift2)
    hn2_ref[...] = hn2

    logits = _dot3(hn2, rw_ref[...]) + rb_ref[...]
    lane = lax.broadcasted_iota(jnp.int32, logits.shape, 1)
    work = logits
    sel = None
    top = None
    for j in range(TOP_K):
        m = jnp.max(work, axis=-1, keepdims=True)
        if j == 0:
            top = m
        idx = jnp.min(jnp.where(work == m, lane, LANES), axis=-1, keepdims=True)
        pick = lane == idx
        sel = pick if sel is None else (sel | pick)
        work = jnp.where(pick, -jnp.inf, work)
    e = jnp.where(sel, jnp.exp(logits - top), 0.0)
    comb = e / jnp.sum(e, axis=-1, keepdims=True)

    tm = logits.shape[0]
    sel_b = jnp.where(sel, 1.0, 0.0).astype(BF16)
    before = (lax.broadcasted_iota(jnp.int32, (tm, tm), 1) < lax.broadcasted_iota(jnp.int32, (tm, tm), 0))
    rank = carry_ref[...] + _dot(before.astype(BF16), sel_b)
    carry = carry_ref[...] + jnp.sum(sel_b.astype(F32), axis=0, keepdims=True)
    carry_ref[...] = carry
    lower_e = (lax.broadcasted_iota(jnp.int32, (LANES, LANES), 0) < lax.broadcasted_iota(jnp.int32, (LANES, LANES), 1))
    rowpos = _dot(sel_b, lower_e.astype(BF16))
    code = jnp.zeros(logits.shape, jnp.int32)
    w4 = jnp.zeros(logits.shape, F32)
    for k in range(TOP_K):
        mk = sel & (rowpos == float(k))
        ek = jnp.sum(jnp.where(mk, lane, 0), axis=-1, keepdims=True)
        rk = jnp.sum(jnp.where(mk, rank, 0.0), axis=-1, keepdims=True).astype(jnp.int32)
        wk = jnp.sum(jnp.where(mk, comb, 0.0), axis=-1, keepdims=True)
        code = jnp.where(lane == k, ek * (1 << RANK_BITS) + rk, code)
        w4 = jnp.where(lane == k, wk, w4)
    code_ref[...] = code
    w4_ref[...] = w4

    tiles = jnp.floor((carry + (TM_SLOT - 1)) * (1.0 / TM_SLOT))
    incl = (lax.broadcasted_iota(jnp.int32, (LANES, LANES), 0) <= lax.broadcasted_iota(jnp.int32, (LANES, LANES), 1))
    tiles8 = jnp.broadcast_to(tiles, (8, LANES)).astype(BF16)
    cum = _dot(tiles8, incl.astype(BF16))[0:1]
    lane1 = lax.broadcasted_iota(jnp.int32, (1, LANES), 1)
    offs = ((cum - tiles) * TM_SLOT).astype(jnp.int32)
    n_used = jnp.sum(jnp.where(lane1 == N_EXP - 1, cum, 0.0), axis=-1, keepdims=True).astype(jnp.int32)
    meta_ref[...] = jnp.where(lane1 == META_NT, n_used, jnp.where(lane1 < N_EXP, offs, 0))
    tile_id = lax.broadcasted_iota(jnp.int32, (N_SLOT_TILES, LANES), 0).astype(F32)
    done = jnp.where((lax.broadcasted_iota(jnp.int32, (N_SLOT_TILES, LANES), 1) < N_EXP) & (cum <= tile_id), 1, 0)
    texp = jnp.minimum(jnp.sum(done, axis=-1, keepdims=True), N_EXP - 1)
    texp_ref[...] = jnp.broadcast_to(texp, (N_SLOT_TILES, LANES))


def _post(x, mod3, ypre, gsig, y, weights):
    def tile(n, dt):
        return pl.BlockSpec((TM_POST, n), lambda i: (i, 0))

    def full(a):
        nd = a.ndim
        return pl.BlockSpec(a.shape, lambda i, _nd=nd: (0,) * _nd)

    def const(shape):
        return pl.BlockSpec(shape, lambda i: (0, 0))

    return pl.pallas_call(
        _post_kernel,
        out_shape=[jax.ShapeDtypeStruct((N_TOK, D), F32), jax.ShapeDtypeStruct((N_TOK, D), F32),
                   jax.ShapeDtypeStruct((N_TOK, LANES), jnp.int32), jax.ShapeDtypeStruct((N_TOK, LANES), F32),
                   jax.ShapeDtypeStruct((1, LANES), jnp.int32),
                   jax.ShapeDtypeStruct((N_SLOT_TILES, LANES), jnp.int32)],
        grid=(N_TOK // TM_POST,),
        in_specs=_path_specs(TM_POST) + [
            pl.BlockSpec((1, 1, N_MOD * D), lambda i: (_mod_row(i, TM_POST), 0, 0)),
        ] + _path_specs(TM_POST, POOL_W) + _path_specs(TM_POST, GATE_LORA) + _path_specs(TM_POST, RW)
        + [full(a) for a in weights],
        out_specs=[tile(D, F32), tile(D, F32), tile(LANES, jnp.int32), tile(LANES, F32),
                   const((1, LANES)), const((N_SLOT_TILES, LANES))],
        scratch_shapes=[pltpu.VMEM((1, LANES), F32)],
        compiler_params=pltpu.CompilerParams(dimension_semantics=("arbitrary",), vmem_limit_bytes=VMEM_LIMIT),
        name="post",
    )(*x, mod3, *ypre, *gsig, *y, *weights)


def _slots_kernel(code_ref, meta_ref, slot_ref):
    code = code_ref[...]
    meta = meta_ref[...]
    expert = code >> RANK_BITS
    off = jnp.zeros(code.shape, jnp.int32)
    for e in range(N_EXP):
        off = jnp.where(expert == e, meta[:, e:e + 1], off)
    slot_ref[...] = off + (code & ((1 << RANK_BITS) - 1))


def _slots(code_dense, meta):
    return pl.pallas_call(
        _slots_kernel,
        out_shape=jax.ShapeDtypeStruct(code_dense.shape, jnp.int32),
        grid=(1,),
        in_specs=[pl.BlockSpec(code_dense.shape, lambda i: (0, 0)), pl.BlockSpec((1, LANES), lambda i: (0, 0))],
        out_specs=pl.BlockSpec(code_dense.shape, lambda i: (0, 0)),
        compiler_params=pltpu.CompilerParams(dimension_semantics=("arbitrary",), vmem_limit_bytes=VMEM_LIMIT),
        name="slots",
    )(code_dense, meta)


def _load_slots(slot_ref, slot_smem, sem):
    cp = pltpu.make_async_copy(slot_ref.at[pl.program_id(0)], slot_smem, sem)
    cp.start()
    cp.wait()


def _dispatch_kernel(slot_ref, meta_ref, hn2_ref, xs_ref, slot_smem, zero_buf, sem_c, sem_z, sem):
    @pl.when(pl.program_id(0) == 0)
    def _():
        zero_buf[...] = jnp.zeros_like(zero_buf)

        def pad_tile(e):
            end = meta_ref[e + 1] if e + 1 < N_EXP else meta_ref[META_NT] * TM_SLOT
            start = pl.multiple_of(end - TM_SLOT, TM_SLOT)
            return end > meta_ref[e], pltpu.make_async_copy(zero_buf, xs_ref.at[pl.ds(start, TM_SLOT)], sem_z)

        def tail_tile(j):
            return pltpu.make_async_copy(zero_buf, xs_ref.at[pl.ds(pl.multiple_of(j * TM_SLOT, TM_SLOT), TM_SLOT)], sem_z)

        n_used = meta_ref[META_NT]
        for e in range(N_EXP):
            nonempty, cp = pad_tile(e)
            pl.when(nonempty)(cp.start)

        def start_tail(j, c):
            tail_tile(j).start()
            return c

        def wait_tail(j, c):
            tail_tile(j).wait()
            return c

        lax.fori_loop(n_used, N_SLOT_TILES, start_tail, 0)
        for e in range(N_EXP):
            nonempty, cp = pad_tile(e)
            pl.when(nonempty)(cp.wait)
        lax.fori_loop(n_used, N_SLOT_TILES, wait_tail, 0)

    _load_slots(slot_ref, slot_smem, sem_c)

    for t in range(TM_DISP):
        for k in range(TOP_K):
            slot = slot_smem[t * TOP_K + k]
            pltpu.make_async_copy(hn2_ref.at[pl.ds(t, 1)], xs_ref.at[pl.ds(slot, 1)], sem).start(priority=k % 2)
    for _ in range(TOP_K):
        pltpu.make_async_copy(hn2_ref, xs_ref.at[pl.ds(0, TM_DISP)], sem).wait()


def _dispatch(slot2d, meta, hn2):
    n_steps = N_TOK // TM_DISP
    return pl.pallas_call(
        _dispatch_kernel,
        out_shape=jax.ShapeDtypeStruct((N_SLOTS, D), F32),
        grid=(n_steps,),
        in_specs=[
            pl.BlockSpec(slot2d.shape, lambda i: (0, 0)),
            pl.BlockSpec(memory_space=pltpu.SMEM),
            pl.BlockSpec((TM_DISP, D), lambda i: (i, 0)),
        ],
        out_specs=pl.BlockSpec(memory_space=pl.ANY),
        scratch_shapes=[pltpu.SMEM((TM_DISP * TOP_K,), jnp.int32), pltpu.VMEM((TM_SLOT, D), F32),
                        pltpu.SemaphoreType.DMA, pltpu.SemaphoreType.DMA, pltpu.SemaphoreType.DMA],
        compiler_params=pltpu.CompilerParams(dimension_semantics=("arbitrary",), vmem_limit_bytes=VMEM_LIMIT),
        name="dispatch",
    )(slot2d, meta, hn2)


def _ffn_kernel(texp_ref, meta_ref, xs_ref, wgu_ref, bgu_ref, wd_ref, bd_ref, ys_ref, wgu_bf, wd_bf):
    i = pl.program_id(0)
    valid = i < meta_ref[META_NT]
    fresh = (i == 0) | (texp_ref[i] != texp_ref[jnp.maximum(i - 1, 0)])

    @pl.when(valid & fresh)
    def _():
        wgu_bf[...] = wgu_ref[0].astype(BF16)
        wd_bf[...] = wd_ref[0].astype(BF16)

    @pl.when(valid)
    def _():
        x = xs_ref[...].astype(BF16)
        bgu = bgu_ref[0]
        acc = None
        for h0 in range(0, D_FF, FFN_HIDDEN_CHUNK):
            h1 = h0 + FFN_HIDDEN_CHUNK
            glu = jnp.minimum(_dot(x, wgu_bf[:, h0:h1]) + bgu[:, h0:h1], SWIGLU_LIMIT)
            lin = jnp.clip(_dot(x, wgu_bf[:, D_FF + h0:D_FF + h1]) + bgu[:, D_FF + h0:D_FF + h1],
                           -SWIGLU_LIMIT, SWIGLU_LIMIT)
            act = glu * _sigmoid(SWIGLU_ALPHA * glu) * (lin + 1.0)
            part = _dot(act.astype(BF16), wd_bf[h0:h1, :])
            acc = part if acc is None else acc + part
        ys_ref[...] = acc + bd_ref[0]

    @pl.when(jnp.logical_not(valid))
    def _():
        ys_ref[...] = jnp.zeros_like(ys_ref)


def _ffn(texp, meta, xs, wgu, bgu, wd, bd):
    def used_tile(i, te, me):
        return (jnp.minimum(i, me[META_NT] - 1), 0)

    return pl.pallas_call(
        _ffn_kernel,
        out_shape=jax.ShapeDtypeStruct((N_SLOTS, D), F32),
        grid_spec=pltpu.PrefetchScalarGridSpec(
            num_scalar_prefetch=2,
            grid=(N_SLOT_TILES,),
            in_specs=[
                pl.BlockSpec((TM_SLOT, D), used_tile),
                pl.BlockSpec((1, D, 2 * D_FF), lambda i, te, me: (te[i], 0, 0)),
                pl.BlockSpec((1, 1, 2 * D_FF), lambda i, te, me: (te[i], 0, 0)),
                pl.BlockSpec((1, D_FF, D), lambda i, te, me: (te[i], 0, 0)),
                pl.BlockSpec((1, 1, D), lambda i, te, me: (te[i], 0, 0)),
            ],
            out_specs=pl.BlockSpec((TM_SLOT, D), lambda i, te, me: (i, 0)),
            scratch_shapes=[pltpu.VMEM((D, 2 * D_FF), BF16), pltpu.VMEM((D_FF, D), BF16)],
        ),
        compiler_params=pltpu.CompilerParams(dimension_semantics=("arbitrary",), vmem_limit_bytes=VMEM_LIMIT),
        name="ffn",
    )(texp, meta, xs, wgu, bgu, wd, bd)


def _combine_kernel(slot_ref, x1_ref, w4_ref, mod_ref, nfw_ref, ys_ref, oc_ref, ol_ref,
                    slot_smem, buf, sem_c, sem):
    _load_slots(slot_ref, slot_smem, sem_c)

    for t in range(TM_DISP):
        for k in range(TOP_K):
            slot = slot_smem[t * TOP_K + k]
            pltpu.make_async_copy(ys_ref.at[pl.ds(slot, 1)], buf.at[k, pl.ds(t, 1)], sem).start(priority=k % 2)
    for k in range(TOP_K):
        pltpu.make_async_copy(ys_ref.at[pl.ds(0, TM_DISP)], buf.at[k], sem).wait()
    w4 = w4_ref[...]
    moe = w4[:, 0:1] * buf[0]
    for k in range(1, TOP_K):
        moe = moe + w4[:, k:k + 1] * buf[k]
    gate2 = mod_ref[0][:, 5 * D:6 * D]
    x2 = x1_ref[...] + gate2 * moe
    ms = jnp.mean(x2 * x2, axis=-1, keepdims=True)
    out = x2 * lax.rsqrt(ms + RMS_EPS) * nfw_ref[...]
    is_ctx = pl.program_id(0) < N_CTX_TOK // TM_DISP

    @pl.when(is_ctx)
    def _():
        oc_ref[...] = out

    @pl.when(jnp.logical_not(is_ctx))
    def _():
        ol_ref[...] = out


def _combine(slot2d, x1, w4, mod3, nfw, ys):
    n_steps = N_TOK // TM_DISP
    n_ctx = N_CTX_TOK // TM_DISP
    return pl.pallas_call(
        _combine_kernel,
        out_shape=[jax.ShapeDtypeStruct((N_CTX_TOK, D), F32), jax.ShapeDtypeStruct((N_LAT_TOK, D), F32)],
        grid=(n_steps,),
        in_specs=[
            pl.BlockSpec(slot2d.shape, lambda i: (0, 0)),
            pl.BlockSpec((TM_DISP, D), lambda i: (i, 0)),
            pl.BlockSpec((TM_DISP, LANES), lambda i: (i, 0)),
            pl.BlockSpec((1, 1, N_MOD * D), lambda i: (_mod_row(i, TM_DISP), 0, 0)),
            pl.BlockSpec((1, D), lambda i: (0, 0)),
            pl.BlockSpec(memory_space=pl.ANY),
        ],
        out_specs=[pl.BlockSpec((TM_DISP, D), lambda i: (jnp.minimum(i, n_ctx - 1), 0)),
                   pl.BlockSpec((TM_DISP, D), lambda i: (jnp.maximum(i - n_ctx, 0), 0))],
        scratch_shapes=[pltpu.SMEM((TM_DISP * TOP_K,), jnp.int32), pltpu.VMEM((TOP_K, TM_DISP, D), F32),
                        pltpu.SemaphoreType.DMA, pltpu.SemaphoreType.DMA],
        compiler_params=pltpu.CompilerParams(dimension_semantics=("arbitrary",), vmem_limit_bytes=VMEM_LIMIT),
        name="combine",
    )(slot2d, x1, w4, mod3, nfw, ys)


def _pair_blockdiag(s):
    b = s.shape[0]
    s = s.reshape(b, N_PAIR, 2, HEAD, HEAD)
    z = jnp.zeros((b, N_PAIR, HEAD, HEAD), s.dtype)
    top = jnp.concatenate([s[:, :, 0], z], axis=-1)
    bot = jnp.concatenate([z, s[:, :, 1]], axis=-1)
    return jnp.concatenate([top, bot], axis=-2)


def _pair_unblock(s):
    b = s.shape[0]
    a = s[:, :, 0:HEAD, 0:HEAD]
    c = s[:, :, HEAD:, HEAD:]
    return jnp.stack([a, c], axis=2).reshape(b, 1, N_HEAD, HEAD, HEAD)


def kernel(x_prompt, x_sample, state_fwd, state_bwd, c, c_ctx, w_mod, b_mod, norm_mix_w, w_in, b_merge, pool_w, pool_scale, w_pool_out, shift_mu, decay_w0, decay_w2, iclr_a0, iclr_a2, gate_w2, k_k, k_a, r_k, ln_x_w, ln_x_b, w_rwkv_out, w_o, norm_ffn_w, router_w, router_b, expert_w_gu, expert_b_gu, expert_w_down, expert_b_down, norm_final_w):
    l = 0
    x_ctx = x_prompt.reshape(N_CTX_TOK, D)
    x_lat = x_sample.reshape(N_LAT_TOK, D)
    cvec = jnp.concatenate([c_ctx[None, :], c, jnp.zeros((MOD_ROWS - 1 - N_LAT_SEQ, D), F32)], axis=0)
    mod = _modulation(cvec, w_mod[l], b_mod[l][None, :])
    mod3 = mod.reshape(MOD_ROWS, 1, N_MOD * D)

    w_in_l = w_in[l]
    proj = _projection(x_ctx, x_lat, mod3, norm_mix_w[l][None, :], w_in_l[:, :OFF_MERGE].astype(BF16))

    zl = jnp.zeros((LORA, RW), F32)
    dw2 = jnp.stack([jnp.concatenate([decay_w2[l, 0], zl], 0), jnp.concatenate([zl, decay_w2[l, 1]], 0)])
    a2 = jnp.stack([jnp.concatenate([iclr_a2[l, 0], zl], 0), jnp.concatenate([zl, iclr_a2[l, 1]], 0)])
    hid = jnp.arange(RW) // HEAD
    seg = (hid[:, None] == hid[None, :]).astype(BF16)
    front_w = (pool_w[l].astype(BF16), pool_scale[l][None, :], shift_mu[l][None, :],
               decay_w0[l], dw2, iclr_a0[l], a2, k_k[l][None, :], k_a[l][None, :],
               r_k[l].reshape(1, RW), seg)
    f_ctx = _front(proj, False, N_CTX_SEQ, T_CTX, 0, front_w)
    f_lat = _front(proj, True, N_LAT_SEQ, T_LAT, N_CTX_TOK, front_w)

    y_ctx, sf, sb = _scan(f_ctx[2:], N_CTX_SEQ, T_CTX, None)
    init = (_pair_blockdiag(state_fwd[:, l]), _pair_blockdiag(state_bwd[:, l]))
    y_lat, _, _ = _scan(f_lat[2:], N_LAT_SEQ, T_LAT, init)

    rw_pad = jnp.concatenate([router_w[l], jnp.zeros((D, LANES - N_EXP), F32)], axis=1)
    rb_pad = jnp.concatenate([router_b[l], jnp.full((LANES - N_EXP,), -1e30, F32)])[None, :]
    post_w = (norm_mix_w[l][None, :], w_in_l[:, OFF_MERGE:].astype(BF16), b_merge[l][None, :],
              w_pool_out[l].astype(BF16), seg, ln_x_w[l][None, :], ln_x_b[l][None, :],
              gate_w2[l].astype(BF16), w_rwkv_out[l].astype(BF16), w_o[l].astype(BF16),
              norm_ffn_w[l][None, :], rw_pad, rb_pad)
    x1, hn2, code, w4, meta, texp = _post((x_ctx, x_lat), mod3, (f_ctx[0], f_lat[0]), (f_ctx[1], f_lat[1]),
                                          (y_ctx, y_lat), post_w)

    code_dense = code[:, :TOP_K].reshape(N_TOK * TOP_K // LANES, LANES)
    slot2d = _slots(code_dense, meta).reshape(N_TOK // TM_DISP, TM_DISP * TOP_K)
    meta1 = meta.reshape(LANES)
    xs = _dispatch(slot2d, meta1, hn2)
    ys = _ffn(texp[:, 0], meta1, xs, expert_w_gu[l], expert_b_gu[l][:, None, :],
              expert_w_down[l], expert_b_down[l][:, None, :])
    out_ctx, out_lat = _combine(slot2d, x1, w4, mod3, norm_final_w[None, :], ys)
    y_prompt = out_ctx.reshape(N_CTX_SEQ, T_CTX, D)
    y_sample = out_lat.reshape(N_LAT_SEQ, T_LAT, D)
    return (y_prompt, y_sample, _pair_unblock(sf), _pair_unblock(sb))
```
